```python
import jax, jax.numpy as jnp
import numpy as np

D_MODEL = 1024
BATCH = 8
SEQ = 4096
DEPTH = 2
DEC_BATCH = 128
DEC_SEQ = 1
PAST_LEN = 16384
PAGE_SIZE = 128

HEAD_DIM = 64
A_GROUPS = ((128, 1), (512, 4), (2048, 16))
A_HEADS_PER_GROUP = 4
A_HEADS = A_HEADS_PER_GROUP * len(A_GROUPS)
A_WIDTH = A_HEADS * HEAD_DIM
B_Q_HEADS = D_MODEL // HEAD_DIM
B_KV_HEADS = 2
B_WIDTH = B_Q_HEADS * HEAD_DIM
B_WINDOW = 128
BLOCK = 128
N_A = DEPTH // 2
N_B = DEPTH - N_A
ROPE_THETA = 10000.0
EPS = 1e-6
NEG = -1e30
ADA_STD = 0.5

kernel_name = "yoco_dilated_swa_sink_decoder_step"


def rmsnorm(x, g):
    xf = x.astype(jnp.float32)
    y = xf * jax.lax.rsqrt(jnp.mean(xf * xf, axis=-1, keepdims=True) + EPS) * g.astype(jnp.float32)
    return y.astype(x.dtype)


def modulate(h, shift, scale):
    return h * (1 + scale[:, None]) + shift[:, None]


def rope(x, pos):
    half = HEAD_DIM // 2
    inv = ROPE_THETA ** (-jnp.arange(half, dtype=jnp.float32) / half)
    ang = pos.astype(jnp.float32)[:, None] * inv[None, :]
    cos = jnp.cos(ang)[None, :, None, :]
    sin = jnp.sin(ang)[None, :, None, :]
    xf = x.astype(jnp.float32)
    x1, x2 = xf[..., :half], xf[..., half:]
    return jnp.concatenate([x1 * cos - x2 * sin, x2 * cos + x1 * sin], axis=-1).astype(x.dtype)


def masked_softmax_stats(s, mask):
    s = jnp.where(mask, s, NEG)
    m = jnp.max(s, axis=-1)
    p = jnp.exp(s - m[..., None])
    return p, jnp.sum(p, axis=-1), m


def sink_softmax(s, mask, sink):
    s = jnp.where(mask, s, NEG)
    m = jnp.maximum(jnp.max(s, axis=-1), sink)
    p = jnp.exp(s - m[..., None])
    return p, jnp.sum(p, axis=-1) + jnp.exp(sink - m)


def dilated_attn_prompt(q, k, v, dil, win):
    B, T, H, hd = q.shape
    M = T // dil
    Mp = -(-M // BLOCK) * BLOCK
    nb = Mp // BLOCK

    def to_blocks(a):
        a = a.reshape(B, M, dil, H, hd).transpose(0, 2, 1, 3, 4)
        a = jnp.pad(a, ((0, 0), (0, 0), (0, Mp - M), (0, 0), (0, 0)))
        return a.reshape(B, dil, nb, BLOCK, H, hd)

    def with_prev(a):
        prev = jnp.pad(a[:, :, :-1], ((0, 0), (0, 0), (1, 0), (0, 0), (0, 0), (0, 0)))
        return jnp.concatenate([prev, a], axis=3)

    qb = to_blocks(q)
    kk = with_prev(to_blocks(k))
    vv = with_prev(to_blocks(v))
    s = jnp.einsum('brnqhd,brnkhd->brnhqk', qb, kk, preferred_element_type=jnp.float32) * hd ** -0.5
    qi = jnp.arange(BLOCK)[:, None]
    kj = jnp.arange(2 * BLOCK)[None, :]
    dist = qi + BLOCK - kj
    kpos = (jnp.arange(nb) * BLOCK - BLOCK)[:, None, None] + kj[None]
    mask = ((dist >= 0) & (dist <= win // dil))[None] & (kpos >= 0)
    p, l, m = masked_softmax_stats(s, mask[:, None])
    o = jnp.einsum('brnhqk,brnkhd->brnqhd', p, vv.astype(jnp.float32)) / jnp.swapaxes(l, -1, -2)[..., None]
    lse = jnp.swapaxes(m + jnp.log(l), -1, -2)
    o = o.reshape(B, dil, Mp, H, hd)[:, :, :M].transpose(0, 2, 1, 3, 4).reshape(B, T, H, hd)
    lse = lse.reshape(B, dil, Mp, H)[:, :, :M].transpose(0, 2, 1, 3).reshape(B, T, H)
    return o, lse


def dilated_attn_sample(q, k, v, kv_cache, dil, win):
    L = kv_cache.shape[1]
    S = q.shape[1]
    ext = jnp.concatenate([kv_cache.astype(k.dtype), jnp.stack([k, v], axis=2)], axis=1)
    n = win // dil + 1
    idx = L + jnp.arange(S)[:, None] - dil * jnp.arange(n)[None, :]
    valid = idx >= 0
    g = ext[:, jnp.maximum(idx, 0)]
    s = jnp.einsum('bshd,bsnhd->bshn', q, g[:, :, :, 0], preferred_element_type=jnp.float32) * HEAD_DIM ** -0.5
    p, l, m = masked_softmax_stats(s, valid[None, :, None, :])
    o = jnp.einsum('bshn,bsnhd->bshd', p, g[:, :, :, 1].astype(jnp.float32)) / l[..., None]
    keep = min(win, L + S)
    return o, m + jnp.log(l), ext[:, L + S - keep:]


def swa_sink_prompt(q, k, v, sinks):
    B, T, HQ, hd = q.shape
    HKV = k.shape[2]
    G = HQ // HKV
    nb = T // BLOCK
    qb = q.reshape(B, nb, BLOCK, HKV, G, hd)

    def with_prev(a):
        a = a.reshape(B, nb, BLOCK, HKV, hd)
        prev = jnp.pad(a[:, :-1], ((0, 0), (1, 0), (0, 0), (0, 0), (0, 0)))
        return jnp.concatenate([prev, a], axis=2)

    kk, vv = with_prev(k), with_prev(v)
    s = jnp.einsum('bnqkgd,bnjkd->bnkgqj', qb, kk, preferred_element_type=jnp.float32) * hd ** -0.5
    qi = jnp.arange(BLOCK)[:, None]
    kj = jnp.arange(2 * BLOCK)[None, :]
    dist = qi + BLOCK - kj
    kpos = (jnp.arange(nb) * BLOCK - BLOCK)[:, None, None] + kj[None]
    mask = ((dist >= 0) & (dist < B_WINDOW))[None] & (kpos >= 0)
    sink = sinks.astype(jnp.float32).reshape(HKV, G, 1)
    p, den = sink_softmax(s, mask[:, None, None], sink)
    o = jnp.einsum('bnkgqj,bnjkd->bnqkgd', p, vv.astype(jnp.float32)) / jnp.moveaxis(den, -1, 2)[..., None]
    return o.reshape(B, T, HQ * hd)


def swa_sink_sample(q, kv_ext, n_past, sinks):
    B, S, HQ, hd = q.shape
    HKV = kv_ext.shape[3]
    G = HQ // HKV
    N = kv_ext.shape[1]
    qg = q.reshape(B, S, HKV, G, hd)
    s = jnp.einsum('bskgd,bjkd->bkgsj', qg, kv_ext[:, :, 0], preferred_element_type=jnp.float32) * hd ** -0.5
    dist = n_past + jnp.arange(S)[:, None] - jnp.arange(N)[None, :]
    mask = (dist >= 0) & (dist < B_WINDOW)
    sink = sinks.astype(jnp.float32).reshape(HKV, G, 1)
    p, den = sink_softmax(s, mask, sink)
    o = jnp.einsum('bkgsj,bjkd->bskgd', p, kv_ext[:, :, 1].astype(jnp.float32)) / jnp.moveaxis(den, -1, 1)[..., None]
    return o.reshape(B, S, HQ * hd)


def trunk(x, c, pos, a_caches, b_cache, ada_w, ada_b, g_pre, g_post, w_in_a, w_o_a,
          w_in_b, w_o_b, sinks_b, ada_kv_w, ada_kv_b, g_kv, w_kv):
    B, T, _ = x.shape
    new_a = [[] for _ in A_GROUPS]
    new_b = None
    kv_ext, n_past, kb, vb = None, 0, None, None
    for layer in range(DEPTH):
        shift, scale, gate = jnp.split(jax.nn.silu(c) @ ada_w[layer] + ada_b[layer], 3, axis=-1)
        h = modulate(rmsnorm(x, g_pre[layer]), shift, scale)
        if layer < N_A:
            i = layer
            q, k, v, z = jnp.split(h @ w_in_a[i], 4, axis=-1)
            q = rope(q.reshape(B, T, A_HEADS, HEAD_DIM), pos)
            k = rope(k.reshape(B, T, A_HEADS, HEAD_DIM), pos)
            v = v.reshape(B, T, A_HEADS, HEAD_DIM)
            outs, lses = [], []
            for g, (win, dil) in enumerate(A_GROUPS):
                sl = slice(g * A_HEADS_PER_GROUP, (g + 1) * A_HEADS_PER_GROUP)
                if a_caches is None:
                    o, lse = dilated_attn_prompt(q[:, :, sl], k[:, :, sl], v[:, :, sl], dil, win)
                    st = jnp.stack([k[:, :, sl], v[:, :, sl]], axis=2)[:, T - min(win, T):]
                else:
                    o, lse, st = dilated_attn_sample(q[:, :, sl], k[:, :, sl], v[:, :, sl], a_caches[g][i], dil, win)
                outs.append(o)
                lses.append(lse)
                new_a[g].append(st)
            alpha = jax.nn.softmax(jnp.stack(lses, axis=2), axis=2)
            o = (jnp.stack(outs, axis=2) * alpha[..., None]).reshape(B, T, A_WIDTH)
            mix = (o.astype(x.dtype) * jax.nn.silu(z)) @ w_o_a[i]
        else:
            j = layer - N_A
            if layer == N_A:
                kv_shift, kv_scale = jnp.split(jax.nn.silu(c) @ ada_kv_w + ada_kv_b, 2, axis=-1)
                hk = modulate(rmsnorm(x, g_kv), kv_shift, kv_scale)
                kvp = (hk @ w_kv).reshape(B, T, 2, B_KV_HEADS, HEAD_DIM)
                kb = rope(kvp[:, :, 0], pos)
                vb = kvp[:, :, 1]
                kv_new = jnp.stack([kb, vb], axis=2)
                if b_cache is None:
                    kv_ext, n_past = kv_new, 0
                else:
                    kv_ext = jnp.concatenate([b_cache.astype(kv_new.dtype), kv_new], axis=1)
                    n_past = b_cache.shape[1]
                n_ext = kv_ext.shape[1]
                new_b = kv_ext[:, n_ext - min(B_WINDOW, n_ext):]
            q, z = jnp.split(h @ w_in_b[j], 2, axis=-1)
            q = rope(q.reshape(B, T, B_Q_HEADS, HEAD_DIM), pos)
            if b_cache is None:
                o = swa_sink_prompt(q, kb, vb, sinks_b[j])
            else:
                o = swa_sink_sample(q, kv_ext, n_past, sinks_b[j])
            mix = (o.astype(x.dtype) * jax.nn.silu(z)) @ w_o_b[j]
        x = x + gate[:, None] * rmsnorm(mix, g_post[layer])
    return x, [jnp.stack(s, axis=0) for s in new_a], new_b


def setup_inputs(seed: int = 0) -> dict:
    key = jax.random.key(seed)
    ks = jax.random.split(key, 24)
    D = D_MODEL

    def nrm(k, shape, std):
        return jax.random.normal(k, shape, jnp.float32) * std

    inputs = {
        "x_prompt": nrm(ks[0], (BATCH, SEQ, D), 1.0),
        "x_sample": nrm(ks[1], (DEC_BATCH, DEC_SEQ, D), 1.0),
        "c_prompt": nrm(ks[2], (BATCH, D), 1.0),
        "c_sample": nrm(ks[3], (DEC_BATCH, D), 1.0),
        "cache_a_kv_g0": nrm(ks[4], (N_A, DEC_BATCH, min(A_GROUPS[0][0], PAST_LEN), 2, A_HEADS_PER_GROUP, HEAD_DIM), 1.0),
        "cache_a_kv_g1": nrm(ks[5], (N_A, DEC_BATCH, min(A_GROUPS[1][0], PAST_LEN), 2, A_HEADS_PER_GROUP, HEAD_DIM), 1.0),
        "cache_a_kv_g2": nrm(ks[6], (N_A, DEC_BATCH, min(A_GROUPS[2][0], PAST_LEN), 2, A_HEADS_PER_GROUP, HEAD_DIM), 1.0),
        "cache_b_kv": nrm(ks[7], (DEC_BATCH, min(B_WINDOW, PAST_LEN), 2, B_KV_HEADS, HEAD_DIM), 1.0),
        "ada_w": nrm(ks[8], (DEPTH, D, 3 * D), ADA_STD * D ** -0.5),
        "ada_b": nrm(ks[9], (DEPTH, 3 * D), 0.02),
        "g_pre": 1.0 + nrm(ks[10], (DEPTH, D), 0.02),
        "g_post": 1.0 + nrm(ks[11], (DEPTH, D), 0.02),
        "w_in_a": nrm(ks[12], (N_A, D, 4 * A_WIDTH), D ** -0.5),
        "w_o_a": nrm(ks[13], (N_A, A_WIDTH, D), A_WIDTH ** -0.5),
        "w_in_b": nrm(ks[14], (N_B, D, 2 * B_WIDTH), D ** -0.5),
        "w_o_b": nrm(ks[15], (N_B, B_WIDTH, D), B_WIDTH ** -0.5),
        "sinks_b": nrm(ks[16], (N_B, B_Q_HEADS), 1.0),
        "ada_kv_w": nrm(ks[17], (D, 2 * D), ADA_STD * D ** -0.5),
        "ada_kv_b": nrm(ks[18], (2 * D,), 0.02),
        "g_kv": 1.0 + nrm(ks[19], (D,), 0.02),
        "w_kv": nrm(ks[20], (D, 2 * B_KV_HEADS * HEAD_DIM), D ** -0.5),
    }
    return inputs


def reference(x_prompt, x_sample, c_prompt, c_sample, cache_a_kv_g0, cache_a_kv_g1, cache_a_kv_g2,
              cache_b_kv, ada_w, ada_b, g_pre, g_post, w_in_a, w_o_a, w_in_b, w_o_b, sinks_b,
              ada_kv_w, ada_kv_b, g_kv, w_kv):
    pos_prompt = jnp.arange(x_prompt.shape[1])
    pos_sample = PAST_LEN + jnp.arange(x_sample.shape[1])
    y_prompt, a_p, b_p = trunk(x_prompt, c_prompt, pos_prompt, None, None,
                               ada_w, ada_b, g_pre, g_post, w_in_a, w_o_a, w_in_b, w_o_b,
                               sinks_b, ada_kv_w, ada_kv_b, g_kv, w_kv)
    y_sample, a_s, b_s = trunk(x_sample, c_sample, pos_sample,
                               (cache_a_kv_g0, cache_a_kv_g1, cache_a_kv_g2), cache_b_kv,
                               ada_w, ada_b, g_pre, g_post, w_in_a, w_o_a, w_in_b, w_o_b,
                               sinks_b, ada_kv_w, ada_kv_b, g_kv, w_kv)
    return (y_prompt, y_sample, a_p[0], a_p[1], a_p[2], b_p, a_s[0], a_s[1], a_s[2], b_s)
```

```python
import functools

import jax
import jax.numpy as jnp
from jax import lax
from jax.experimental import pallas as pl
from jax.experimental.pallas import tpu as pltpu

D_MODEL = 1024
HEAD_DIM = 64
A_GROUPS = ((128, 1), (512, 4), (2048, 16))
A_HEADS_PER_GROUP = 4
GROUP_WIDTH = A_HEADS_PER_GROUP * HEAD_DIM
A_WIDTH = GROUP_WIDTH * len(A_GROUPS)
B_Q_HEADS = 16
B_KV_HEADS = 2
B_GROUP = B_Q_HEADS // B_KV_HEADS
B_WIDTH = B_Q_HEADS * HEAD_DIM
B_KV_WIDTH = B_KV_HEADS * HEAD_DIM
B_WINDOW = 128
BLOCK = 128
PAST_LEN = 16384
ROPE_THETA = 10000.0
EPS = 1e-6
NEG = -1e30
LANES = 128
Q_SCALE = HEAD_DIM ** -0.5

BF16 = jnp.bfloat16
F32 = jnp.float32
TOKEN_TILE = 512

_ARB = lambda n: pltpu.CompilerParams(dimension_semantics=("arbitrary",) * n)


def _rope(x, cos, sin_signed):
    lane = lax.broadcasted_iota(jnp.int32, (1, LANES), 1)
    first_half = (lane % HEAD_DIM) < (HEAD_DIM // 2)
    outs = []
    for j in range(x.shape[1] // LANES):
        xs = x[:, j * LANES:(j + 1) * LANES]
        partner = jnp.where(first_half, pltpu.roll(xs, LANES - 32, 1), pltpu.roll(xs, 32, 1))
        outs.append(xs * cos + partner * sin_signed)
    return outs[0] if len(outs) == 1 else jnp.concatenate(outs, axis=1)


def _normalize(x):
    return x * lax.rsqrt(jnp.mean(x * x, axis=-1, keepdims=True) + EPS)


def _dot(a, b):
    return jnp.dot(a, b, preferred_element_type=F32)


def _ada_kernel(c_ref, w_ref, b_ref, o_ref):
    s = jax.nn.silu(c_ref[...]).astype(BF16)
    o_ref[...] = _dot(s, w_ref[...].astype(BF16)) + b_ref[...]


def _ada(c, w, b):
    nl, _, n = w.shape
    m = c.shape[0]
    tn = 1024
    return pl.pallas_call(
        _ada_kernel,
        grid=(nl, n // tn),
        in_specs=[
            pl.BlockSpec((m, D_MODEL), lambda l, j: (0, 0)),
            pl.BlockSpec((None, D_MODEL, tn), lambda l, j: (l, 0, j)),
            pl.BlockSpec((None, 1, tn), lambda l, j: (l, 0, j)),
        ],
        out_specs=pl.BlockSpec((None, m, tn), lambda l, j: (l, 0, j)),
        out_shape=jax.ShapeDtypeStruct((nl, m, n), F32),
        compiler_params=_ARB(2),
        name="ada",
    )(c, w, b.reshape(nl, 1, n))


def _inproj_a_body(x_ref, sh_ref, sc_ref, g_ref, w_ref, cos_ref, sin_ref):
    h = _normalize(x_ref[0]) * g_ref[...]
    hb = (h * (1.0 + sc_ref[0]) + sh_ref[0]).astype(BF16)
    cos, sin = cos_ref[...], sin_ref[...]
    q = _rope(_dot(hb, w_ref[:, 0:A_WIDTH]), cos, sin) * Q_SCALE
    k = _rope(_dot(hb, w_ref[:, A_WIDTH:2 * A_WIDTH]), cos, sin)
    v = _dot(hb, w_ref[:, 2 * A_WIDTH:3 * A_WIDTH])
    gz = jax.nn.silu(_dot(hb, w_ref[:, 3 * A_WIDTH:4 * A_WIDTH]))
    return q, k, v, gz


def _inproj_a_prompt_kernel(x_ref, sh_ref, sc_ref, g_ref, w_ref, cos_ref, sin_ref, *rest, tm, nt):
    qkv_refs, (gz_ref, c0_ref, c1_ref, c2_ref, scr) = rest[:9], rest[9:]
    i = pl.program_id(1)
    q, k, v, gz = _inproj_a_body(x_ref, sh_ref, sc_ref, g_ref, w_ref, cos_ref, sin_ref)
    gz_ref[0] = gz.astype(BF16)
    slabs_per_group = GROUP_WIDTH // LANES
    for which, val in enumerate((q, k, v)):
        for j in range(A_WIDTH // LANES):
            scr[j] = val[:, j * LANES:(j + 1) * LANES]
        for g, (_, dil) in enumerate(A_GROUPS):
            out_ref = qkv_refs[3 * g + which]
            for r in range(dil):
                for h in range(slabs_per_group):
                    rows = scr[g * slabs_per_group + h, pl.ds(r, tm // dil, stride=dil), :]
                    out_ref[0, r, :, h * LANES:(h + 1) * LANES] = rows.astype(BF16)
    for g, c_ref in enumerate((c0_ref, c1_ref, c2_ref)):
        win = A_GROUPS[g][0]
        rows = min(tm, win)
        first_tile = nt - max(win // tm, 1)

        @pl.when(i >= first_tile)
        def _(g=g, c_ref=c_ref, rows=rows):
            cols = slice(g * GROUP_WIDTH, (g + 1) * GROUP_WIDTH)
            c_ref[0, 0:GROUP_WIDTH, :] = k[tm - rows:, cols].T
            c_ref[0, GROUP_WIDTH:2 * GROUP_WIDTH, :] = v[tm - rows:, cols].T


def _inproj_a_sample_kernel(x_ref, sh_ref, sc_ref, g_ref, w_ref, cos_ref, sin_ref,
                            q_ref, k_ref, v_ref, gz_ref):
    q, k, v, gz = _inproj_a_body(x_ref, sh_ref, sc_ref, g_ref, w_ref, cos_ref, sin_ref)
    q_ref[0] = q
    k_ref[0] = k
    v_ref[0] = v
    gz_ref[0] = gz.astype(BF16)


def _mod_spec(mod, tm):
    if mod.shape[1] == 1:
        return pl.BlockSpec((1, 1, D_MODEL), lambda b, i: (b, 0, 0))
    return pl.BlockSpec((1, tm, D_MODEL), lambda b, i: (b, i, 0))


def _inproj_a(x, shift, scale, g, w, cos, sin, *, sample):
    nb, t, _ = x.shape
    tm = min(TOKEN_TILE, t)
    nt = t // tm
    row = lambda width: pl.BlockSpec((1, tm, width), lambda b, i: (b, i, 0))
    in_specs = [
        row(D_MODEL), _mod_spec(shift, tm), _mod_spec(scale, tm),
        pl.BlockSpec((1, D_MODEL), lambda b, i: (0, 0)),
        pl.BlockSpec((D_MODEL, 4 * A_WIDTH), lambda b, i: (0, 0)),
        pl.BlockSpec((tm, LANES), lambda b, i: (i, 0)),
        pl.BlockSpec((tm, LANES), lambda b, i: (i, 0)),
    ]
    if sample:
        out_shape = [jax.ShapeDtypeStruct((nb, t, A_WIDTH), dt) for dt in (F32, F32, F32, BF16)]
        return pl.pallas_call(
            _inproj_a_sample_kernel, grid=(nb, nt), in_specs=in_specs,
            out_specs=[row(A_WIDTH)] * 4, out_shape=out_shape,
            compiler_params=_ARB(2), name="inproj_a_sample",
        )(x, shift, scale, g, w, cos, sin)
    out_shape, out_specs = [], []
    for _, dil in A_GROUPS:
        out_shape += [jax.ShapeDtypeStruct((nb, dil, t // dil, GROUP_WIDTH), BF16)] * 3
        out_specs += [pl.BlockSpec((1, dil, tm // dil, GROUP_WIDTH), lambda b, i: (b, 0, i, 0))] * 3
    out_shape.append(jax.ShapeDtypeStruct((nb, t, A_WIDTH), BF16))
    out_specs.append(row(A_WIDTH))
    for win, _ in A_GROUPS:
        rows = min(tm, win)
        first_tile = nt - max(win // tm, 1)
        out_shape.append(jax.ShapeDtypeStruct((nb, 2 * GROUP_WIDTH, min(win, t)), F32))
        out_specs.append(pl.BlockSpec(
            (1, 2 * GROUP_WIDTH, rows),
            lambda b, i, first_tile=first_tile: (b, 0, jnp.maximum(i - first_tile, 0))))
    return pl.pallas_call(
        functools.partial(_inproj_a_prompt_kernel, tm=tm, nt=nt),
        grid=(nb, nt), in_specs=in_specs, out_specs=out_specs, out_shape=out_shape,
        scratch_shapes=[pltpu.VMEM((A_WIDTH // LANES, tm, LANES), F32)],
        compiler_params=_ARB(2), name="inproj_a_prompt",
    )(x, shift, scale, g, w, cos, sin)


def _band_mask(n, lo):
    qi = lax.broadcasted_iota(jnp.int32, (BLOCK, 2 * BLOCK), 0)
    kj = lax.broadcasted_iota(jnp.int32, (BLOCK, 2 * BLOCK), 1)
    first_key = jnp.where(n > 0, 0, BLOCK)
    return (kj >= qi + lo) & (kj <= qi + BLOCK) & (kj >= first_key)


def _attend(q_slabs, k, v, mask, sink):
    width = k.shape[1]
    heads = width // HEAD_DIM
    lane_head = lax.broadcasted_iota(jnp.int32, (1, width), 1) // HEAD_DIM
    blocks = [jnp.where(lane_head == u, qs, jnp.zeros_like(qs)) for qs in q_slabs for u in range(heads)]
    nblk = len(blocks)
    s = lax.dot_general(jnp.concatenate(blocks, axis=0), k, (((1,), (1,)), ((), ())), preferred_element_type=F32)
    ps, ms, ls = [], [], []
    for j in range(nblk):
        sj = jnp.where(mask, s[j * BLOCK:(j + 1) * BLOCK], NEG)
        mj = jnp.max(sj, axis=-1, keepdims=True)
        if sink is not None:
            mj = jnp.maximum(mj, sink[j])
        pj = jnp.exp(sj - mj)
        lj = jnp.sum(pj, axis=-1, keepdims=True)
        if sink is not None:
            lj = lj + jnp.exp(sink[j] - mj)
        ps.append(pj.astype(BF16))
        ms.append(mj)
        ls.append(lj)
    o = _dot(jnp.concatenate(ps, axis=0), v)
    outs = []
    for si in range(len(q_slabs)):
        acc = None
        for u in range(heads):
            j = si * heads + u
            term = jnp.where(lane_head == u, o[j * BLOCK:(j + 1) * BLOCK] * (1.0 / ls[j]), 0.0)
            acc = term if acc is None else acc + term
        outs.append(acc)
    return outs, ms, ls, lane_head


def _dilated_attn_kernel(q_ref, kp_ref, kc_ref, vp_ref, vc_ref, o_ref, lse_ref):
    n = pl.program_id(2)
    k = jnp.concatenate([kp_ref[0, 0], kc_ref[0, 0]], axis=0)
    v = jnp.concatenate([vp_ref[0, 0], vc_ref[0, 0]], axis=0)
    outs, ms, ls, lane_head = _attend([q_ref[0, 0]], k, v, _band_mask(n, 0), None)
    o_ref[0, 0] = outs[0].astype(BF16)
    acc = None
    for u in range(A_HEADS_PER_GROUP):
        term = jnp.where(lane_head == u, ms[u] + jnp.log(ls[u]), 0.0)
        acc = term if acc is None else acc + term
    lse_ref[0, 0] = acc


def _dilated_attn(q, k, v, g):
    nb, dil, m_rows, _ = q.shape
    cur = lambda b, r, n: (b, r, n, 0)
    prev = lambda b, r, n: (b, r, jnp.maximum(n - 1, 0), 0)
    blk = lambda imap: pl.BlockSpec((1, 1, BLOCK, GROUP_WIDTH), imap)
    return pl.pallas_call(
        _dilated_attn_kernel,
        grid=(nb, dil, m_rows // BLOCK),
        in_specs=[blk(cur), blk(prev), blk(cur), blk(prev), blk(cur)],
        out_specs=[blk(cur), blk(cur)],
        out_shape=[jax.ShapeDtypeStruct(q.shape, BF16), jax.ShapeDtypeStruct(q.shape, F32)],
        compiler_params=_ARB(3), name=f"dilated_attn_g{g}",
    )(q, k, k, v, v)


def _swa_attn_kernel(q_ref, kp_ref, kc_ref, vp_ref, vc_ref, sink_ref, o_ref):
    n = pl.program_id(1)
    mask = _band_mask(n, 1)
    kdup = jnp.concatenate([kp_ref[0], kc_ref[0]], axis=0)
    vdup = jnp.concatenate([vp_ref[0], vc_ref[0]], axis=0)
    slabs_per_kv = B_GROUP * HEAD_DIM // LANES
    for kvh in range(B_KV_HEADS):
        k = kdup[:, kvh * LANES:(kvh + 1) * LANES]
        v = vdup[:, kvh * LANES:(kvh + 1) * LANES]
        base = kvh * slabs_per_kv
        q_slabs = [q_ref[0, :, (base + j) * LANES:(base + j + 1) * LANES] for j in range(slabs_per_kv)]
        sink = [sink_ref[kvh * B_GROUP + j:kvh * B_GROUP + j + 1, 0:1] for j in range(B_GROUP)]
        outs, _, _, _ = _attend(q_slabs, k, v, mask, sink)
        for j in range(slabs_per_kv):
            o_ref[0, :, (base + j) * LANES:(base + j + 1) * LANES] = outs[j].astype(BF16)


def _swa_attn(q, kdup, vdup, sinks):
    nb, t, _ = q.shape
    nblk = t // BLOCK
    cur = lambda b, n: (b, n, 0)
    prev = lambda b, n: (b, jnp.maximum(n - 1, 0), 0)
    kv_blk = lambda imap: pl.BlockSpec((1, BLOCK, 2 * B_KV_WIDTH), imap)
    return pl.pallas_call(
        _swa_attn_kernel,
        grid=(nb, nblk),
        in_specs=[pl.BlockSpec((1, BLOCK, B_WIDTH), cur), kv_blk(prev), kv_blk(cur), kv_blk(prev), kv_blk(cur),
                  pl.BlockSpec((B_Q_HEADS, LANES), lambda b, n: (0, 0))],
        out_specs=pl.BlockSpec((1, BLOCK, B_WIDTH), cur),
        out_shape=jax.ShapeDtypeStruct((nb, t, B_WIDTH), BF16),
        compiler_params=_ARB(2), name="swa_attn",
    )(q, kdup, kdup, vdup, vdup, sinks)


def _finish(a_parts, w_ref, x_ref, gate_ref, g_ref, y_ref):
    acc = None
    row = 0
    for a in a_parts:
        part = _dot(a, w_ref[row:row + a.shape[1], :])
        acc = part if acc is None else acc + part
        row += a.shape[1]
    y_ref[0] = x_ref[0] + gate_ref[0] * (_normalize(acc) * g_ref[...])


def _in_row_order(ref, scr):
    dil = ref.shape[1]
    if dil == 1:
        return ref[0, 0].astype(F32)
    nslab = ref.shape[3] // LANES
    for r in range(dil):
        for h in range(nslab):
            scr[h, pl.ds(r, ref.shape[2], stride=dil), :] = ref[0, r, :, h * LANES:(h + 1) * LANES].astype(F32)
    return jnp.concatenate([scr[h] for h in range(nslab)], axis=1)


def _outproj_mix_kernel(o0_ref, o1_ref, o2_ref, l0_ref, l1_ref, l2_ref, gz_ref, x_ref, gate_ref, g_ref, w_ref,
                        y_ref, *scratch):
    os = [_in_row_order(r, s) for r, s in zip((o0_ref, o1_ref, o2_ref), scratch[0:3])]
    lses = [_in_row_order(r, s) for r, s in zip((l0_ref, l1_ref, l2_ref), scratch[3:6])]
    top = jnp.maximum(jnp.maximum(lses[0], lses[1]), lses[2])
    es = [jnp.exp(l - top) for l in lses]
    inv = 1.0 / (es[0] + es[1] + es[2])
    parts = []
    for g in range(len(A_GROUPS)):
        gz = gz_ref[0, :, g * GROUP_WIDTH:(g + 1) * GROUP_WIDTH].astype(F32)
        parts.append((os[g] * (es[g] * inv) * gz).astype(BF16))
    _finish(parts, w_ref, x_ref, gate_ref, g_ref, y_ref)


def _outproj_kernel(o_ref, gz_ref, x_ref, gate_ref, g_ref, w_ref, y_ref):
    a = (o_ref[0].astype(F32) * gz_ref[0].astype(F32)).astype(BF16)
    _finish([a], w_ref, x_ref, gate_ref, g_ref, y_ref)


def _outproj(os, lses, gz, x, gate, g, w):
    nb, t, _ = x.shape
    tm = min(TOKEN_TILE, t)
    row = lambda width: pl.BlockSpec((1, tm, width), lambda b, i: (b, i, 0))
    width = w.shape[0]
    tail_specs = [row(width), row(D_MODEL), _mod_spec(gate, tm),
                  pl.BlockSpec((1, D_MODEL), lambda b, i: (0, 0)),
                  pl.BlockSpec((width, D_MODEL), lambda b, i: (0, 0))]
    scratch = []
    if lses is None:
        kern, name = _outproj_kernel, "outproj_b"
        in_specs = [row(width)] + tail_specs
        args = (os[0], gz, x, gate, g, w)
    else:
        kern, name = _outproj_mix_kernel, "outproj_a"
        split = lambda a: pl.BlockSpec((1, a.shape[1], tm // a.shape[1], GROUP_WIDTH), lambda b, i: (b, 0, i, 0))
        in_specs = [split(a) for a in (*os, *lses)] + tail_specs
        args = (*os, *lses, gz, x, gate, g, w)
        scratch = [pltpu.VMEM((GROUP_WIDTH // LANES, tm, LANES), F32)] * 6
    return pl.pallas_call(
        kern, grid=(nb, t // tm), in_specs=in_specs, out_specs=row(D_MODEL),
        out_shape=jax.ShapeDtypeStruct((nb, t, D_MODEL), F32), scratch_shapes=scratch,
        compiler_params=_ARB(2), name=name,
    )(*args)


def _repeat_heads(a):
    lane = lax.broadcasted_iota(jnp.int32, (1, LANES), 1)
    swapped = pltpu.roll(a, HEAD_DIM, 1)
    low = lane < HEAD_DIM
    return jnp.concatenate([jnp.where(low, a, swapped), jnp.where(low, swapped, a)], axis=1)


def _inproj_b_body(x_ref, sh_ref, sc_ref, ksh_ref, ksc_ref, g_ref, gkv_ref, w_ref, wkv_ref, cos_ref, sin_ref):
    xn = _normalize(x_ref[0])
    hb = ((xn * g_ref[...]) * (1.0 + sc_ref[0]) + sh_ref[0]).astype(BF16)
    hk = ((xn * gkv_ref[...]) * (1.0 + ksc_ref[0]) + ksh_ref[0]).astype(BF16)
    cos, sin = cos_ref[...], sin_ref[...]
    q = _rope(_dot(hb, w_ref[:, 0:B_WIDTH]), cos, sin) * Q_SCALE
    gz = jax.nn.silu(_dot(hb, w_ref[:, B_WIDTH:2 * B_WIDTH]))
    kv = _dot(hk, wkv_ref[...])
    k = _rope(kv[:, 0:B_KV_WIDTH], cos, sin)
    v = kv[:, B_KV_WIDTH:2 * B_KV_WIDTH]
    return q, gz, k, v


def _inproj_b_prompt_kernel(x_ref, sh_ref, sc_ref, ksh_ref, ksc_ref, g_ref, gkv_ref, w_ref, wkv_ref, cos_ref,
                            sin_ref, q_ref, gz_ref, kd_ref, vd_ref, c_ref, *, tm, nt):
    q, gz, k, v = _inproj_b_body(x_ref, sh_ref, sc_ref, ksh_ref, ksc_ref, g_ref, gkv_ref, w_ref, wkv_ref, cos_ref,
                                 sin_ref)
    q_ref[0] = q.astype(BF16)
    gz_ref[0] = gz.astype(BF16)
    kd_ref[0] = _repeat_heads(k).astype(BF16)
    vd_ref[0] = _repeat_heads(v).astype(BF16)

    @pl.when(pl.program_id(1) == nt - 1)
    def _():
        c_ref[0, 0:B_KV_WIDTH, :] = k[tm - B_WINDOW:, :].T
        c_ref[0, B_KV_WIDTH:2 * B_KV_WIDTH, :] = v[tm - B_WINDOW:, :].T


def _inproj_b_sample_kernel(x_ref, sh_ref, sc_ref, ksh_ref, ksc_ref, g_ref, gkv_ref, w_ref, wkv_ref, cos_ref,
                            sin_ref, q_ref, gz_ref, k_ref, v_ref):
    q, gz, k, v = _inproj_b_body(x_ref, sh_ref, sc_ref, ksh_ref, ksc_ref, g_ref, gkv_ref, w_ref, wkv_ref, cos_ref,
                                 sin_ref)
    q_ref[0] = q
    gz_ref[0] = gz.astype(BF16)
    k_ref[0] = k
    v_ref[0] = v


def _inproj_b(x, shift, scale, kshift, kscale, g, gkv, w, wkv, cos, sin, *, sample):
    nb, t, _ = x.shape
    tm = min(TOKEN_TILE, t)
    nt = t // tm
    row = lambda width: pl.BlockSpec((1, tm, width), lambda b, i: (b, i, 0))
    vec = pl.BlockSpec((1, D_MODEL), lambda b, i: (0, 0))
    tab = pl.BlockSpec((tm, LANES), lambda b, i: (i, 0))
    in_specs = [row(D_MODEL), _mod_spec(shift, tm), _mod_spec(scale, tm), _mod_spec(kshift, tm),
                _mod_spec(kscale, tm), vec, vec,
                pl.BlockSpec((D_MODEL, 2 * B_WIDTH), lambda b, i: (0, 0)),
                pl.BlockSpec((D_MODEL, 2 * B_KV_WIDTH), lambda b, i: (0, 0)), tab, tab]
    args = (x, shift, scale, kshift, kscale, g, gkv, w, wkv, cos, sin)
    if sample:
        return pl.pallas_call(
            _inproj_b_sample_kernel, grid=(nb, nt), in_specs=in_specs,
            out_specs=[row(B_WIDTH), row(B_WIDTH), row(B_KV_WIDTH), row(B_KV_WIDTH)],
            out_shape=[jax.ShapeDtypeStruct((nb, t, B_WIDTH), F32), jax.ShapeDtypeStruct((nb, t, B_WIDTH), BF16),
                       jax.ShapeDtypeStruct((nb, t, B_KV_WIDTH), F32),
                       jax.ShapeDtypeStruct((nb, t, B_KV_WIDTH), F32)],
            compiler_params=_ARB(2), name="inproj_b_sample",
        )(*args)
    return pl.pallas_call(
        functools.partial(_inproj_b_prompt_kernel, tm=tm, nt=nt),
        grid=(nb, nt), in_specs=in_specs,
        out_specs=[row(B_WIDTH), row(B_WIDTH), row(2 * B_KV_WIDTH), row(2 * B_KV_WIDTH),
                   pl.BlockSpec((1, 2 * B_KV_WIDTH, B_WINDOW), lambda b, i: (b, 0, 0))],
        out_shape=[jax.ShapeDtypeStruct((nb, t, B_WIDTH), BF16), jax.ShapeDtypeStruct((nb, t, B_WIDTH), BF16),
                   jax.ShapeDtypeStruct((nb, t, 2 * B_KV_WIDTH), BF16),
                   jax.ShapeDtypeStruct((nb, t, 2 * B_KV_WIDTH), BF16),
                   jax.ShapeDtypeStruct((nb, 2 * B_KV_WIDTH, B_WINDOW), F32)],
        compiler_params=_ARB(2), name="inproj_b_prompt",
    )(*args)


def _eye():
    r = lax.broadcasted_iota(jnp.int32, (LANES, LANES), 0)
    c = lax.broadcasted_iota(jnp.int32, (LANES, LANES), 1)
    return r == c


def _row_to_col(row):
    eye = _eye()
    chunks = [jnp.sum(jnp.where(eye, row[:, j * LANES:(j + 1) * LANES], 0.0), axis=1, keepdims=True)
              for j in range(row.shape[1] // LANES)]
    return chunks[0] if len(chunks) == 1 else jnp.concatenate(chunks, axis=0)


def _col_to_row(col):
    eye = _eye()
    chunks = [jnp.sum(jnp.where(eye, col[j * LANES:(j + 1) * LANES, :], 0.0), axis=0, keepdims=True)
              for j in range(col.shape[0] // LANES)]
    return chunks[0] if len(chunks) == 1 else jnp.concatenate(chunks, axis=1)


def _sample_attn_kernel(q_ref, kn_ref, vn_ref, cache_ref, *rest, dil, q_per_kv, first_row, with_lse, with_sink):
    rest = list(rest)
    sink_ref = rest.pop(0) if with_sink else None
    out_cache_ref, o_ref = rest[0], rest[1]
    lse_ref = rest[2] if with_lse else None
    b = pl.program_id(0)
    half = cache_ref.shape[1] // 2
    length = cache_ref.shape[2]
    kv_heads = half // HEAD_DIM
    qc = _row_to_col(q_ref[pl.ds(b, 1), :])
    knc = _row_to_col(kn_ref[pl.ds(b, 1), :])
    vnc = _row_to_col(vn_ref[pl.ds(b, 1), :])
    pos = lax.broadcasted_iota(jnp.int32, (1, length), 1)
    valid = (pos % dil == 0) & (pos >= first_row)
    o_cols, lse_cols = [], []
    for kh in range(kv_heads):
        rows = slice(kh * HEAD_DIM, (kh + 1) * HEAD_DIM)
        kt = cache_ref[0, rows, :]
        vt = cache_ref[0, half + kh * HEAD_DIM:half + (kh + 1) * HEAD_DIM, :]
        for gq in range(q_per_kv):
            hq = kh * q_per_kv + gq
            qh = qc[hq * HEAD_DIM:(hq + 1) * HEAD_DIM, :]
            s = jnp.where(valid, jnp.sum(kt * qh, axis=0, keepdims=True), NEG)
            s_new = jnp.sum(knc[rows, :] * qh, axis=0, keepdims=True)
            m = jnp.maximum(jnp.max(s, axis=1, keepdims=True), s_new)
            if with_sink:
                sink = sink_ref[hq:hq + 1, 0:1]
                m = jnp.maximum(m, sink)
            p = jnp.exp(s - m)
            p_new = jnp.exp(s_new - m)
            l = jnp.sum(p, axis=1, keepdims=True) + p_new
            if with_sink:
                l = l + jnp.exp(sink - m)
            o = jnp.sum(vt * p, axis=1, keepdims=True) + vnc[rows, :] * p_new
            o_cols.append(o * (1.0 / l))
            if with_lse:
                lse_cols.append(jnp.broadcast_to(m + jnp.log(l), (HEAD_DIM, 1)))
    o_ref[pl.ds(b, 1), :] = _col_to_row(jnp.concatenate(o_cols, axis=0))
    if with_lse:
        lse_ref[pl.ds(b, 1), :] = _col_to_row(jnp.concatenate(lse_cols, axis=0))
    new_col = jnp.concatenate([knc, vnc], axis=0)
    lane = lax.broadcasted_iota(jnp.int32, (1, LANES), 1)
    ntile = length // LANES
    nxt = pltpu.roll(cache_ref[0, :, 0:LANES], LANES - 1, 1)
    for j in range(ntile):
        cur = nxt
        if j + 1 < ntile:
            nxt = pltpu.roll(cache_ref[0, :, (j + 1) * LANES:(j + 2) * LANES], LANES - 1, 1)
            fill = nxt
        else:
            fill = new_col
        out_cache_ref[0, :, j * LANES:(j + 1) * LANES] = jnp.where(lane < LANES - 1, cur, fill)


def _sample_attn(q, k_new, v_new, cache_t, *, dil, q_per_kv, first_row, with_lse, sinks=None):
    nb, chans, length = cache_t.shape
    wq, wkv = q.shape[1], k_new.shape[1]
    full = lambda w: pl.BlockSpec((nb, w), lambda b: (0, 0))
    tile = pl.BlockSpec((1, chans, length), lambda b: (b, 0, 0))
    in_specs = [full(wq), full(wkv), full(wkv), tile]
    args = [q, k_new, v_new, cache_t]
    if sinks is not None:
        in_specs.append(pl.BlockSpec(sinks.shape, lambda b: (0, 0)))
        args.append(sinks)
    out_specs = [tile, full(wq)]
    out_shape = [jax.ShapeDtypeStruct(cache_t.shape, F32), jax.ShapeDtypeStruct((nb, wq), F32)]
    if with_lse:
        out_specs.append(full(wq))
        out_shape.append(jax.ShapeDtypeStruct((nb, wq), F32))
    return pl.pallas_call(
        functools.partial(_sample_attn_kernel, dil=dil, q_per_kv=q_per_kv, first_row=first_row, with_lse=with_lse,
                          with_sink=sinks is not None),
        grid=(nb,), in_specs=in_specs, out_specs=out_specs, out_shape=out_shape,
        compiler_params=_ARB(1), name=f"sample_attn_d{dil}_l{length}",
    )(*args)


def _rope_tables(pos):
    half = HEAD_DIM // 2
    inv = ROPE_THETA ** (-jnp.arange(half, dtype=F32) / half)
    ang = pos.astype(F32)[:, None] * inv[None, :]
    cos, sin = jnp.cos(ang), jnp.sin(ang)
    return jnp.tile(cos, (1, 4)), jnp.tile(jnp.concatenate([-sin, sin], axis=1), (1, 2))


def _to_tiles(cache):
    nb, length = cache.shape[0], cache.shape[1]
    return jnp.transpose(cache, (0, 2, 3, 4, 1)).reshape(nb, -1, length)


def _from_tiles(tiles, heads):
    nb, _, length = tiles.shape
    return jnp.transpose(tiles.reshape(nb, 2, heads, HEAD_DIM, length), (0, 4, 1, 2, 3))


def kernel(x_prompt, x_sample, c_prompt, c_sample, cache_a_kv_g0, cache_a_kv_g1, cache_a_kv_g2, cache_b_kv, ada_w,
           ada_b, g_pre, g_post, w_in_a, w_o_a, w_in_b, w_o_b, sinks_b, ada_kv_w, ada_kv_b, g_kv, w_kv):
    nbp, t, _ = x_prompt.shape
    nbs = x_sample.shape[0]
    assert x_sample.shape[1] == 1

    c_all = jnp.concatenate([c_prompt, c_sample], axis=0)
    mod = _ada(c_all, ada_w, ada_b)
    mod_kv = _ada(c_all, ada_kv_w[None], ada_kv_b[None])[0]

    def split(a, parts, sample):
        rows = a[nbp:][None] if sample else a[:nbp][:, None]
        return [rows[..., p * D_MODEL:(p + 1) * D_MODEL] for p in range(parts)]

    w_in_a_b, w_o_a_b = w_in_a[0].astype(BF16), w_o_a[0].astype(BF16)
    w_in_b_b, w_o_b_b, w_kv_b = w_in_b[0].astype(BF16), w_o_b[0].astype(BF16), w_kv.astype(BF16)
    g_pre0, g_pre1 = g_pre[0:1], g_pre[1:2]
    g_post0, g_post1 = g_post[0:1], g_post[1:2]
    g_kv_r = g_kv[None]
    sinks = jnp.broadcast_to(sinks_b[0][:, None], (B_Q_HEADS, LANES))

    cos_p, sin_p = _rope_tables(jnp.arange(t))
    sh0, sc0, gt0 = split(mod[0], 3, False)
    sh1, sc1, gt1 = split(mod[1], 3, False)
    ksh, ksc = split(mod_kv, 2, False)
    *qkv, gz, ca0, ca1, ca2 = _inproj_a(x_prompt, sh0, sc0, g_pre0, w_in_a_b, cos_p, sin_p, sample=False)
    os, lses = zip(*[_dilated_attn(*qkv[3 * g:3 * g + 3], g) for g in range(len(A_GROUPS))])
    x1 = _outproj(os, lses, gz, x_prompt, gt0, g_post0, w_o_a_b)
    qb, gzb, kd, vd, cb = _inproj_b(x1, sh1, sc1, ksh, ksc, g_pre1, g_kv_r, w_in_b_b, w_kv_b, cos_p, sin_p,
                                    sample=False)
    ob = _swa_attn(qb, kd, vd, sinks)
    y_prompt = _outproj([ob], None, gzb, x1, gt1, g_post1, w_o_b_b)
    new_a_prompt = [_from_tiles(c, A_HEADS_PER_GROUP)[None] for c in (ca0, ca1, ca2)]
    new_b_prompt = _from_tiles(cb, B_KV_HEADS)

    xs = x_sample.reshape(1, nbs, D_MODEL)
    cos_s, sin_s = _rope_tables(jnp.full((nbs,), PAST_LEN, jnp.int32))
    sh0, sc0, gt0 = split(mod[0], 3, True)
    sh1, sc1, gt1 = split(mod[1], 3, True)
    ksh, ksc = split(mod_kv, 2, True)
    q, k, v, gz = _inproj_a(xs, sh0, sc0, g_pre0, w_in_a_b, cos_s, sin_s, sample=True)
    os, lses, new_a_sample = [], [], []
    for g, cache in enumerate((cache_a_kv_g0, cache_a_kv_g1, cache_a_kv_g2)):
        cols = slice(g * GROUP_WIDTH, (g + 1) * GROUP_WIDTH)
        shifted, o, lse = _sample_attn(q[0, :, cols], k[0, :, cols], v[0, :, cols], _to_tiles(cache[0]),
                                       dil=A_GROUPS[g][1], q_per_kv=1, first_row=0, with_lse=True)
        os.append(o.astype(BF16)[None, None])
        lses.append(lse[None, None])
        new_a_sample.append(_from_tiles(shifted, A_HEADS_PER_GROUP)[None])
    xs1 = _outproj(os, lses, gz, xs, gt0, g_post0, w_o_a_b)
    qb, gzb, kb, vb = _inproj_b(xs1, sh1, sc1, ksh, ksc, g_pre1, g_kv_r, w_in_b_b, w_kv_b, cos_s, sin_s,
                                sample=True)
    shifted_b, ob = _sample_attn(qb[0], kb[0], vb[0], _to_tiles(cache_b_kv), dil=1, q_per_kv=B_GROUP, first_row=1,
                                 with_lse=False, sinks=sinks)
    y_sample = _outproj([ob.astype(BF16)[None]], None, gzb, xs1, gt1, g_post1, w_o_b_b).reshape(nbs, 1, D_MODEL)
    new_b_sample = _from_tiles(shifted_b, B_KV_HEADS)

    return (y_prompt, y_sample, *new_a_prompt, new_b_prompt, *new_a_sample, new_b_sample)
```

```python
import functools

import jax
import jax.numpy as jnp
from jax import lax
from jax.experimental import pallas as pl
from jax.experimental.pallas import tpu as pltpu

D_MODEL = 1024
HEAD_DIM = 64
A_GROUPS = ((128, 1), (512, 4), (2048, 16))
A_HEADS_PER_GROUP = 4
GROUP_WIDTH = A_HEADS_PER_GROUP * HEAD_DIM
A_WIDTH = GROUP_WIDTH * len(A_GROUPS)
B_Q_HEADS = 16
B_KV_HEADS = 2
B_GROUP = B_Q_HEADS // B_KV_HEADS
B_WIDTH = B_Q_HEADS * HEAD_DIM
B_KV_WIDTH = B_KV_HEADS * HEAD_DIM
B_WINDOW = 128
BLOCK = 128
PAST_LEN = 16384
ROPE_THETA = 10000.0
EPS = 1e-6
NEG = -1e30
LANES = 128
Q_SCALE = HEAD_DIM ** -0.5

BF16 = jnp.bfloat16
F32 = jnp.float32
TOKEN_TILE = 512

_ARB = lambda n: pltpu.CompilerParams(dimension_semantics=("arbitrary",) * n)


def _rope(x, cos, sin_signed):
    lane = lax.broadcasted_iota(jnp.int32, (1, LANES), 1)
    first_half = (lane % HEAD_DIM) < (HEAD_DIM // 2)
    outs = []
    for j in range(x.shape[1] // LANES):
        xs = x[:, j * LANES:(j + 1) * LANES]
        partner = jnp.where(first_half, pltpu.roll(xs, LANES - 32, 1), pltpu.roll(xs, 32, 1))
        outs.append(xs * cos + partner * sin_signed)
    return outs[0] if len(outs) == 1 else jnp.concatenate(outs, axis=1)


def _normalize(x):
    return x * lax.rsqrt(jnp.mean(x * x, axis=-1, keepdims=True) + EPS)


def _dot(a, b):
    return jnp.dot(a, b, preferred_element_type=F32)


def _ada_kernel(c_ref, w_ref, b_ref, o_ref):
    s = jax.nn.silu(c_ref[...]).astype(BF16)
    o_ref[...] = _dot(s, w_ref[...].astype(BF16)) + b_ref[...]


def _ada(c, w, b):
    nl, _, n = w.shape
    m = c.shape[0]
    tn = 1024
    return pl.pallas_call(
        _ada_kernel,
        grid=(nl, n // tn),
        in_specs=[
            pl.BlockSpec((m, D_MODEL), lambda l, j: (0, 0)),
            pl.BlockSpec((None, D_MODEL, tn), lambda l, j: (l, 0, j)),
            pl.BlockSpec((None, 1, tn), lambda l, j: (l, 0, j)),
        ],
        out_specs=pl.BlockSpec((None, m, tn), lambda l, j: (l, 0, j)),
        out_shape=jax.ShapeDtypeStruct((nl, m, n), F32),
        compiler_params=_ARB(2),
        name="ada",
    )(c, w, b.reshape(nl, 1, n))


def _inproj_a_body(x_ref, sh_ref, sc_ref, g_ref, w_ref, cos_ref, sin_ref):
    h = _normalize(x_ref[0]) * g_ref[...]
    hb = (h * (1.0 + sc_ref[0]) + sh_ref[0]).astype(BF16)
    cos, sin = cos_ref[...], sin_ref[...]
    q = _rope(_dot(hb, w_ref[:, 0:A_WIDTH]), cos, sin) * Q_SCALE
    k = _rope(_dot(hb, w_ref[:, A_WIDTH:2 * A_WIDTH]), cos, sin)
    v = _dot(hb, w_ref[:, 2 * A_WIDTH:3 * A_WIDTH])
    gz = jax.nn.silu(_dot(hb, w_ref[:, 3 * A_WIDTH:4 * A_WIDTH]))
    return q, k, v, gz


def _inproj_a_prompt_kernel(x_ref, sh_ref, sc_ref, g_ref, w_ref, cos_ref, sin_ref, *rest, tm, nt):
    qkv_refs, (gz_ref, c0_ref, c1_ref, c2_ref, scr) = rest[:9], rest[9:]
    i = pl.program_id(1)
    q, k, v, gz = _inproj_a_body(x_ref, sh_ref, sc_ref, g_ref, w_ref, cos_ref, sin_ref)
    gz_ref[0] = gz.astype(BF16)
    slabs_per_group = GROUP_WIDTH // LANES
    for which, val in enumerate((q, k, v)):
        for j in range(A_WIDTH // LANES):
            scr[j] = val[:, j * LANES:(j + 1) * LANES]
        for g, (_, dil) in enumerate(A_GROUPS):
            out_ref = qkv_refs[3 * g + which]
            for r in range(dil):
                for h in range(slabs_per_group):
                    rows = scr[g * slabs_per_group + h, pl.ds(r, tm // dil, stride=dil), :]
                    out_ref[0, r, :, h * LANES:(h + 1) * LANES] = rows.astype(BF16)
    for g, c_ref in enumerate((c0_ref, c1_ref, c2_ref)):
        win = A_GROUPS[g][0]
        rows = min(tm, win)
        first_tile = nt - max(win // tm, 1)

        @pl.when(i >= first_tile)
        def _(g=g, c_ref=c_ref, rows=rows):
            cols = slice(g * GROUP_WIDTH, (g + 1) * GROUP_WIDTH)
            c_ref[0, 0:GROUP_WIDTH, :] = k[tm - rows:, cols].T
            c_ref[0, GROUP_WIDTH:2 * GROUP_WIDTH, :] = v[tm - rows:, cols].T


def _inproj_a_sample_kernel(x_ref, sh_ref, sc_ref, g_ref, w_ref, cos_ref, sin_ref,
                            q_ref, k_ref, v_ref, gz_ref):
    q, k, v, gz = _inproj_a_body(x_ref, sh_ref, sc_ref, g_ref, w_ref, cos_ref, sin_ref)
    q_ref[0] = q
    k_ref[0] = k
    v_ref[0] = v
    gz_ref[0] = gz.astype(BF16)


def _mod_spec(mod, tm):
    if mod.shape[1] == 1:
        return pl.BlockSpec((1, 1, D_MODEL), lambda b, i: (b, 0, 0))
    return pl.BlockSpec((1, tm, D_MODEL), lambda b, i: (b, i, 0))


def _inproj_a(x, shift, scale, g, w, cos, sin, *, sample):
    nb, t, _ = x.shape
    tm = min(TOKEN_TILE, t)
    nt = t // tm
    row = lambda width: pl.BlockSpec((1, tm, width), lambda b, i: (b, i, 0))
    in_specs = [
        row(D_MODEL), _mod_spec(shift, tm), _mod_spec(scale, tm),
        pl.BlockSpec((1, D_MODEL), lambda b, i: (0, 0)),
        pl.BlockSpec((D_MODEL, 4 * A_WIDTH), lambda b, i: (0, 0)),
        pl.BlockSpec((tm, LANES), lambda b, i: (i, 0)),
        pl.BlockSpec((tm, LANES), lambda b, i: (i, 0)),
    ]
    if sample:
        out_shape = [jax.ShapeDtypeStruct((nb, t, A_WIDTH), dt) for dt in (F32, F32, F32, BF16)]
        return pl.pallas_call(
            _inproj_a_sample_kernel, grid=(nb, nt), in_specs=in_specs,
            out_specs=[row(A_WIDTH)] * 4, out_shape=out_shape,
            compiler_params=_ARB(2), name="inproj_a_sample",
        )(x, shift, scale, g, w, cos, sin)
    out_shape, out_specs = [], []
    for _, dil in A_GROUPS:
        out_shape += [jax.ShapeDtypeStruct((nb, dil, t // dil, GROUP_WIDTH), BF16)] * 3
        out_specs += [pl.BlockSpec((1, dil, tm // dil, GROUP_WIDTH), lambda b, i: (b, 0, i, 0))] * 3
    out_shape.append(jax.ShapeDtypeStruct((nb, t, A_WIDTH), BF16))
    out_specs.append(row(A_WIDTH))
    for win, _ in A_GROUPS:
        rows = min(tm, win)
        first_tile = nt - max(win // tm, 1)
        out_shape.append(jax.ShapeDtypeStruct((nb, 2 * GROUP_WIDTH, min(win, t)), F32))
        out_specs.append(pl.BlockSpec(
            (1, 2 * GROUP_WIDTH, rows),
            lambda b, i, first_tile=first_tile: (b, 0, jnp.maximum(i - first_tile, 0))))
    return pl.pallas_call(
        functools.partial(_inproj_a_prompt_kernel, tm=tm, nt=nt),
        grid=(nb, nt), in_specs=in_specs, out_specs=out_specs, out_shape=out_shape,
        scratch_shapes=[pltpu.VMEM((A_WIDTH // LANES, tm, LANES), F32)],
        compiler_params=_ARB(2), name="inproj_a_prompt",
    )(x, shift, scale, g, w, cos, sin)


def _query_minus_key():
    kj = lax.broadcasted_iota(jnp.int32, (2 * BLOCK, BLOCK), 0)
    qi = lax.broadcasted_iota(jnp.int32, (2 * BLOCK, BLOCK), 1)
    return qi - kj


def _head_attend(st, mask, vt_ext, sink):
    st = jnp.where(mask, st, NEG)
    m = jnp.max(st, axis=0, keepdims=True)
    if sink is not None:
        m = jnp.maximum(m, sink)
    ext = _dot(vt_ext, jnp.exp(st - m).astype(BF16))
    l = ext[HEAD_DIM:HEAD_DIM + 1, :]
    if sink is not None:
        l = l + jnp.exp(sink - m)
    return ext[0:HEAD_DIM, :] * (1.0 / l), m, l


def _values_ext(vt_prev, vt_cur, head):
    rows = slice(head * HEAD_DIM, (head + 1) * HEAD_DIM)
    ones = jnp.ones((16, 2 * BLOCK), BF16)
    return jnp.concatenate([jnp.concatenate([vt_prev[rows], vt_cur[rows]], axis=1), ones], axis=0)


def _block_diag_queries(qt, heads, kv_of_head, kv_heads):
    zeros = jnp.zeros((HEAD_DIM, BLOCK), BF16)
    cols = []
    for h in range(heads):
        pieces = [zeros] * kv_heads
        pieces[kv_of_head(h)] = qt[h * HEAD_DIM:(h + 1) * HEAD_DIM]
        cols.append(jnp.concatenate(pieces, axis=0))
    return jnp.concatenate(cols, axis=1)


def _dilated_attn_kernel(q_ref, k_ref, v_ref, o_ref, lse_ref, vt_ref):
    nres, m_rows = q_ref.shape[1], q_ref.shape[2]
    nblk = m_rows // BLOCK
    base = _query_minus_key()
    heads = A_HEADS_PER_GROUP
    for r in range(nres):
        def transpose_values(j, c, r=r):
            rows = pl.ds(pl.multiple_of(j * BLOCK, BLOCK), BLOCK)
            vt_ref[j] = v_ref[0, r, rows, :].T
            return c

        lax.fori_loop(0, nblk, transpose_values, 0)

        def block(i, c, r=r):
            prev = jnp.maximum(i - 1, 0)
            dist = base + (i - prev) * BLOCK
            mask = (dist >= 0) & (dist <= BLOCK)
            qrows = pl.ds(pl.multiple_of(i * BLOCK, BLOCK), BLOCK)
            k = k_ref[0, r, pl.ds(pl.multiple_of(prev * BLOCK, BLOCK), 2 * BLOCK), :]
            qd = _block_diag_queries(q_ref[0, r, qrows, :].T, heads, lambda h: h, heads)
            st_all = _dot(k, qd)
            vt_prev, vt_cur = vt_ref[prev], vt_ref[i]
            outs, lses = [], []
            for h in range(heads):
                o_t, m, l = _head_attend(st_all[:, h * BLOCK:(h + 1) * BLOCK], mask,
                                         _values_ext(vt_prev, vt_cur, h), None)
                outs.append(o_t)
                lses.append(jnp.broadcast_to(m + jnp.log(l), (HEAD_DIM, BLOCK)))
            o_ref[0, r, qrows, :] = jnp.concatenate(outs, axis=0).T.astype(BF16)
            lse_ref[0, r, qrows, :] = jnp.concatenate(lses, axis=0).T
            return c

        lax.fori_loop(0, nblk, block, 0, unroll=2)


def _dilated_attn(q, k, v, g):
    nb, dil, m_rows, _ = q.shape
    nres = max(1, dil // 4)
    blk = pl.BlockSpec((1, nres, m_rows, GROUP_WIDTH), lambda b, r: (b, r, 0, 0))
    return pl.pallas_call(
        _dilated_attn_kernel,
        grid=(nb, dil // nres),
        in_specs=[blk, blk, blk],
        out_specs=[blk, blk],
        out_shape=[jax.ShapeDtypeStruct(q.shape, BF16), jax.ShapeDtypeStruct(q.shape, F32)],
        scratch_shapes=[pltpu.VMEM((m_rows // BLOCK, GROUP_WIDTH, BLOCK), BF16)],
        compiler_params=_ARB(2), name=f"dilated_attn_g{g}",
    )(q, k, v)


SWA_QUERY_TILE = 1024


def _swa_attn_kernel(q_ref, k_ref, v_ref, sink_ref, o_ref, vt_ref):
    j = pl.program_id(1)
    seq = k_ref.shape[1]
    nq = q_ref.shape[1] // BLOCK
    base = _query_minus_key()

    @pl.when(j == 0)
    def _():
        def transpose_values(t, c):
            vt_ref[t] = v_ref[0, pl.ds(pl.multiple_of(t * BLOCK, BLOCK), BLOCK), :].T
            return c

        lax.fori_loop(0, seq // BLOCK, transpose_values, 0)

    def block(i, c):
        cur = j * nq + i
        prev = jnp.maximum(cur - 1, 0)
        dist = base + (cur - prev) * BLOCK
        mask = (dist >= 0) & (dist < B_WINDOW)
        qrows = pl.ds(pl.multiple_of(i * BLOCK, BLOCK), BLOCK)
        k = k_ref[0, pl.ds(pl.multiple_of(prev * BLOCK, BLOCK), 2 * BLOCK), :]
        qt = q_ref[0, qrows, :].T
        vt_prev, vt_cur = vt_ref[prev], vt_ref[cur]
        outs = []
        for kvh in range(B_KV_HEADS):
            qd = _block_diag_queries(qt[kvh * B_GROUP * HEAD_DIM:(kvh + 1) * B_GROUP * HEAD_DIM], B_GROUP,
                                     lambda h, kvh=kvh: kvh, B_KV_HEADS)
            st_all = _dot(k, qd)
            vt_ext = _values_ext(vt_prev, vt_cur, kvh)
            for gq in range(B_GROUP):
                hq = kvh * B_GROUP + gq
                o_t, _, _ = _head_attend(st_all[:, gq * BLOCK:(gq + 1) * BLOCK], mask, vt_ext,
                                         sink_ref[hq:hq + 1, :])
                outs.append(o_t)
        o_ref[0, qrows, :] = jnp.concatenate(outs, axis=0).T.astype(BF16)
        return c

    lax.fori_loop(0, nq, block, 0)


def _swa_attn(q, k, v, sinks):
    nb, t, _ = q.shape
    tq = min(SWA_QUERY_TILE, t)
    qblk = pl.BlockSpec((1, tq, B_WIDTH), lambda b, j: (b, j, 0))
    kvblk = pl.BlockSpec((1, t, B_KV_WIDTH), lambda b, j: (b, 0, 0))
    return pl.pallas_call(
        _swa_attn_kernel,
        grid=(nb, t // tq),
        in_specs=[qblk, kvblk, kvblk, pl.BlockSpec((B_Q_HEADS, LANES), lambda b, j: (0, 0))],
        out_specs=qblk,
        out_shape=jax.ShapeDtypeStruct((nb, t, B_WIDTH), BF16),
        scratch_shapes=[pltpu.VMEM((t // BLOCK, B_KV_WIDTH, BLOCK), BF16)],
        compiler_params=_ARB(2), name="swa_attn",
    )(q, k, v, sinks)


def _finish(a_parts, w_ref, x_ref, gate_ref, g_ref, y_ref):
    acc = None
    row = 0
    for a in a_parts:
        part = _dot(a, w_ref[row:row + a.shape[1], :])
        acc = part if acc is None else acc + part
        row += a.shape[1]
    y_ref[0] = x_ref[0] + gate_ref[0] * (_normalize(acc) * g_ref[...])


def _in_row_order(ref, scr):
    dil = ref.shape[1]
    if dil == 1:
        return ref[0, 0].astype(F32)
    nslab = ref.shape[3] // LANES
    for r in range(dil):
        for h in range(nslab):
            scr[h, pl.ds(r, ref.shape[2], stride=dil), :] = ref[0, r, :, h * LANES:(h + 1) * LANES].astype(F32)
    return jnp.concatenate([scr[h] for h in range(nslab)], axis=1)


def _outproj_mix_kernel(o0_ref, o1_ref, o2_ref, l0_ref, l1_ref, l2_ref, gz_ref, x_ref, gate_ref, g_ref, w_ref,
                        y_ref, *scratch):
    os = [_in_row_order(r, s) for r, s in zip((o0_ref, o1_ref, o2_ref), scratch[0:3])]
    lses = [_in_row_order(r, s) for r, s in zip((l0_ref, l1_ref, l2_ref), scratch[3:6])]
    top = jnp.maximum(jnp.maximum(lses[0], lses[1]), lses[2])
    es = [jnp.exp(l - top) for l in lses]
    inv = 1.0 / (es[0] + es[1] + es[2])
    parts = []
    for g in range(len(A_GROUPS)):
        gz = gz_ref[0, :, g * GROUP_WIDTH:(g + 1) * GROUP_WIDTH].astype(F32)
        parts.append((os[g] * (es[g] * inv) * gz).astype(BF16))
    _finish(parts, w_ref, x_ref, gate_ref, g_ref, y_ref)


def _outproj_kernel(o_ref, gz_ref, x_ref, gate_ref, g_ref, w_ref, y_ref):
    a = (o_ref[0].astype(F32) * gz_ref[0].astype(F32)).astype(BF16)
    _finish([a], w_ref, x_ref, gate_ref, g_ref, y_ref)


def _outproj(os, lses, gz, x, gate, g, w):
    nb, t, _ = x.shape
    tm = min(TOKEN_TILE, t)
    row = lambda width: pl.BlockSpec((1, tm, width), lambda b, i: (b, i, 0))
    width = w.shape[0]
    tail_specs = [row(width), row(D_MODEL), _mod_spec(gate, tm),
                  pl.BlockSpec((1, D_MODEL), lambda b, i: (0, 0)),
                  pl.BlockSpec((width, D_MODEL), lambda b, i: (0, 0))]
    scratch = []
    if lses is None:
        kern, name = _outproj_kernel, "outproj_b"
        in_specs = [row(width)] + tail_specs
        args = (os[0], gz, x, gate, g, w)
    else:
        kern, name = _outproj_mix_kernel, "outproj_a"
        split = lambda a: pl.BlockSpec((1, a.shape[1], tm // a.shape[1], GROUP_WIDTH), lambda b, i: (b, 0, i, 0))
        in_specs = [split(a) for a in (*os, *lses)] + tail_specs
        args = (*os, *lses, gz, x, gate, g, w)
        scratch = [pltpu.VMEM((GROUP_WIDTH // LANES, tm, LANES), F32)] * 6
    return pl.pallas_call(
        kern, grid=(nb, t // tm), in_specs=in_specs, out_specs=row(D_MODEL),
        out_shape=jax.ShapeDtypeStruct((nb, t, D_MODEL), F32), scratch_shapes=scratch,
        compiler_params=_ARB(2), name=name,
    )(*args)


def _inproj_b_body(x_ref, sh_ref, sc_ref, ksh_ref, ksc_ref, g_ref, gkv_ref, w_ref, wkv_ref, cos_ref, sin_ref):
    xn = _normalize(x_ref[0])
    hb = ((xn * g_ref[...]) * (1.0 + sc_ref[0]) + sh_ref[0]).astype(BF16)
    hk = ((xn * gkv_ref[...]) * (1.0 + ksc_ref[0]) + ksh_ref[0]).astype(BF16)
    cos, sin = cos_ref[...], sin_ref[...]
    q = _rope(_dot(hb, w_ref[:, 0:B_WIDTH]), cos, sin) * Q_SCALE
    gz = jax.nn.silu(_dot(hb, w_ref[:, B_WIDTH:2 * B_WIDTH]))
    kv = _dot(hk, wkv_ref[...])
    k = _rope(kv[:, 0:B_KV_WIDTH], cos, sin)
    v = kv[:, B_KV_WIDTH:2 * B_KV_WIDTH]
    return q, gz, k, v


def _inproj_b_prompt_kernel(x_ref, sh_ref, sc_ref, ksh_ref, ksc_ref, g_ref, gkv_ref, w_ref, wkv_ref, cos_ref,
                            sin_ref, q_ref, gz_ref, kd_ref, vd_ref, c_ref, *, tm, nt):
    q, gz, k, v = _inproj_b_body(x_ref, sh_ref, sc_ref, ksh_ref, ksc_ref, g_ref, gkv_ref, w_ref, wkv_ref, cos_ref,
                                 sin_ref)
    q_ref[0] = q.astype(BF16)
    gz_ref[0] = gz.astype(BF16)
    kd_ref[0] = k.astype(BF16)
    vd_ref[0] = v.astype(BF16)

    @pl.when(pl.program_id(1) == nt - 1)
    def _():
        c_ref[0, 0:B_KV_WIDTH, :] = k[tm - B_WINDOW:, :].T
        c_ref[0, B_KV_WIDTH:2 * B_KV_WIDTH, :] = v[tm - B_WINDOW:, :].T


def _inproj_b_sample_kernel(x_ref, sh_ref, sc_ref, ksh_ref, ksc_ref, g_ref, gkv_ref, w_ref, wkv_ref, cos_ref,
                            sin_ref, q_ref, gz_ref, k_ref, v_ref):
    q, gz, k, v = _inproj_b_body(x_ref, sh_ref, sc_ref, ksh_ref, ksc_ref, g_ref, gkv_ref, w_ref, wkv_ref, cos_ref,
                                 sin_ref)
    q_ref[0] = q
    gz_ref[0] = gz.astype(BF16)
    k_ref[0] = k
    v_ref[0] = v


def _inproj_b(x, shift, scale, kshift, kscale, g, gkv, w, wkv, cos, sin, *, sample):
    nb, t, _ = x.shape
    tm = min(TOKEN_TILE, t)
    nt = t // tm
    row = lambda width: pl.BlockSpec((1, tm, width), lambda b, i: (b, i, 0))
    vec = pl.BlockSpec((1, D_MODEL), lambda b, i: (0, 0))
    tab = pl.BlockSpec((tm, LANES), lambda b, i: (i, 0))
    in_specs = [row(D_MODEL), _mod_spec(shift, tm), _mod_spec(scale, tm), _mod_spec(kshift, tm),
                _mod_spec(kscale, tm), vec, vec,
                pl.BlockSpec((D_MODEL, 2 * B_WIDTH), lambda b, i: (0, 0)),
                pl.BlockSpec((D_MODEL, 2 * B_KV_WIDTH), lambda b, i: (0, 0)), tab, tab]
    args = (x, shift, scale, kshift, kscale, g, gkv, w, wkv, cos, sin)
    if sample:
        return pl.pallas_call(
            _inproj_b_sample_kernel, grid=(nb, nt), in_specs=in_specs,
            out_specs=[row(B_WIDTH), row(B_WIDTH), row(B_KV_WIDTH), row(B_KV_WIDTH)],
            out_shape=[jax.ShapeDtypeStruct((nb, t, B_WIDTH), F32), jax.ShapeDtypeStruct((nb, t, B_WIDTH), BF16),
                       jax.ShapeDtypeStruct((nb, t, B_KV_WIDTH), F32),
                       jax.ShapeDtypeStruct((nb, t, B_KV_WIDTH), F32)],
            compiler_params=_ARB(2), name="inproj_b_sample",
        )(*args)
    return pl.pallas_call(
        functools.partial(_inproj_b_prompt_kernel, tm=tm, nt=nt),
        grid=(nb, nt), in_specs=in_specs,
        out_specs=[row(B_WIDTH), row(B_WIDTH), row(B_KV_WIDTH), row(B_KV_WIDTH),
                   pl.BlockSpec((1, 2 * B_KV_WIDTH, B_WINDOW), lambda b, i: (b, 0, 0))],
        out_shape=[jax.ShapeDtypeStruct((nb, t, B_WIDTH), BF16), jax.ShapeDtypeStruct((nb, t, B_WIDTH), BF16),
                   jax.ShapeDtypeStruct((nb, t, B_KV_WIDTH), BF16),
                   jax.ShapeDtypeStruct((nb, t, B_KV_WIDTH), BF16),
                   jax.ShapeDtypeStruct((nb, 2 * B_KV_WIDTH, B_WINDOW), F32)],
        compiler_params=_ARB(2), name="inproj_b_prompt",
    )(*args)


def _eye():
    r = lax.broadcasted_iota(jnp.int32, (LANES, LANES), 0)
    c = lax.broadcasted_iota(jnp.int32, (LANES, LANES), 1)
    return r == c


def _row_to_col(row):
    eye = _eye()
    chunks = [jnp.sum(jnp.where(eye, row[:, j * LANES:(j + 1) * LANES], 0.0), axis=1, keepdims=True)
              for j in range(row.shape[1] // LANES)]
    return chunks[0] if len(chunks) == 1 else jnp.concatenate(chunks, axis=0)


def _shift_rows(cache_ref, out_ref, e, new_col):
    ntile = cache_ref.shape[2] // LANES
    lane = lax.broadcasted_iota(jnp.int32, (1, LANES), 1)
    nxt = pltpu.roll(cache_ref[e, :, 0:LANES], LANES - 1, 1)
    for j in range(ntile):
        cur = nxt
        if j + 1 < ntile:
            nxt = pltpu.roll(cache_ref[e, :, (j + 1) * LANES:(j + 2) * LANES], LANES - 1, 1)
            fill = nxt
        else:
            fill = new_col
        out_ref[e, :, j * LANES:(j + 1) * LANES] = jnp.where(lane < LANES - 1, cur, fill)


_NT = (((1,), (1,)), ((), ()))


def _sample_dilated_kernel(q_ref, kn_ref, vn_ref, cache_ref, out_cache_ref, o_ref, lse_ref, *, dil, bb):
    step = pl.program_id(0)
    width, length = q_ref.shape[1], cache_ref.shape[2]
    sel = (lax.broadcasted_iota(jnp.int32, (8, width), 0)
           == lax.broadcasted_iota(jnp.int32, (8, width), 1) // HEAD_DIM)
    valid = lax.broadcasted_iota(jnp.int32, (1, length), 1) % dil == 0
    per_head = lambda x: jnp.sum(jnp.where(sel, x, 0.0), axis=0, keepdims=True)
    for e in range(bb):
        row = pl.ds(step * bb + e, 1)
        q, kn, vn = q_ref[row, :], kn_ref[row, :], vn_ref[row, :]
        qb = jnp.where(sel, q, 0.0)
        kt = cache_ref[e, 0:width, :].astype(BF16)
        vt = cache_ref[e, width:2 * width, :].astype(BF16)
        s = jnp.where(valid, _dot(qb.astype(BF16), kt), NEG)
        s_new = jnp.sum(qb * kn, axis=1, keepdims=True)
        m = jnp.maximum(jnp.max(s, axis=1, keepdims=True), s_new)
        p = jnp.exp(s - m)
        p_new = jnp.exp(s_new - m)
        l = per_head(jnp.sum(p, axis=1, keepdims=True) + p_new)
        o = lax.dot_general(p.astype(BF16), vt, _NT, preferred_element_type=F32)
        o_ref[row, :] = (per_head(o) + per_head(p_new) * vn) * (1.0 / l)
        lse_ref[row, :] = per_head(m) + jnp.log(l)
        _shift_rows(cache_ref, out_cache_ref, e, jnp.concatenate([_row_to_col(kn), _row_to_col(vn)], axis=0))


def _sample_swa_kernel(q_ref, kn_ref, vn_ref, cache_ref, sink_ref, out_cache_ref, o_ref, qexp, oexp, *, bb):
    step = pl.program_id(0)
    nb, length = q_ref.shape[0], cache_ref.shape[2]
    low = lax.broadcasted_iota(jnp.int32, (1, LANES), 1) < HEAD_DIM

    @pl.when(step == 0)
    def _():
        for hq in range(B_Q_HEADS):
            slab = q_ref[:, (hq // 2) * LANES:(hq // 2 + 1) * LANES]
            src_low, dst_low = hq % 2 == 0, hq // B_GROUP == 0
            x = slab if src_low == dst_low else pltpu.roll(slab, HEAD_DIM, 1)
            qexp[hq * nb:(hq + 1) * nb, :] = jnp.where(low if dst_low else jnp.logical_not(low), x, 0.0)

    own_half = (lax.broadcasted_iota(jnp.int32, (B_Q_HEADS, LANES), 0) // B_GROUP
                == lax.broadcasted_iota(jnp.int32, (B_Q_HEADS, LANES), 1) // HEAD_DIM)
    valid = lax.broadcasted_iota(jnp.int32, (1, length), 1) >= 1
    sink = sink_ref[:, 0:1]
    for e in range(bb):
        b = step * bb + e
        heads = pl.ds(b, B_Q_HEADS, stride=nb)
        kn, vn = kn_ref[pl.ds(b, 1), :], vn_ref[pl.ds(b, 1), :]
        qb = qexp[heads, :]
        kt = cache_ref[e, 0:B_KV_WIDTH, :].astype(BF16)
        vt = cache_ref[e, B_KV_WIDTH:2 * B_KV_WIDTH, :].astype(BF16)
        s = jnp.where(valid, _dot(qb.astype(BF16), kt), NEG)
        s_new = jnp.sum(qb * kn, axis=1, keepdims=True)
        m = jnp.maximum(jnp.maximum(jnp.max(s, axis=1, keepdims=True), s_new), sink)
        p = jnp.exp(s - m)
        p_new = jnp.exp(s_new - m)
        l = jnp.sum(p, axis=1, keepdims=True) + p_new + jnp.exp(sink - m)
        o = lax.dot_general(p.astype(BF16), vt, _NT, preferred_element_type=F32)
        oexp[heads, :] = jnp.where(own_half, o + p_new * vn, 0.0) * (1.0 / l)
        _shift_rows(cache_ref, out_cache_ref, e, jnp.concatenate([_row_to_col(kn), _row_to_col(vn)], axis=0))

    @pl.when(step == pl.num_programs(0) - 1)
    def _():
        for j in range(B_Q_HEADS // 2):
            even, odd = oexp[2 * j * nb:(2 * j + 1) * nb, :], oexp[(2 * j + 1) * nb:(2 * j + 2) * nb, :]
            if 2 * j // B_GROUP == 0:
                odd = pltpu.roll(odd, HEAD_DIM, 1)
            else:
                even = pltpu.roll(even, HEAD_DIM, 1)
            o_ref[:, j * LANES:(j + 1) * LANES] = jnp.where(low, even, odd)


def _sample_attn(q, k_new, v_new, cache_t, *, dil=1, sinks=None):
    nb, chans, length = cache_t.shape
    wq, wkv = q.shape[1], k_new.shape[1]
    bb = max(1, min(8, 2048 // length))
    full = lambda w: pl.BlockSpec((nb, w), lambda s: (0, 0))
    tile = pl.BlockSpec((bb, chans, length), lambda s: (s, 0, 0))
    in_specs = [full(wq), full(wkv), full(wkv), tile]
    args = [q, k_new, v_new, cache_t]
    out_specs = [tile, full(wq)]
    out_shape = [jax.ShapeDtypeStruct(cache_t.shape, F32), jax.ShapeDtypeStruct((nb, wq), F32)]
    if sinks is None:
        kern = functools.partial(_sample_dilated_kernel, dil=dil, bb=bb)
        out_specs.append(full(wq))
        out_shape.append(jax.ShapeDtypeStruct((nb, wq), F32))
        scratch = []
    else:
        kern = functools.partial(_sample_swa_kernel, bb=bb)
        in_specs.append(pl.BlockSpec(sinks.shape, lambda s: (0, 0)))
        args.append(sinks)
        scratch = [pltpu.VMEM((B_Q_HEADS * nb, LANES), F32)] * 2
    return pl.pallas_call(
        kern, grid=(nb // bb,), in_specs=in_specs, out_specs=out_specs, out_shape=out_shape,
        scratch_shapes=scratch, compiler_params=_ARB(1),
        name=f"sample_attn_d{dil}_l{length}" if sinks is None else "sample_swa",
    )(*args)


def _rope_tables(pos):
    half = HEAD_DIM // 2
    inv = ROPE_THETA ** (-jnp.arange(half, dtype=F32) / half)
    ang = pos.astype(F32)[:, None] * inv[None, :]
    cos, sin = jnp.cos(ang), jnp.sin(ang)
    return jnp.tile(cos, (1, 4)), jnp.tile(jnp.concatenate([-sin, sin], axis=1), (1, 2))


def _to_tiles(cache):
    nb, length = cache.shape[0], cache.shape[1]
    return jnp.transpose(cache, (0, 2, 3, 4, 1)).reshape(nb, -1, length)


def _from_tiles(tiles, heads):
    nb, _, length = tiles.shape
    return jnp.transpose(tiles.reshape(nb, 2, heads, HEAD_DIM, length), (0, 4, 1, 2, 3))


def kernel(x_prompt, x_sample, c_prompt, c_sample, cache_a_kv_g0, cache_a_kv_g1, cache_a_kv_g2, cache_b_kv, ada_w,
           ada_b, g_pre, g_post, w_in_a, w_o_a, w_in_b, w_o_b, sinks_b, ada_kv_w, ada_kv_b, g_kv, w_kv):
    nbp, t, _ = x_prompt.shape
    nbs = x_sample.shape[0]
    assert x_sample.shape[1] == 1

    c_all = jnp.concatenate([c_prompt, c_sample], axis=0)
    mod = _ada(c_all, ada_w, ada_b)
    mod_kv = _ada(c_all, ada_kv_w[None], ada_kv_b[None])[0]

    def split(a, parts, sample):
        rows = a[nbp:][None] if sample else a[:nbp][:, None]
        return [rows[..., p * D_MODEL:(p + 1) * D_MODEL] for p in range(parts)]

    w_in_a_b, w_o_a_b = w_in_a[0].astype(BF16), w_o_a[0].astype(BF16)
    w_in_b_b, w_o_b_b, w_kv_b = w_in_b[0].astype(BF16), w_o_b[0].astype(BF16), w_kv.astype(BF16)
    g_pre0, g_pre1 = g_pre[0:1], g_pre[1:2]
    g_post0, g_post1 = g_post[0:1], g_post[1:2]
    g_kv_r = g_kv[None]
    sinks = jnp.broadcast_to(sinks_b[0][:, None], (B_Q_HEADS, LANES))

    cos_p, sin_p = _rope_tables(jnp.arange(t))
    sh0, sc0, gt0 = split(mod[0], 3, False)
    sh1, sc1, gt1 = split(mod[1], 3, False)
    ksh, ksc = split(mod_kv, 2, False)
    *qkv, gz, ca0, ca1, ca2 = _inproj_a(x_prompt, sh0, sc0, g_pre0, w_in_a_b, cos_p, sin_p, sample=False)
    os, lses = zip(*[_dilated_attn(*qkv[3 * g:3 * g + 3], g) for g in range(len(A_GROUPS))])
    x1 = _outproj(os, lses, gz, x_prompt, gt0, g_post0, w_o_a_b)
    qb, gzb, kd, vd, cb = _inproj_b(x1, sh1, sc1, ksh, ksc, g_pre1, g_kv_r, w_in_b_b, w_kv_b, cos_p, sin_p,
                                    sample=False)
    ob = _swa_attn(qb, kd, vd, sinks)
    y_prompt = _outproj([ob], None, gzb, x1, gt1, g_post1, w_o_b_b)
    new_a_prompt = [_from_tiles(c, A_HEADS_PER_GROUP)[None] for c in (ca0, ca1, ca2)]
    new_b_prompt = _from_tiles(cb, B_KV_HEADS)

    xs = x_sample.reshape(1, nbs, D_MODEL)
    cos_s, sin_s = _rope_tables(jnp.full((nbs,), PAST_LEN, jnp.int32))
    sh0, sc0, gt0 = split(mod[0], 3, True)
    sh1, sc1, gt1 = split(mod[1], 3, True)
    ksh, ksc = split(mod_kv, 2, True)
    q, k, v, gz = _inproj_a(xs, sh0, sc0, g_pre0, w_in_a_b, cos_s, sin_s, sample=True)
    os, lses, new_a_sample = [], [], []
    for g, cache in enumerate((cache_a_kv_g0, cache_a_kv_g1, cache_a_kv_g2)):
        cols = slice(g * GROUP_WIDTH, (g + 1) * GROUP_WIDTH)
        shifted, o, lse = _sample_attn(q[0, :, cols], k[0, :, cols], v[0, :, cols], _to_tiles(cache[0]),
                                       dil=A_GROUPS[g][1])
        os.append(o.astype(BF16)[None, None])
        lses.append(lse[None, None])
        new_a_sample.append(_from_tiles(shifted, A_HEADS_PER_GROUP)[None])
    xs1 = _outproj(os, lses, gz, xs, gt0, g_post0, w_o_a_b)
    qb, gzb, kb, vb = _inproj_b(xs1, sh1, sc1, ksh, ksc, g_pre1, g_kv_r, w_in_b_b, w_kv_b, cos_s, sin_s,
                                sample=True)
    shifted_b, ob = _sample_attn(qb[0], kb[0], vb[0], _to_tiles(cache_b_kv), sinks=sinks)
    y_sample = _outproj([ob.astype(BF16)[None]], None, gzb, xs1, gt1, g_post1, w_o_b_b).reshape(nbs, 1, D_MODEL)
    new_b_sample = _from_tiles(shifted_b, B_KV_HEADS)

    return (y_prompt, y_sample, *new_a_prompt, new_b_prompt, *new_a_sample, new_b_sample)
```

```python
import functools
from typing import NamedTuple, Optional

import jax
import jax.numpy as jnp
from jax import lax
from jax.experimental import pallas as pl
from jax.experimental.pallas import tpu as pltpu

D_MODEL = 1024
HEAD_DIM = 64
A_GROUPS = ((128, 1), (512, 4), (2048, 16))
A_HEADS_PER_GROUP = 4
GROUP_WIDTH = A_HEADS_PER_GROUP * HEAD_DIM
A_WIDTH = GROUP_WIDTH * len(A_GROUPS)
B_Q_HEADS = 16
B_KV_HEADS = 2
B_GROUP = B_Q_HEADS // B_KV_HEADS
B_WIDTH = B_Q_HEADS * HEAD_DIM
B_KV_WIDTH = B_KV_HEADS * HEAD_DIM
B_WINDOW = 128
BLOCK = 128
PAST_LEN = 16384
ROPE_THETA = 10000.0
EPS = 1e-6
NEG = -1e30
LANES = 128
Q_SCALE = HEAD_DIM ** -0.5

BF16 = jnp.bfloat16
F32 = jnp.float32
TOKEN_TILE = 512

_ARB = lambda n: pltpu.CompilerParams(dimension_semantics=("arbitrary",) * n)


def _rope(x, cos, sin_signed):
    lane = lax.broadcasted_iota(jnp.int32, (1, LANES), 1)
    first_half = (lane % HEAD_DIM) < (HEAD_DIM // 2)
    outs = []
    for j in range(x.shape[1] // LANES):
        xs = x[:, j * LANES:(j + 1) * LANES]
        partner = jnp.where(first_half, pltpu.roll(xs, LANES - 32, 1), pltpu.roll(xs, 32, 1))
        outs.append(xs * cos + partner * sin_signed)
    return outs[0] if len(outs) == 1 else jnp.concatenate(outs, axis=1)


def _normalize(x):
    return x * lax.rsqrt(jnp.mean(x * x, axis=-1, keepdims=True) + EPS)


def _dot(a, b):
    return jnp.dot(a, b, preferred_element_type=F32)


def _ada_kernel(c_ref, w_ref, b_ref, o_ref):
    s = jax.nn.silu(c_ref[...]).astype(BF16)
    o_ref[...] = _dot(s, w_ref[...].astype(BF16)) + b_ref[...]


def _ada(c, w, b):
    nl, _, n = w.shape
    m = c.shape[0]
    tn = 1024
    return pl.pallas_call(
        _ada_kernel,
        grid=(nl, n // tn),
        in_specs=[
            pl.BlockSpec((m, D_MODEL), lambda l, j: (0, 0)),
            pl.BlockSpec((None, D_MODEL, tn), lambda l, j: (l, 0, j)),
            pl.BlockSpec((None, 1, tn), lambda l, j: (l, 0, j)),
        ],
        out_specs=pl.BlockSpec((None, m, tn), lambda l, j: (l, 0, j)),
        out_shape=jax.ShapeDtypeStruct((nl, m, n), F32),
        compiler_params=_ARB(2),
        name="ada",
    )(c, w, b.reshape(nl, 1, n))


def _inproj_a_body(x_ref, sh_ref, sc_ref, g_ref, w_ref, cos_ref, sin_ref):
    h = _normalize(x_ref[0]) * g_ref[...]
    hb = (h * (1.0 + sc_ref[0]) + sh_ref[0]).astype(BF16)
    cos, sin = cos_ref[...], sin_ref[...]
    q = _rope(_dot(hb, w_ref[:, 0:A_WIDTH]), cos, sin) * Q_SCALE
    k = _rope(_dot(hb, w_ref[:, A_WIDTH:2 * A_WIDTH]), cos, sin)
    v = _dot(hb, w_ref[:, 2 * A_WIDTH:3 * A_WIDTH])
    gz = jax.nn.silu(_dot(hb, w_ref[:, 3 * A_WIDTH:4 * A_WIDTH]))
    return q, k, v, gz


def _inproj_a_prompt_kernel(x_ref, sh_ref, sc_ref, g_ref, w_ref, cos_ref, sin_ref, *rest, tm, nt, job):
    job_in, main_out, job_out, (scr,) = _split_job_refs(rest, job, 13)
    qkv_refs, (gz_ref, c0_ref, c1_ref, c2_ref) = main_out[:9], main_out[9:]
    i = pl.program_id(1)
    if job:
        _run_job(job, job_in, job_out, pl.program_id(0) * nt + i)
    q, k, v, gz = _inproj_a_body(x_ref, sh_ref, sc_ref, g_ref, w_ref, cos_ref, sin_ref)
    gz_ref[0] = gz.astype(BF16)
    slabs_per_group = GROUP_WIDTH // LANES
    for which, val in enumerate((q, k, v)):
        for j in range(A_WIDTH // LANES):
            scr[j] = val[:, j * LANES:(j + 1) * LANES]
        for g, (_, dil) in enumerate(A_GROUPS):
            out_ref = qkv_refs[3 * g + which]
            for r in range(dil):
                for h in range(slabs_per_group):
                    rows = scr[g * slabs_per_group + h, pl.ds(r, tm // dil, stride=dil), :]
                    out_ref[0, r, :, h * LANES:(h + 1) * LANES] = rows.astype(BF16)
    for g, c_ref in enumerate((c0_ref, c1_ref, c2_ref)):
        win = A_GROUPS[g][0]
        rows = min(tm, win)
        first_tile = nt - max(win // tm, 1)

        @pl.when(i >= first_tile)
        def _(g=g, c_ref=c_ref, rows=rows):
            cols = slice(g * GROUP_WIDTH, (g + 1) * GROUP_WIDTH)
            c_ref[0, 0:GROUP_WIDTH, :] = k[tm - rows:, cols].T
            c_ref[0, GROUP_WIDTH:2 * GROUP_WIDTH, :] = v[tm - rows:, cols].T


def _inproj_a_sample_kernel(x_ref, sh_ref, sc_ref, g_ref, w_ref, cos_ref, sin_ref,
                            q_ref, k_ref, v_ref, gz_ref):
    q, k, v, gz = _inproj_a_body(x_ref, sh_ref, sc_ref, g_ref, w_ref, cos_ref, sin_ref)
    q_ref[0] = q
    k_ref[0] = k
    v_ref[0] = v
    gz_ref[0] = gz.astype(BF16)


def _mod_spec(mod, tm):
    if mod.shape[1] == 1:
        return pl.BlockSpec((1, 1, D_MODEL), lambda b, i: (b, 0, 0))
    return pl.BlockSpec((1, tm, D_MODEL), lambda b, i: (b, i, 0))


def _inproj_a(x, shift, scale, g, w, cos, sin, *, sample, job=None):
    nb, t, _ = x.shape
    tm = min(TOKEN_TILE, t)
    nt = t // tm
    row = lambda width: pl.BlockSpec((1, tm, width), lambda b, i: (b, i, 0))
    in_specs = [
        row(D_MODEL), _mod_spec(shift, tm), _mod_spec(scale, tm),
        pl.BlockSpec((1, D_MODEL), lambda b, i: (0, 0)),
        pl.BlockSpec((D_MODEL, 4 * A_WIDTH), lambda b, i: (0, 0)),
        pl.BlockSpec((tm, LANES), lambda b, i: (i, 0)),
        pl.BlockSpec((tm, LANES), lambda b, i: (i, 0)),
    ]
    if sample:
        out_shape = [jax.ShapeDtypeStruct((nb, t, A_WIDTH), dt) for dt in (F32, F32, F32, BF16)]
        return pl.pallas_call(
            _inproj_a_sample_kernel, grid=(nb, nt), in_specs=in_specs,
            out_specs=[row(A_WIDTH)] * 4, out_shape=out_shape,
            compiler_params=_ARB(2), name="inproj_a_sample",
        )(x, shift, scale, g, w, cos, sin)
    out_shape, out_specs = [], []
    for _, dil in A_GROUPS:
        out_shape += [jax.ShapeDtypeStruct((nb, dil, t // dil, GROUP_WIDTH), BF16)] * 3
        out_specs += [pl.BlockSpec((1, dil, tm // dil, GROUP_WIDTH), lambda b, i: (b, 0, i, 0))] * 3
    out_shape.append(jax.ShapeDtypeStruct((nb, t, A_WIDTH), BF16))
    out_specs.append(row(A_WIDTH))
    for win, _ in A_GROUPS:
        rows = min(tm, win)
        first_tile = nt - max(win // tm, 1)
        out_shape.append(jax.ShapeDtypeStruct((nb, 2 * GROUP_WIDTH, min(win, t)), F32))
        out_specs.append(pl.BlockSpec(
            (1, 2 * GROUP_WIDTH, rows),
            lambda b, i, first_tile=first_tile: (b, 0, jnp.maximum(i - first_tile, 0))))
    args, aliases = [x, shift, scale, g, w, cos, sin], {}
    if job is not None:
        assert job.q.shape[0] == nb * nt * job.per_step
        j_in, j_args, j_out, j_shape, aliases = _job_operands(job, lambda b, i: b * nt + i, len(args),
                                                              len(out_specs))
        in_specs, args = in_specs + j_in, args + j_args
        out_specs, out_shape = out_specs + j_out, out_shape + j_shape
    return pl.pallas_call(
        functools.partial(_inproj_a_prompt_kernel, tm=tm, nt=nt, job=job.static if job else None),
        grid=(nb, nt), in_specs=in_specs, out_specs=out_specs, out_shape=out_shape,
        scratch_shapes=[pltpu.VMEM((A_WIDTH // LANES, tm, LANES), F32)],
        input_output_aliases=aliases, compiler_params=_ARB(2), name="inproj_a_prompt",
    )(*args)


def _query_minus_key():
    kj = lax.broadcasted_iota(jnp.int32, (2 * BLOCK, BLOCK), 0)
    qi = lax.broadcasted_iota(jnp.int32, (2 * BLOCK, BLOCK), 1)
    return qi - kj


def _head_attend(st, mask, vt_ext, sink):
    st = jnp.where(mask, st, NEG)
    m = jnp.max(st, axis=0, keepdims=True)
    if sink is not None:
        m = jnp.maximum(m, sink)
    ext = _dot(vt_ext, jnp.exp(st - m).astype(BF16))
    l = ext[HEAD_DIM:HEAD_DIM + 1, :]
    if sink is not None:
        l = l + jnp.exp(sink - m)
    return ext[0:HEAD_DIM, :] * (1.0 / l), m, l


def _values_ext(vt_prev, vt_cur, head):
    rows = slice(head * HEAD_DIM, (head + 1) * HEAD_DIM)
    ones = jnp.ones((16, 2 * BLOCK), BF16)
    return jnp.concatenate([jnp.concatenate([vt_prev[rows], vt_cur[rows]], axis=1), ones], axis=0)


def _block_diag_queries(qt, heads, kv_of_head, kv_heads):
    zeros = jnp.zeros((HEAD_DIM, BLOCK), BF16)
    cols = []
    for h in range(heads):
        pieces = [zeros] * kv_heads
        pieces[kv_of_head(h)] = qt[h * HEAD_DIM:(h + 1) * HEAD_DIM]
        cols.append(jnp.concatenate(pieces, axis=0))
    return jnp.concatenate(cols, axis=1)


def _dilated_attn_kernel(q_ref, k_ref, v_ref, o_ref, lse_ref, vt_ref):
    nres, m_rows = q_ref.shape[1], q_ref.shape[2]
    nblk = m_rows // BLOCK
    base = _query_minus_key()
    heads = A_HEADS_PER_GROUP
    for r in range(nres):
        def transpose_values(j, c, r=r):
            rows = pl.ds(pl.multiple_of(j * BLOCK, BLOCK), BLOCK)
            vt_ref[j] = v_ref[0, r, rows, :].T
            return c

        lax.fori_loop(0, nblk, transpose_values, 0)

        def block(i, c, r=r):
            prev = jnp.maximum(i - 1, 0)
            dist = base + (i - prev) * BLOCK
            mask = (dist >= 0) & (dist <= BLOCK)
            qrows = pl.ds(pl.multiple_of(i * BLOCK, BLOCK), BLOCK)
            k = k_ref[0, r, pl.ds(pl.multiple_of(prev * BLOCK, BLOCK), 2 * BLOCK), :]
            qd = _block_diag_queries(q_ref[0, r, qrows, :].T, heads, lambda h: h, heads)
            st_all = _dot(k, qd)
            vt_prev, vt_cur = vt_ref[prev], vt_ref[i]
            outs, lses = [], []
            for h in range(heads):
                o_t, m, l = _head_attend(st_all[:, h * BLOCK:(h + 1) * BLOCK], mask,
                                         _values_ext(vt_prev, vt_cur, h), None)
                outs.append(o_t)
                lses.append(jnp.broadcast_to(m + jnp.log(l), (HEAD_DIM, BLOCK)))
            o_ref[0, r, qrows, :] = jnp.concatenate(outs, axis=0).T.astype(BF16)
            lse_ref[0, r, qrows, :] = jnp.concatenate(lses, axis=0).T
            return c

        lax.fori_loop(0, nblk, block, 0, unroll=2)


def _dilated_attn(q, k, v, g):
    nb, dil, m_rows, _ = q.shape
    nres = max(1, dil // 4)
    blk = pl.BlockSpec((1, nres, m_rows, GROUP_WIDTH), lambda b, r: (b, r, 0, 0))
    return pl.pallas_call(
        _dilated_attn_kernel,
        grid=(nb, dil // nres),
        in_specs=[blk, blk, blk],
        out_specs=[blk, blk],
        out_shape=[jax.ShapeDtypeStruct(q.shape, BF16), jax.ShapeDtypeStruct(q.shape, F32)],
        scratch_shapes=[pltpu.VMEM((m_rows // BLOCK, GROUP_WIDTH, BLOCK), BF16)],
        compiler_params=_ARB(2), name=f"dilated_attn_g{g}",
    )(q, k, v)


SWA_QUERY_TILE = 1024


def _swa_attn_kernel(q_ref, k_ref, v_ref, sink_ref, *rest, job):
    job_in, (o_ref,), job_out, (vt_ref,) = _split_job_refs(rest, job, 1)
    j = pl.program_id(1)
    if job:
        _run_job(job, job_in, job_out, pl.program_id(0) * pl.num_programs(1) + j)
    seq = k_ref.shape[1]
    nq = q_ref.shape[1] // BLOCK
    base = _query_minus_key()

    @pl.when(j == 0)
    def _():
        def transpose_values(t, c):
            vt_ref[t] = v_ref[0, pl.ds(pl.multiple_of(t * BLOCK, BLOCK), BLOCK), :].T
            return c

        lax.fori_loop(0, seq // BLOCK, transpose_values, 0)

    def block(i, c):
        cur = j * nq + i
        prev = jnp.maximum(cur - 1, 0)
        dist = base + (cur - prev) * BLOCK
        mask = (dist >= 0) & (dist < B_WINDOW)
        qrows = pl.ds(pl.multiple_of(i * BLOCK, BLOCK), BLOCK)
        k = k_ref[0, pl.ds(pl.multiple_of(prev * BLOCK, BLOCK), 2 * BLOCK), :]
        qt = q_ref[0, qrows, :].T
        vt_prev, vt_cur = vt_ref[prev], vt_ref[cur]
        outs = []
        for kvh in range(B_KV_HEADS):
            qd = _block_diag_queries(qt[kvh * B_GROUP * HEAD_DIM:(kvh + 1) * B_GROUP * HEAD_DIM], B_GROUP,
                                     lambda h, kvh=kvh: kvh, B_KV_HEADS)
            st_all = _dot(k, qd)
            vt_ext = _values_ext(vt_prev, vt_cur, kvh)
            for gq in range(B_GROUP):
                hq = kvh * B_GROUP + gq
                o_t, _, _ = _head_attend(st_all[:, gq * BLOCK:(gq + 1) * BLOCK], mask, vt_ext,
                                         sink_ref[hq:hq + 1, :])
                outs.append(o_t)
        o_ref[0, qrows, :] = jnp.concatenate(outs, axis=0).T.astype(BF16)
        return c

    lax.fori_loop(0, nq, block, 0)


def _swa_attn(q, k, v, sinks, job=None):
    nb, t, _ = q.shape
    tq = min(SWA_QUERY_TILE, t)
    nj = t // tq
    qblk = pl.BlockSpec((1, tq, B_WIDTH), lambda b, j: (b, j, 0))
    kvblk = pl.BlockSpec((1, t, B_KV_WIDTH), lambda b, j: (b, 0, 0))
    in_specs = [qblk, kvblk, kvblk, pl.BlockSpec((B_Q_HEADS, LANES), lambda b, j: (0, 0))]
    args, aliases = [q, k, v, sinks], {}
    out_specs, out_shape = [qblk], [jax.ShapeDtypeStruct((nb, t, B_WIDTH), BF16)]
    if job is not None:
        assert job.q.shape[0] == nb * nj * job.per_step
        j_in, j_args, j_out, j_shape, aliases = _job_operands(job, lambda b, j: b * nj + j, len(args), 1)
        in_specs, args = in_specs + j_in, args + j_args
        out_specs, out_shape = out_specs + j_out, out_shape + j_shape
    return pl.pallas_call(
        functools.partial(_swa_attn_kernel, job=job.static if job else None),
        grid=(nb, nj), in_specs=in_specs, out_specs=out_specs, out_shape=out_shape,
        scratch_shapes=[pltpu.VMEM((t // BLOCK, B_KV_WIDTH, BLOCK), BF16)],
        input_output_aliases=aliases, compiler_params=_ARB(2), name="swa_attn",
    )(*args)


def _finish(a_parts, w_ref, x_ref, gate_ref, g_ref, y_ref):
    acc = None
    row = 0
    for a in a_parts:
        part = _dot(a, w_ref[row:row + a.shape[1], :])
        acc = part if acc is None else acc + part
        row += a.shape[1]
    y_ref[0] = x_ref[0] + gate_ref[0] * (_normalize(acc) * g_ref[...])


def _in_row_order(ref, scr):
    dil = ref.shape[1]
    if dil == 1:
        return ref[0, 0].astype(F32)
    nslab = ref.shape[3] // LANES
    for r in range(dil):
        for h in range(nslab):
            scr[h, pl.ds(r, ref.shape[2], stride=dil), :] = ref[0, r, :, h * LANES:(h + 1) * LANES].astype(F32)
    return jnp.concatenate([scr[h] for h in range(nslab)], axis=1)


def _outproj_mix_kernel(o0_ref, o1_ref, o2_ref, l0_ref, l1_ref, l2_ref, gz_ref, x_ref, gate_ref, g_ref, w_ref,
                        y_ref, *scratch):
    os = [_in_row_order(r, s) for r, s in zip((o0_ref, o1_ref, o2_ref), scratch[0:3])]
    lses = [_in_row_order(r, s) for r, s in zip((l0_ref, l1_ref, l2_ref), scratch[3:6])]
    top = jnp.maximum(jnp.maximum(lses[0], lses[1]), lses[2])
    es = [jnp.exp(l - top) for l in lses]
    inv = 1.0 / (es[0] + es[1] + es[2])
    parts = []
    for g in range(len(A_GROUPS)):
        gz = gz_ref[0, :, g * GROUP_WIDTH:(g + 1) * GROUP_WIDTH].astype(F32)
        parts.append((os[g] * (es[g] * inv) * gz).astype(BF16))
    _finish(parts, w_ref, x_ref, gate_ref, g_ref, y_ref)


def _outproj_kernel(o_ref, gz_ref, x_ref, gate_ref, g_ref, w_ref, y_ref):
    a = (o_ref[0].astype(F32) * gz_ref[0].astype(F32)).astype(BF16)
    _finish([a], w_ref, x_ref, gate_ref, g_ref, y_ref)


def _outproj(os, lses, gz, x, gate, g, w):
    nb, t, _ = x.shape
    tm = min(TOKEN_TILE, t)
    row = lambda width: pl.BlockSpec((1, tm, width), lambda b, i: (b, i, 0))
    width = w.shape[0]
    tail_specs = [row(width), row(D_MODEL), _mod_spec(gate, tm),
                  pl.BlockSpec((1, D_MODEL), lambda b, i: (0, 0)),
                  pl.BlockSpec((width, D_MODEL), lambda b, i: (0, 0))]
    scratch = []
    if lses is None:
        kern, name = _outproj_kernel, "outproj_b"
        in_specs = [row(width)] + tail_specs
        args = (os[0], gz, x, gate, g, w)
    else:
        kern, name = _outproj_mix_kernel, "outproj_a"
        split = lambda a: pl.BlockSpec((1, a.shape[1], tm // a.shape[1], GROUP_WIDTH), lambda b, i: (b, 0, i, 0))
        in_specs = [split(a) for a in (*os, *lses)] + tail_specs
        args = (*os, *lses, gz, x, gate, g, w)
        scratch = [pltpu.VMEM((GROUP_WIDTH // LANES, tm, LANES), F32)] * 6
    return pl.pallas_call(
        kern, grid=(nb, t // tm), in_specs=in_specs, out_specs=row(D_MODEL),
        out_shape=jax.ShapeDtypeStruct((nb, t, D_MODEL), F32), scratch_shapes=scratch,
        compiler_params=_ARB(2), name=name,
    )(*args)


def _inproj_b_body(x_ref, sh_ref, sc_ref, ksh_ref, ksc_ref, g_ref, gkv_ref, w_ref, wkv_ref, cos_ref, sin_ref):
    xn = _normalize(x_ref[0])
    hb = ((xn * g_ref[...]) * (1.0 + sc_ref[0]) + sh_ref[0]).astype(BF16)
    hk = ((xn * gkv_ref[...]) * (1.0 + ksc_ref[0]) + ksh_ref[0]).astype(BF16)
    cos, sin = cos_ref[...], sin_ref[...]
    q = _rope(_dot(hb, w_ref[:, 0:B_WIDTH]), cos, sin) * Q_SCALE
    gz = jax.nn.silu(_dot(hb, w_ref[:, B_WIDTH:2 * B_WIDTH]))
    kv = _dot(hk, wkv_ref[...])
    k = _rope(kv[:, 0:B_KV_WIDTH], cos, sin)
    v = kv[:, B_KV_WIDTH:2 * B_KV_WIDTH]
    return q, gz, k, v


def _inproj_b_prompt_kernel(x_ref, sh_ref, sc_ref, ksh_ref, ksc_ref, g_ref, gkv_ref, w_ref, wkv_ref, cos_ref,
                            sin_ref, *rest, tm, nt, job):
    job_in, (q_ref, gz_ref, kd_ref, vd_ref, c_ref), job_out, _ = _split_job_refs(rest, job, 5)
    if job:
        _run_job(job, job_in, job_out, pl.program_id(0) * nt + pl.program_id(1))
    q, gz, k, v = _inproj_b_body(x_ref, sh_ref, sc_ref, ksh_ref, ksc_ref, g_ref, gkv_ref, w_ref, wkv_ref, cos_ref,
                                 sin_ref)
    q_ref[0] = q.astype(BF16)
    gz_ref[0] = gz.astype(BF16)
    kd_ref[0] = k.astype(BF16)
    vd_ref[0] = v.astype(BF16)

    @pl.when(pl.program_id(1) == nt - 1)
    def _():
        c_ref[0, 0:B_KV_WIDTH, :] = k[tm - B_WINDOW:, :].T
        c_ref[0, B_KV_WIDTH:2 * B_KV_WIDTH, :] = v[tm - B_WINDOW:, :].T


def _inproj_b_sample_kernel(x_ref, sh_ref, sc_ref, ksh_ref, ksc_ref, g_ref, gkv_ref, w_ref, wkv_ref, cos_ref,
                            sin_ref, q_ref, gz_ref, k_ref, v_ref):
    q, gz, k, v = _inproj_b_body(x_ref, sh_ref, sc_ref, ksh_ref, ksc_ref, g_ref, gkv_ref, w_ref, wkv_ref, cos_ref,
                                 sin_ref)
    q_ref[0] = q
    gz_ref[0] = gz.astype(BF16)
    k_ref[0] = k
    v_ref[0] = v


def _inproj_b(x, shift, scale, kshift, kscale, g, gkv, w, wkv, cos, sin, *, sample, job=None):
    nb, t, _ = x.shape
    tm = min(TOKEN_TILE, t)
    nt = t // tm
    row = lambda width: pl.BlockSpec((1, tm, width), lambda b, i: (b, i, 0))
    vec = pl.BlockSpec((1, D_MODEL), lambda b, i: (0, 0))
    tab = pl.BlockSpec((tm, LANES), lambda b, i: (i, 0))
    in_specs = [row(D_MODEL), _mod_spec(shift, tm), _mod_spec(scale, tm), _mod_spec(kshift, tm),
                _mod_spec(kscale, tm), vec, vec,
                pl.BlockSpec((D_MODEL, 2 * B_WIDTH), lambda b, i: (0, 0)),
                pl.BlockSpec((D_MODEL, 2 * B_KV_WIDTH), lambda b, i: (0, 0)), tab, tab]
    args = (x, shift, scale, kshift, kscale, g, gkv, w, wkv, cos, sin)
    if sample:
        return pl.pallas_call(
            _inproj_b_sample_kernel, grid=(nb, nt), in_specs=in_specs,
            out_specs=[row(B_WIDTH), row(B_WIDTH), row(B_KV_WIDTH), row(B_KV_WIDTH)],
            out_shape=[jax.ShapeDtypeStruct((nb, t, B_WIDTH), F32), jax.ShapeDtypeStruct((nb, t, B_WIDTH), BF16),
                       jax.ShapeDtypeStruct((nb, t, B_KV_WIDTH), F32),
                       jax.ShapeDtypeStruct((nb, t, B_KV_WIDTH), F32)],
            compiler_params=_ARB(2), name="inproj_b_sample",
        )(*args)
    out_specs = [row(B_WIDTH), row(B_WIDTH), row(B_KV_WIDTH), row(B_KV_WIDTH),
                 pl.BlockSpec((1, 2 * B_KV_WIDTH, B_WINDOW), lambda b, i: (b, 0, 0))]
    out_shape = [jax.ShapeDtypeStruct((nb, t, B_WIDTH), BF16), jax.ShapeDtypeStruct((nb, t, B_WIDTH), BF16),
                 jax.ShapeDtypeStruct((nb, t, B_KV_WIDTH), BF16), jax.ShapeDtypeStruct((nb, t, B_KV_WIDTH), BF16),
                 jax.ShapeDtypeStruct((nb, 2 * B_KV_WIDTH, B_WINDOW), F32)]
    args, aliases = list(args), {}
    if job is not None:
        assert job.q.shape[0] == nb * nt * job.per_step
        j_in, j_args, j_out, j_shape, aliases = _job_operands(job, lambda b, i: b * nt + i, len(args),
                                                              len(out_specs))
        in_specs, args = in_specs + j_in, args + j_args
        out_specs, out_shape = out_specs + j_out, out_shape + j_shape
    return pl.pallas_call(
        functools.partial(_inproj_b_prompt_kernel, tm=tm, nt=nt, job=job.static if job else None),
        grid=(nb, nt), in_specs=in_specs, out_specs=out_specs, out_shape=out_shape,
        input_output_aliases=aliases, compiler_params=_ARB(2), name="inproj_b_prompt",
    )(*args)


def _eye():
    r = lax.broadcasted_iota(jnp.int32, (LANES, LANES), 0)
    c = lax.broadcasted_iota(jnp.int32, (LANES, LANES), 1)
    return r == c


def _row_to_col(row):
    eye = _eye()
    chunks = [jnp.sum(jnp.where(eye, row[:, j * LANES:(j + 1) * LANES], 0.0), axis=1, keepdims=True)
              for j in range(row.shape[1] // LANES)]
    return chunks[0] if len(chunks) == 1 else jnp.concatenate(chunks, axis=0)


def _shift_rows(cache_ref, out_ref, e, new_col):
    ntile = cache_ref.shape[2] // LANES
    lane = lax.broadcasted_iota(jnp.int32, (1, LANES), 1)
    nxt = pltpu.roll(cache_ref[e, :, 0:LANES], LANES - 1, 1)
    for j in range(ntile):
        cur = nxt
        if j + 1 < ntile:
            nxt = pltpu.roll(cache_ref[e, :, (j + 1) * LANES:(j + 2) * LANES], LANES - 1, 1)
            fill = nxt
        else:
            fill = new_col
        out_ref[e, :, j * LANES:(j + 1) * LANES] = jnp.where(lane < LANES - 1, cur, fill)


_NT = (((1,), (1,)), ((), ()))


def _sample_dilated_step(q_ref, kn_ref, vn_ref, cache_ref, out_cache_ref, o_ref, lse_ref, step, *, dil, bb):
    width, length = q_ref.shape[1], cache_ref.shape[2]
    sel = (lax.broadcasted_iota(jnp.int32, (8, width), 0)
           == lax.broadcasted_iota(jnp.int32, (8, width), 1) // HEAD_DIM)
    valid = lax.broadcasted_iota(jnp.int32, (1, length), 1) % dil == 0
    per_head = lambda x: jnp.sum(jnp.where(sel, x, 0.0), axis=0, keepdims=True)
    for e in range(bb):
        row = pl.ds(step * bb + e, 1)
        q, kn, vn = q_ref[row, :], kn_ref[row, :], vn_ref[row, :]
        qb = jnp.where(sel, q, 0.0)
        kt = cache_ref[e, 0:width, :].astype(BF16)
        vt = cache_ref[e, width:2 * width, :].astype(BF16)
        s = jnp.where(valid, _dot(qb.astype(BF16), kt), NEG)
        s_new = jnp.sum(qb * kn, axis=1, keepdims=True)
        m = jnp.maximum(jnp.max(s, axis=1, keepdims=True), s_new)
        p = jnp.exp(s - m)
        p_new = jnp.exp(s_new - m)
        l = per_head(jnp.sum(p, axis=1, keepdims=True) + p_new)
        o = lax.dot_general(p.astype(BF16), vt, _NT, preferred_element_type=F32)
        o_ref[row, :] = (per_head(o) + per_head(p_new) * vn) * (1.0 / l)
        lse_ref[row, :] = per_head(m) + jnp.log(l)
        _shift_rows(cache_ref, out_cache_ref, e, jnp.concatenate([_row_to_col(kn), _row_to_col(vn)], axis=0))


def _sample_dilated_kernel(q_ref, kn_ref, vn_ref, cache_ref, out_cache_ref, o_ref, lse_ref, *, dil, bb):
    _sample_dilated_step(q_ref, kn_ref, vn_ref, cache_ref, out_cache_ref, o_ref, lse_ref, pl.program_id(0),
                         dil=dil, bb=bb)


class _ShiftJob(NamedTuple):
    q: jax.Array
    k_new: jax.Array
    v_new: jax.Array
    cache: jax.Array
    first: int
    per_step: int
    dil: int
    prev_out: Optional[jax.Array]

    @property
    def static(self):
        return (4 if self.prev_out is None else 5, self.dil, self.per_step)


def _job_operands(job, step_of, inputs_before, outputs_before):
    count, wq = job.q.shape
    _, chans, length = job.cache.shape
    first_blk = job.first // job.per_step
    full = pl.BlockSpec((count, wq), lambda *ids: (0, 0))
    tile = pl.BlockSpec((job.per_step, chans, length), lambda *ids: (first_blk + step_of(*ids), 0, 0))
    in_specs, args, aliases = [full, full, full, tile], [job.q, job.k_new, job.v_new, job.cache], {}
    if job.prev_out is not None:
        in_specs.append(pl.BlockSpec(memory_space=pl.ANY))
        args.append(job.prev_out)
        aliases[inputs_before + 4] = outputs_before
    out_shape = [jax.ShapeDtypeStruct(job.cache.shape, F32)] + [jax.ShapeDtypeStruct((count, wq), F32)] * 2
    return in_specs, args, [tile, full, full], out_shape, aliases


def _split_job_refs(rest, job_static, n_main_out):
    n_in = job_static[0] if job_static else 0
    n_out = 3 if job_static else 0
    job_in, rest = rest[:n_in], rest[n_in:]
    return job_in, rest[:n_main_out], rest[n_main_out:n_main_out + n_out], rest[n_main_out + n_out:]


def _run_job(job_static, in_refs, out_refs, step):
    _, dil, per_step = job_static
    _sample_dilated_step(*in_refs[:4], *out_refs, step, dil=dil, bb=per_step)


def _sample_swa_kernel(q_ref, kn_ref, vn_ref, cache_ref, sink_ref, out_cache_ref, o_ref, qexp, oexp, *, bb):
    step = pl.program_id(0)
    nb, length = q_ref.shape[0], cache_ref.shape[2]
    low = lax.broadcasted_iota(jnp.int32, (1, LANES), 1) < HEAD_DIM

    @pl.when(step == 0)
    def _():
        for hq in range(B_Q_HEADS):
            slab = q_ref[:, (hq // 2) * LANES:(hq // 2 + 1) * LANES]
            src_low, dst_low = hq % 2 == 0, hq // B_GROUP == 0
            x = slab if src_low == dst_low else pltpu.roll(slab, HEAD_DIM, 1)
            qexp[hq * nb:(hq + 1) * nb, :] = jnp.where(low if dst_low else jnp.logical_not(low), x, 0.0)

    own_half = (lax.broadcasted_iota(jnp.int32, (B_Q_HEADS, LANES), 0) // B_GROUP
                == lax.broadcasted_iota(jnp.int32, (B_Q_HEADS, LANES), 1) // HEAD_DIM)
    valid = lax.broadcasted_iota(jnp.int32, (1, length), 1) >= 1
    sink = sink_ref[:, 0:1]
    for e in range(bb):
        b = step * bb + e
        heads = pl.ds(b, B_Q_HEADS, stride=nb)
        kn, vn = kn_ref[pl.ds(b, 1), :], vn_ref[pl.ds(b, 1), :]
        qb = qexp[heads, :]
        kt = cache_ref[e, 0:B_KV_WIDTH, :].astype(BF16)
        vt = cache_ref[e, B_KV_WIDTH:2 * B_KV_WIDTH, :].astype(BF16)
        s = jnp.where(valid, _dot(qb.astype(BF16), kt), NEG)
        s_new = jnp.sum(qb * kn, axis=1, keepdims=True)
        m = jnp.maximum(jnp.maximum(jnp.max(s, axis=1, keepdims=True), s_new), sink)
        p = jnp.exp(s - m)
        p_new = jnp.exp(s_new - m)
        l = jnp.sum(p, axis=1, keepdims=True) + p_new + jnp.exp(sink - m)
        o = lax.dot_general(p.astype(BF16), vt, _NT, preferred_element_type=F32)
        oexp[heads, :] = jnp.where(own_half, o + p_new * vn, 0.0) * (1.0 / l)
        _shift_rows(cache_ref, out_cache_ref, e, jnp.concatenate([_row_to_col(kn), _row_to_col(vn)], axis=0))

    @pl.when(step == pl.num_programs(0) - 1)
    def _():
        for j in range(B_Q_HEADS // 2):
            even, odd = oexp[2 * j * nb:(2 * j + 1) * nb, :], oexp[(2 * j + 1) * nb:(2 * j + 2) * nb, :]
            if 2 * j // B_GROUP == 0:
                odd = pltpu.roll(odd, HEAD_DIM, 1)
            else:
                even = pltpu.roll(even, HEAD_DIM, 1)
            o_ref[:, j * LANES:(j + 1) * LANES] = jnp.where(low, even, odd)


def _sample_attn(q, k_new, v_new, cache_t, *, dil=1, sinks=None):
    nb, chans, length = cache_t.shape
    wq, wkv = q.shape[1], k_new.shape[1]
    bb = max(1, min(8, 2048 // length))
    full = lambda w: pl.BlockSpec((nb, w), lambda s: (0, 0))
    tile = pl.BlockSpec((bb, chans, length), lambda s: (s, 0, 0))
    in_specs = [full(wq), full(wkv), full(wkv), tile]
    args = [q, k_new, v_new, cache_t]
    out_specs = [tile, full(wq)]
    out_shape = [jax.ShapeDtypeStruct(cache_t.shape, F32), jax.ShapeDtypeStruct((nb, wq), F32)]
    if sinks is None:
        kern = functools.partial(_sample_dilated_kernel, dil=dil, bb=bb)
        out_specs.append(full(wq))
        out_shape.append(jax.ShapeDtypeStruct((nb, wq), F32))
        scratch = []
    else:
        kern = functools.partial(_sample_swa_kernel, bb=bb)
        in_specs.append(pl.BlockSpec(sinks.shape, lambda s: (0, 0)))
        args.append(sinks)
        scratch = [pltpu.VMEM((B_Q_HEADS * nb, LANES), F32)] * 2
    return pl.pallas_call(
        kern, grid=(nb // bb,), in_specs=in_specs, out_specs=out_specs, out_shape=out_shape,
        scratch_shapes=scratch, compiler_params=_ARB(1),
        name=f"sample_attn_d{dil}_l{length}" if sinks is None else "sample_swa",
    )(*args)


def _rope_tables(pos):
    half = HEAD_DIM // 2
    inv = ROPE_THETA ** (-jnp.arange(half, dtype=F32) / half)
    ang = pos.astype(F32)[:, None] * inv[None, :]
    cos, sin = jnp.cos(ang), jnp.sin(ang)
    return jnp.tile(cos, (1, 4)), jnp.tile(jnp.concatenate([-sin, sin], axis=1), (1, 2))


def _to_tiles(cache):
    nb, length = cache.shape[0], cache.shape[1]
    return jnp.transpose(cache, (0, 2, 3, 4, 1)).reshape(nb, -1, length)


def _from_tiles(tiles, heads):
    nb, _, length = tiles.shape
    return jnp.transpose(tiles.reshape(nb, 2, heads, HEAD_DIM, length), (0, 4, 1, 2, 3))


def kernel(x_prompt, x_sample, c_prompt, c_sample, cache_a_kv_g0, cache_a_kv_g1, cache_a_kv_g2, cache_b_kv, ada_w,
           ada_b, g_pre, g_post, w_in_a, w_o_a, w_in_b, w_o_b, sinks_b, ada_kv_w, ada_kv_b, g_kv, w_kv):
    nbp, t, _ = x_prompt.shape
    nbs = x_sample.shape[0]
    assert x_sample.shape[1] == 1

    c_all = jnp.concatenate([c_prompt, c_sample], axis=0)
    mod = _ada(c_all, ada_w, ada_b)
    mod_kv = _ada(c_all, ada_kv_w[None], ada_kv_b[None])[0]

    def split(a, parts, sample):
        rows = a[nbp:][None] if sample else a[:nbp][:, None]
        return [rows[..., p * D_MODEL:(p + 1) * D_MODEL] for p in range(parts)]

    w_in_a_b, w_o_a_b = w_in_a[0].astype(BF16), w_o_a[0].astype(BF16)
    w_in_b_b, w_o_b_b, w_kv_b = w_in_b[0].astype(BF16), w_o_b[0].astype(BF16), w_kv.astype(BF16)
    g_pre0, g_pre1 = g_pre[0:1], g_pre[1:2]
    g_post0, g_post1 = g_post[0:1], g_post[1:2]
    g_kv_r = g_kv[None]
    sinks = jnp.broadcast_to(sinks_b[0][:, None], (B_Q_HEADS, LANES))

    xs = x_sample.reshape(1, nbs, D_MODEL)
    cos_s, sin_s = _rope_tables(jnp.full((nbs,), PAST_LEN, jnp.int32))
    sh0s, sc0s, gt0s = split(mod[0], 3, True)
    sh1s, sc1s, gt1s = split(mod[1], 3, True)
    kshs, kscs = split(mod_kv, 2, True)
    qs, ks, vs, gzs = _inproj_a(xs, sh0s, sc0s, g_pre0, w_in_a_b, cos_s, sin_s, sample=True)
    tiles = [_to_tiles(c[0]) for c in (cache_a_kv_g0, cache_a_kv_g1, cache_a_kv_g2)]

    def job(g, first, count, per_step, prev_out=None):
        rows, cols = slice(first, first + count), slice(g * GROUP_WIDTH, (g + 1) * GROUP_WIDTH)
        return _ShiftJob(qs[0, rows, cols], ks[0, rows, cols], vs[0, rows, cols], tiles[g], first, per_step,
                         A_GROUPS[g][1], prev_out)

    proj_steps = nbp * (t // min(TOKEN_TILE, t))
    swa_steps = nbp * (t // min(SWA_QUERY_TILE, t))

    cos_p, sin_p = _rope_tables(jnp.arange(t))
    sh0, sc0, gt0 = split(mod[0], 3, False)
    sh1, sc1, gt1 = split(mod[1], 3, False)
    ksh, ksc = split(mod_kv, 2, False)
    *qkv, gz, ca0, ca1, ca2, shifted2, o2a, lse2a = _inproj_a(
        x_prompt, sh0, sc0, g_pre0, w_in_a_b, cos_p, sin_p, sample=False, job=job(2, 0, proj_steps, 1))
    os, lses = zip(*[_dilated_attn(*qkv[3 * g:3 * g + 3], g) for g in range(len(A_GROUPS))])
    x1 = _outproj(os, lses, gz, x_prompt, gt0, g_post0, w_o_a_b)
    qb, gzb, kd, vd, cb, shifted2, o2b, lse2b = _inproj_b(
        x1, sh1, sc1, ksh, ksc, g_pre1, g_kv_r, w_in_b_b, w_kv_b, cos_p, sin_p, sample=False,
        job=job(2, proj_steps, nbs - proj_steps, 1, prev_out=shifted2))
    ob, shifted1, o1, lse1 = _swa_attn(qb, kd, vd, sinks, job=job(1, 0, nbs, nbs // swa_steps))
    y_prompt = _outproj([ob], None, gzb, x1, gt1, g_post1, w_o_b_b)
    new_a_prompt = [_from_tiles(c, A_HEADS_PER_GROUP)[None] for c in (ca0, ca1, ca2)]
    new_b_prompt = _from_tiles(cb, B_KV_HEADS)

    shifted0, o0, lse0 = _sample_attn(qs[0, :, 0:GROUP_WIDTH], ks[0, :, 0:GROUP_WIDTH], vs[0, :, 0:GROUP_WIDTH],
                                      tiles[0], dil=A_GROUPS[0][1])
    os = [o0, o1, jnp.concatenate([o2a, o2b], axis=0)]
    lses = [lse0, lse1, jnp.concatenate([lse2a, lse2b], axis=0)]
    os = [o.astype(BF16)[None, None] for o in os]
    lses = [l[None, None] for l in lses]
    new_a_sample = [_from_tiles(s, A_HEADS_PER_GROUP)[None] for s in (shifted0, shifted1, shifted2)]
    sh1, sc1, gt1, ksh, ksc, gz, gt0 = sh1s, sc1s, gt1s, kshs, kscs, gzs, gt0s
    xs1 = _outproj(os, lses, gz, xs, gt0, g_post0, w_o_a_b)
    qb, gzb, kb, vb = _inproj_b(xs1, sh1, sc1, ksh, ksc, g_pre1, g_kv_r, w_in_b_b, w_kv_b, cos_s, sin_s,
                                sample=True)
    shifted_b, ob = _sample_attn(qb[0], kb[0], vb[0], _to_tiles(cache_b_kv), sinks=sinks)
    y_sample = _outproj([ob.astype(BF16)[None]], None, gzb, xs1, gt1, g_post1, w_o_b_b).reshape(nbs, 1, D_MODEL)
    new_b_sample = _from_tiles(shifted_b, B_KV_HEADS)

    return (y_prompt, y_sample, *new_a_prompt, new_b_prompt, *new_a_sample, new_b_sample)
```

```python
import functools
from typing import NamedTuple, Optional

import jax
import jax.numpy as jnp
from jax import lax
from jax.experimental import pallas as pl
from jax.experimental.pallas import tpu as pltpu

D_MODEL = 1024
HEAD_DIM = 64
A_GROUPS = ((128, 1), (512, 4), (2048, 16))
A_HEADS_PER_GROUP = 4
GROUP_WIDTH = A_HEADS_PER_GROUP * HEAD_DIM
A_WIDTH = GROUP_WIDTH * len(A_GROUPS)
B_Q_HEADS = 16
B_KV_HEADS = 2
B_GROUP = B_Q_HEADS // B_KV_HEADS
B_WIDTH = B_Q_HEADS * HEAD_DIM
B_KV_WIDTH = B_KV_HEADS * HEAD_DIM
B_WINDOW = 128
BLOCK = 128
PAST_LEN = 16384
ROPE_THETA = 10000.0
EPS = 1e-6
NEG = -1e30
LANES = 128
Q_SCALE = HEAD_DIM ** -0.5

BF16 = jnp.bfloat16
F32 = jnp.float32
TOKEN_TILE = 512

_ARB = lambda n: pltpu.CompilerParams(dimension_semantics=("arbitrary",) * n)


def _rope(x, cos, sin_signed):
    lane = lax.broadcasted_iota(jnp.int32, (1, LANES), 1)
    first_half = (lane % HEAD_DIM) < (HEAD_DIM // 2)
    outs = []
    for j in range(x.shape[1] // LANES):
        xs = x[:, j * LANES:(j + 1) * LANES]
        partner = jnp.where(first_half, pltpu.roll(xs, LANES - 32, 1), pltpu.roll(xs, 32, 1))
        outs.append(xs * cos + partner * sin_signed)
    return outs[0] if len(outs) == 1 else jnp.concatenate(outs, axis=1)


def _normalize(x):
    return x * lax.rsqrt(jnp.mean(x * x, axis=-1, keepdims=True) + EPS)


def _dot(a, b):
    return jnp.dot(a, b, preferred_element_type=F32)


def _ada_kernel(c_ref, w_ref, b_ref, o_ref):
    s = jax.nn.silu(c_ref[...]).astype(BF16)
    o_ref[...] = _dot(s, w_ref[...].astype(BF16)) + b_ref[...]


def _ada(c, w, b):
    nl, _, n = w.shape
    m = c.shape[0]
    tn = 1024
    return pl.pallas_call(
        _ada_kernel,
        grid=(nl, n // tn),
        in_specs=[
            pl.BlockSpec((m, D_MODEL), lambda l, j: (0, 0)),
            pl.BlockSpec((None, D_MODEL, tn), lambda l, j: (l, 0, j)),
            pl.BlockSpec((None, 1, tn), lambda l, j: (l, 0, j)),
        ],
        out_specs=pl.BlockSpec((None, m, tn), lambda l, j: (l, 0, j)),
        out_shape=jax.ShapeDtypeStruct((nl, m, n), F32),
        compiler_params=_ARB(2),
        name="ada",
    )(c, w, b.reshape(nl, 1, n))


ROW_CHUNK = 512


def _row_chunks(tm):
    size = min(ROW_CHUNK, tm)
    return [slice(c * size, (c + 1) * size) for c in range(tm // size)]


def _mod_rows(ref, rows):
    return ref[0] if ref.shape[1] == 1 else ref[0, rows, :]


def _inproj_a_body(x_ref, sh_ref, sc_ref, g_ref, w_ref, cos_ref, sin_ref, rows):
    h = _normalize(x_ref[0, rows, :]) * g_ref[...]
    hb = (h * (1.0 + _mod_rows(sc_ref, rows)) + _mod_rows(sh_ref, rows)).astype(BF16)
    cos, sin = cos_ref[rows, :], sin_ref[rows, :]
    q = _rope(_dot(hb, w_ref[:, 0:A_WIDTH]), cos, sin) * Q_SCALE
    k = _rope(_dot(hb, w_ref[:, A_WIDTH:2 * A_WIDTH]), cos, sin)
    v = _dot(hb, w_ref[:, 2 * A_WIDTH:3 * A_WIDTH])
    gz = jax.nn.silu(_dot(hb, w_ref[:, 3 * A_WIDTH:4 * A_WIDTH]))
    return q, k, v, gz


def _inproj_a_prompt_kernel(x_ref, sh_ref, sc_ref, g_ref, w_ref, cos_ref, sin_ref, *rest, tm, nt, job):
    job_in, main_out, job_out, (scr,) = _split_job_refs(rest, job, 13)
    qkv_refs, (gz_ref, c0_ref, c1_ref, c2_ref) = main_out[:9], main_out[9:]
    i = pl.program_id(1)
    if job:
        _run_job(job, job_in, job_out, pl.program_id(0) * nt + i)
    slabs_per_group = GROUP_WIDTH // LANES
    for c, rows in enumerate(_row_chunks(tm)):
        n = rows.stop - rows.start
        q, k, v, gz = _inproj_a_body(x_ref, sh_ref, sc_ref, g_ref, w_ref, cos_ref, sin_ref, rows)
        gz_ref[0, rows, :] = gz.astype(BF16)
        for which, val in enumerate((q, k, v)):
            for j in range(A_WIDTH // LANES):
                scr[c, which, j] = val[:, j * LANES:(j + 1) * LANES]
            for g, (_, dil) in enumerate(A_GROUPS):
                out_ref = qkv_refs[3 * g + which]
                dst = slice(rows.start // dil, rows.stop // dil)
                for r in range(dil):
                    for h in range(slabs_per_group):
                        part = scr[c, which, g * slabs_per_group + h, pl.ds(r, n // dil, stride=dil), :]
                        out_ref[0, r, dst, h * LANES:(h + 1) * LANES] = part.astype(BF16)
        for g, c_ref in enumerate((c0_ref, c1_ref, c2_ref)):
            win = A_GROUPS[g][0]
            width = min(tm, win)
            first_tile = nt - max(win // tm, 1)
            lo = max(rows.start, tm - width)
            if lo >= rows.stop:
                continue

            @pl.when(i >= first_tile)
            def _(g=g, c_ref=c_ref, lo=lo, k=k, v=v, rows=rows, width=width):
                cols = slice(g * GROUP_WIDTH, (g + 1) * GROUP_WIDTH)
                dst = slice(lo - (tm - width), rows.stop - (tm - width))
                c_ref[0, 0:GROUP_WIDTH, dst] = k[lo - rows.start:, cols].T
                c_ref[0, GROUP_WIDTH:2 * GROUP_WIDTH, dst] = v[lo - rows.start:, cols].T


def _inproj_a_sample_kernel(x_ref, sh_ref, sc_ref, g_ref, w_ref, cos_ref, sin_ref,
                            q_ref, k_ref, v_ref, gz_ref):
    q, k, v, gz = _inproj_a_body(x_ref, sh_ref, sc_ref, g_ref, w_ref, cos_ref, sin_ref, slice(0, x_ref.shape[1]))
    q_ref[0] = q
    k_ref[0] = k
    v_ref[0] = v
    gz_ref[0] = gz.astype(BF16)


def _mod_spec(mod, tm):
    if mod.shape[1] == 1:
        return pl.BlockSpec((1, 1, D_MODEL), lambda b, i: (b, 0, 0))
    return pl.BlockSpec((1, tm, D_MODEL), lambda b, i: (b, i, 0))


def _inproj_a(x, shift, scale, g, w, cos, sin, *, sample, job=None):
    nb, t, _ = x.shape
    tm = min(TOKEN_TILE, t)
    nt = t // tm
    row = lambda width: pl.BlockSpec((1, tm, width), lambda b, i: (b, i, 0))
    in_specs = [
        row(D_MODEL), _mod_spec(shift, tm), _mod_spec(scale, tm),
        pl.BlockSpec((1, D_MODEL), lambda b, i: (0, 0)),
        pl.BlockSpec((D_MODEL, 4 * A_WIDTH), lambda b, i: (0, 0)),
        pl.BlockSpec((tm, LANES), lambda b, i: (i, 0)),
        pl.BlockSpec((tm, LANES), lambda b, i: (i, 0)),
    ]
    if sample:
        out_shape = [jax.ShapeDtypeStruct((nb, t, A_WIDTH), dt) for dt in (F32, F32, F32, BF16)]
        return pl.pallas_call(
            _inproj_a_sample_kernel, grid=(nb, nt), in_specs=in_specs,
            out_specs=[row(A_WIDTH)] * 4, out_shape=out_shape,
            compiler_params=_ARB(2), name="inproj_a_sample",
        )(x, shift, scale, g, w, cos, sin)
    out_shape, out_specs = [], []
    for _, dil in A_GROUPS:
        out_shape += [jax.ShapeDtypeStruct((nb, dil, t // dil, GROUP_WIDTH), BF16)] * 3
        out_specs += [pl.BlockSpec((1, dil, tm // dil, GROUP_WIDTH), lambda b, i: (b, 0, i, 0))] * 3
    out_shape.append(jax.ShapeDtypeStruct((nb, t, A_WIDTH), BF16))
    out_specs.append(row(A_WIDTH))
    for win, _ in A_GROUPS:
        rows = min(tm, win)
        first_tile = nt - max(win // tm, 1)
        out_shape.append(jax.ShapeDtypeStruct((nb, 2 * GROUP_WIDTH, min(win, t)), F32))
        out_specs.append(pl.BlockSpec(
            (1, 2 * GROUP_WIDTH, rows),
            lambda b, i, first_tile=first_tile: (b, 0, jnp.maximum(i - first_tile, 0))))
    args, aliases = [x, shift, scale, g, w, cos, sin], {}
    if job is not None:
        assert job.q.shape[0] == nb * nt * job.per_step
        j_in, j_args, j_out, j_shape, aliases = _job_operands(job, lambda b, i: b * nt + i, len(args),
                                                              len(out_specs))
        in_specs, args = in_specs + j_in, args + j_args
        out_specs, out_shape = out_specs + j_out, out_shape + j_shape
    return pl.pallas_call(
        functools.partial(_inproj_a_prompt_kernel, tm=tm, nt=nt, job=job.static if job else None),
        grid=(nb, nt), in_specs=in_specs, out_specs=out_specs, out_shape=out_shape,
        scratch_shapes=[pltpu.VMEM((tm // min(ROW_CHUNK, tm), 3, A_WIDTH // LANES, min(ROW_CHUNK, tm), LANES), F32)],
        input_output_aliases=aliases, compiler_params=_ARB(2), name="inproj_a_prompt",
    )(*args)


def _query_minus_key():
    kj = lax.broadcasted_iota(jnp.int32, (2 * BLOCK, BLOCK), 0)
    qi = lax.broadcasted_iota(jnp.int32, (2 * BLOCK, BLOCK), 1)
    return qi - kj


def _head_attend(st, mask, vt_ext, sink):
    st = jnp.where(mask, st, NEG)
    m = jnp.max(st, axis=0, keepdims=True)
    if sink is not None:
        m = jnp.maximum(m, sink)
    ext = _dot(vt_ext, jnp.exp(st - m).astype(BF16))
    l = ext[HEAD_DIM:HEAD_DIM + 1, :]
    if sink is not None:
        l = l + jnp.exp(sink - m)
    return ext[0:HEAD_DIM, :] * (1.0 / l), m, l


def _values_ext(vt_prev, vt_cur, head):
    rows = slice(head * HEAD_DIM, (head + 1) * HEAD_DIM)
    ones = jnp.ones((16, 2 * BLOCK), BF16)
    return jnp.concatenate([jnp.concatenate([vt_prev[rows], vt_cur[rows]], axis=1), ones], axis=0)


def _block_diag_queries(qt, heads, kv_of_head, kv_heads):
    zeros = jnp.zeros((HEAD_DIM, BLOCK), BF16)
    cols = []
    for h in range(heads):
        pieces = [zeros] * kv_heads
        pieces[kv_of_head(h)] = qt[h * HEAD_DIM:(h + 1) * HEAD_DIM]
        cols.append(jnp.concatenate(pieces, axis=0))
    return jnp.concatenate(cols, axis=1)


def _dilated_attn_kernel(q_ref, k_ref, v_ref, *rest, job):
    job_in, (o_ref, lse_ref), job_out, (vt_ref,) = _split_job_refs(rest, job, 2)
    if job:
        _run_job(job, job_in, job_out, pl.program_id(0) * pl.num_programs(1) + pl.program_id(1))
    nres, m_rows = q_ref.shape[1], q_ref.shape[2]
    nblk = m_rows // BLOCK
    base = _query_minus_key()
    heads = A_HEADS_PER_GROUP
    for r in range(nres):
        def transpose_values(j, c, r=r):
            rows = pl.ds(pl.multiple_of(j * BLOCK, BLOCK), BLOCK)
            vt_ref[j] = v_ref[0, r, rows, :].T
            return c

        lax.fori_loop(0, nblk, transpose_values, 0)

        def block(i, c, r=r):
            prev = jnp.maximum(i - 1, 0)
            dist = base + (i - prev) * BLOCK
            mask = (dist >= 0) & (dist <= BLOCK)
            qrows = pl.ds(pl.multiple_of(i * BLOCK, BLOCK), BLOCK)
            k = k_ref[0, r, pl.ds(pl.multiple_of(prev * BLOCK, BLOCK), 2 * BLOCK), :]
            qd = _block_diag_queries(q_ref[0, r, qrows, :].T, heads, lambda h: h, heads)
            st_all = _dot(k, qd)
            vt_prev, vt_cur = vt_ref[prev], vt_ref[i]
            outs, lses = [], []
            for h in range(heads):
                o_t, m, l = _head_attend(st_all[:, h * BLOCK:(h + 1) * BLOCK], mask,
                                         _values_ext(vt_prev, vt_cur, h), None)
                outs.append(o_t)
                lses.append(jnp.broadcast_to(m + jnp.log(l), (HEAD_DIM, BLOCK)))
            o_ref[0, r, qrows, :] = jnp.concatenate(outs, axis=0).T.astype(BF16)
            lse_ref[0, r, qrows, :] = jnp.concatenate(lses, axis=0).T
            return c

        lax.fori_loop(0, nblk, block, 0, unroll=2)


def _dilated_steps(dil):
    nres = max(1, dil // 4)
    return nres, dil // nres


def _dilated_attn(q, k, v, g, job=None):
    nb, dil, m_rows, _ = q.shape
    nres, nsteps = _dilated_steps(dil)
    blk = pl.BlockSpec((1, nres, m_rows, GROUP_WIDTH), lambda b, r: (b, r, 0, 0))
    in_specs, args, aliases = [blk, blk, blk], [q, k, v], {}
    out_specs = [blk, blk]
    out_shape = [jax.ShapeDtypeStruct(q.shape, BF16), jax.ShapeDtypeStruct(q.shape, F32)]
    if job is not None:
        assert job.q.shape[0] == nb * nsteps * job.per_step
        j_in, j_args, j_out, j_shape, aliases = _job_operands(job, lambda b, r: b * nsteps + r, len(args), 2)
        in_specs, args = in_specs + j_in, args + j_args
        out_specs, out_shape = out_specs + j_out, out_shape + j_shape
    return pl.pallas_call(
        functools.partial(_dilated_attn_kernel, job=job.static if job else None),
        grid=(nb, nsteps), in_specs=in_specs, out_specs=out_specs, out_shape=out_shape,
        scratch_shapes=[pltpu.VMEM((m_rows // BLOCK, GROUP_WIDTH, BLOCK), BF16)],
        input_output_aliases=aliases, compiler_params=_ARB(2), name=f"dilated_attn_g{g}",
    )(*args)


SWA_QUERY_TILE = 512


def _swa_attn_kernel(q_ref, k_ref, v_ref, sink_ref, *rest, job):
    job_in, (o_ref,), job_out, (vt_ref,) = _split_job_refs(rest, job, 1)
    j = pl.program_id(1)
    if job:
        _run_job(job, job_in, job_out, pl.program_id(0) * pl.num_programs(1) + j)
    seq = k_ref.shape[1]
    nq = q_ref.shape[1] // BLOCK
    base = _query_minus_key()

    @pl.when(j == 0)
    def _():
        def transpose_values(t, c):
            vt_ref[t] = v_ref[0, pl.ds(pl.multiple_of(t * BLOCK, BLOCK), BLOCK), :].T
            return c

        lax.fori_loop(0, seq // BLOCK, transpose_values, 0)

    def block(i, c):
        cur = j * nq + i
        prev = jnp.maximum(cur - 1, 0)
        dist = base + (cur - prev) * BLOCK
        mask = (dist >= 0) & (dist < B_WINDOW)
        qrows = pl.ds(pl.multiple_of(i * BLOCK, BLOCK), BLOCK)
        k = k_ref[0, pl.ds(pl.multiple_of(prev * BLOCK, BLOCK), 2 * BLOCK), :]
        qt = q_ref[0, qrows, :].T
        vt_prev, vt_cur = vt_ref[prev], vt_ref[cur]
        outs = []
        for kvh in range(B_KV_HEADS):
            qd = _block_diag_queries(qt[kvh * B_GROUP * HEAD_DIM:(kvh + 1) * B_GROUP * HEAD_DIM], B_GROUP,
                                     lambda h, kvh=kvh: kvh, B_KV_HEADS)
            st_all = _dot(k, qd)
            vt_ext = _values_ext(vt_prev, vt_cur, kvh)
            for gq in range(B_GROUP):
                hq = kvh * B_GROUP + gq
                o_t, _, _ = _head_attend(st_all[:, gq * BLOCK:(gq + 1) * BLOCK], mask, vt_ext,
                                         sink_ref[hq:hq + 1, :])
                outs.append(o_t)
        o_ref[0, qrows, :] = jnp.concatenate(outs, axis=0).T.astype(BF16)
        return c

    lax.fori_loop(0, nq, block, 0)


def _swa_attn(q, k, v, sinks, job=None):
    nb, t, _ = q.shape
    tq = min(SWA_QUERY_TILE, t)
    nj = t // tq
    qblk = pl.BlockSpec((1, tq, B_WIDTH), lambda b, j: (b, j, 0))
    kvblk = pl.BlockSpec((1, t, B_KV_WIDTH), lambda b, j: (b, 0, 0))
    in_specs = [qblk, kvblk, kvblk, pl.BlockSpec((B_Q_HEADS, LANES), lambda b, j: (0, 0))]
    args, aliases = [q, k, v, sinks], {}
    out_specs, out_shape = [qblk], [jax.ShapeDtypeStruct((nb, t, B_WIDTH), BF16)]
    if job is not None:
        assert job.q.shape[0] == nb * nj * job.per_step
        j_in, j_args, j_out, j_shape, aliases = _job_operands(job, lambda b, j: b * nj + j, len(args), 1)
        in_specs, args = in_specs + j_in, args + j_args
        out_specs, out_shape = out_specs + j_out, out_shape + j_shape
    return pl.pallas_call(
        functools.partial(_swa_attn_kernel, job=job.static if job else None),
        grid=(nb, nj), in_specs=in_specs, out_specs=out_specs, out_shape=out_shape,
        scratch_shapes=[pltpu.VMEM((t // BLOCK, B_KV_WIDTH, BLOCK), BF16)],
        input_output_aliases=aliases, compiler_params=_ARB(2), name="swa_attn",
    )(*args)


def _finish(a_parts, w_ref, x_ref, gate_ref, g_ref, y_ref):
    acc = None
    row = 0
    for a in a_parts:
        part = _dot(a, w_ref[row:row + a.shape[1], :])
        acc = part if acc is None else acc + part
        row += a.shape[1]
    y_ref[0] = x_ref[0] + gate_ref[0] * (_normalize(acc) * g_ref[...])


def _in_row_order(ref, scr):
    dil = ref.shape[1]
    if dil == 1:
        return ref[0, 0].astype(F32)
    nslab = ref.shape[3] // LANES
    for r in range(dil):
        for h in range(nslab):
            scr[h, pl.ds(r, ref.shape[2], stride=dil), :] = ref[0, r, :, h * LANES:(h + 1) * LANES].astype(F32)
    return jnp.concatenate([scr[h] for h in range(nslab)], axis=1)


def _outproj_mix_kernel(o0_ref, o1_ref, o2_ref, l0_ref, l1_ref, l2_ref, gz_ref, x_ref, gate_ref, g_ref, w_ref,
                        y_ref, *scratch):
    os = [_in_row_order(r, s) for r, s in zip((o0_ref, o1_ref, o2_ref), scratch[0:3])]
    lses = [_in_row_order(r, s) for r, s in zip((l0_ref, l1_ref, l2_ref), scratch[3:6])]
    top = jnp.maximum(jnp.maximum(lses[0], lses[1]), lses[2])
    es = [jnp.exp(l - top) for l in lses]
    inv = 1.0 / (es[0] + es[1] + es[2])
    parts = []
    for g in range(len(A_GROUPS)):
        gz = gz_ref[0, :, g * GROUP_WIDTH:(g + 1) * GROUP_WIDTH].astype(F32)
        parts.append((os[g] * (es[g] * inv) * gz).astype(BF16))
    _finish(parts, w_ref, x_ref, gate_ref, g_ref, y_ref)


def _outproj_kernel(o_ref, gz_ref, x_ref, gate_ref, g_ref, w_ref, y_ref):
    a = (o_ref[0].astype(F32) * gz_ref[0].astype(F32)).astype(BF16)
    _finish([a], w_ref, x_ref, gate_ref, g_ref, y_ref)


def _outproj(os, lses, gz, x, gate, g, w):
    nb, t, _ = x.shape
    tm = min(TOKEN_TILE, t)
    row = lambda width: pl.BlockSpec((1, tm, width), lambda b, i: (b, i, 0))
    width = w.shape[0]
    tail_specs = [row(width), row(D_MODEL), _mod_spec(gate, tm),
                  pl.BlockSpec((1, D_MODEL), lambda b, i: (0, 0)),
                  pl.BlockSpec((width, D_MODEL), lambda b, i: (0, 0))]
    scratch = []
    if lses is None:
        kern, name = _outproj_kernel, "outproj_b"
        in_specs = [row(width)] + tail_specs
        args = (os[0], gz, x, gate, g, w)
    else:
        kern, name = _outproj_mix_kernel, "outproj_a"
        split = lambda a: pl.BlockSpec((1, a.shape[1], tm // a.shape[1], GROUP_WIDTH), lambda b, i: (b, 0, i, 0))
        in_specs = [split(a) for a in (*os, *lses)] + tail_specs
        args = (*os, *lses, gz, x, gate, g, w)
        scratch = [pltpu.VMEM((GROUP_WIDTH // LANES, tm, LANES), F32)] * 6
    return pl.pallas_call(
        kern, grid=(nb, t // tm), in_specs=in_specs, out_specs=row(D_MODEL),
        out_shape=jax.ShapeDtypeStruct((nb, t, D_MODEL), F32), scratch_shapes=scratch,
        compiler_params=_ARB(2), name=name,
    )(*args)


def _inproj_b_body(x_ref, sh_ref, sc_ref, ksh_ref, ksc_ref, g_ref, gkv_ref, w_ref, wkv_ref, cos_ref, sin_ref, rows):
    xn = _normalize(x_ref[0, rows, :])
    hb = ((xn * g_ref[...]) * (1.0 + _mod_rows(sc_ref, rows)) + _mod_rows(sh_ref, rows)).astype(BF16)
    hk = ((xn * gkv_ref[...]) * (1.0 + _mod_rows(ksc_ref, rows)) + _mod_rows(ksh_ref, rows)).astype(BF16)
    cos, sin = cos_ref[rows, :], sin_ref[rows, :]
    q = _rope(_dot(hb, w_ref[:, 0:B_WIDTH]), cos, sin) * Q_SCALE
    gz = jax.nn.silu(_dot(hb, w_ref[:, B_WIDTH:2 * B_WIDTH]))
    kv = _dot(hk, wkv_ref[...])
    k = _rope(kv[:, 0:B_KV_WIDTH], cos, sin)
    v = kv[:, B_KV_WIDTH:2 * B_KV_WIDTH]
    return q, gz, k, v


def _inproj_b_prompt_kernel(x_ref, sh_ref, sc_ref, ksh_ref, ksc_ref, g_ref, gkv_ref, w_ref, wkv_ref, cos_ref,
                            sin_ref, *rest, tm, nt, job):
    job_in, (q_ref, gz_ref, kd_ref, vd_ref, c_ref), job_out, _ = _split_job_refs(rest, job, 5)
    if job:
        _run_job(job, job_in, job_out, pl.program_id(0) * nt + pl.program_id(1))
    for rows in _row_chunks(tm):
        q, gz, k, v = _inproj_b_body(x_ref, sh_ref, sc_ref, ksh_ref, ksc_ref, g_ref, gkv_ref, w_ref, wkv_ref,
                                     cos_ref, sin_ref, rows)
        q_ref[0, rows, :] = q.astype(BF16)
        gz_ref[0, rows, :] = gz.astype(BF16)
        kd_ref[0, rows, :] = k.astype(BF16)
        vd_ref[0, rows, :] = v.astype(BF16)
        if rows.stop == tm:
            @pl.when(pl.program_id(1) == nt - 1)
            def _(k=k, v=v, n=rows.stop - rows.start):
                c_ref[0, 0:B_KV_WIDTH, :] = k[n - B_WINDOW:, :].T
                c_ref[0, B_KV_WIDTH:2 * B_KV_WIDTH, :] = v[n - B_WINDOW:, :].T


def _inproj_b_sample_kernel(x_ref, sh_ref, sc_ref, ksh_ref, ksc_ref, g_ref, gkv_ref, w_ref, wkv_ref, cos_ref,
                            sin_ref, q_ref, gz_ref, k_ref, v_ref):
    q, gz, k, v = _inproj_b_body(x_ref, sh_ref, sc_ref, ksh_ref, ksc_ref, g_ref, gkv_ref, w_ref, wkv_ref, cos_ref,
                                 sin_ref, slice(0, x_ref.shape[1]))
    q_ref[0] = q
    gz_ref[0] = gz.astype(BF16)
    k_ref[0] = k
    v_ref[0] = v


def _inproj_b(x, shift, scale, kshift, kscale, g, gkv, w, wkv, cos, sin, *, sample, job=None):
    nb, t, _ = x.shape
    tm = min(TOKEN_TILE, t)
    nt = t // tm
    row = lambda width: pl.BlockSpec((1, tm, width), lambda b, i: (b, i, 0))
    vec = pl.BlockSpec((1, D_MODEL), lambda b, i: (0, 0))
    tab = pl.BlockSpec((tm, LANES), lambda b, i: (i, 0))
    in_specs = [row(D_MODEL), _mod_spec(shift, tm), _mod_spec(scale, tm), _mod_spec(kshift, tm),
                _mod_spec(kscale, tm), vec, vec,
                pl.BlockSpec((D_MODEL, 2 * B_WIDTH), lambda b, i: (0, 0)),
                pl.BlockSpec((D_MODEL, 2 * B_KV_WIDTH), lambda b, i: (0, 0)), tab, tab]
    args = (x, shift, scale, kshift, kscale, g, gkv, w, wkv, cos, sin)
    if sample:
        return pl.pallas_call(
            _inproj_b_sample_kernel, grid=(nb, nt), in_specs=in_specs,
            out_specs=[row(B_WIDTH), row(B_WIDTH), row(B_KV_WIDTH), row(B_KV_WIDTH)],
            out_shape=[jax.ShapeDtypeStruct((nb, t, B_WIDTH), F32), jax.ShapeDtypeStruct((nb, t, B_WIDTH), BF16),
                       jax.ShapeDtypeStruct((nb, t, B_KV_WIDTH), F32),
                       jax.ShapeDtypeStruct((nb, t, B_KV_WIDTH), F32)],
            compiler_params=_ARB(2), name="inproj_b_sample",
        )(*args)
    out_specs = [row(B_WIDTH), row(B_WIDTH), row(B_KV_WIDTH), row(B_KV_WIDTH),
                 pl.BlockSpec((1, 2 * B_KV_WIDTH, B_WINDOW), lambda b, i: (b, 0, 0))]
    out_shape = [jax.ShapeDtypeStruct((nb, t, B_WIDTH), BF16), jax.ShapeDtypeStruct((nb, t, B_WIDTH), BF16),
                 jax.ShapeDtypeStruct((nb, t, B_KV_WIDTH), BF16), jax.ShapeDtypeStruct((nb, t, B_KV_WIDTH), BF16),
                 jax.ShapeDtypeStruct((nb, 2 * B_KV_WIDTH, B_WINDOW), F32)]
    args, aliases = list(args), {}
    if job is not None:
        assert job.q.shape[0] == nb * nt * job.per_step
        j_in, j_args, j_out, j_shape, aliases = _job_operands(job, lambda b, i: b * nt + i, len(args),
                                                              len(out_specs))
        in_specs, args = in_specs + j_in, args + j_args
        out_specs, out_shape = out_specs + j_out, out_shape + j_shape
    return pl.pallas_call(
        functools.partial(_inproj_b_prompt_kernel, tm=tm, nt=nt, job=job.static if job else None),
        grid=(nb, nt), in_specs=in_specs, out_specs=out_specs, out_shape=out_shape,
        input_output_aliases=aliases, compiler_params=_ARB(2), name="inproj_b_prompt",
    )(*args)


def _eye():
    r = lax.broadcasted_iota(jnp.int32, (LANES, LANES), 0)
    c = lax.broadcasted_iota(jnp.int32, (LANES, LANES), 1)
    return r == c


def _row_to_col(row):
    eye = _eye()
    chunks = [jnp.sum(jnp.where(eye, row[:, j * LANES:(j + 1) * LANES], 0.0), axis=1, keepdims=True)
              for j in range(row.shape[1] // LANES)]
    return chunks[0] if len(chunks) == 1 else jnp.concatenate(chunks, axis=0)


def _shift_rows(cache_ref, out_ref, e, new_col):
    ntile = cache_ref.shape[2] // LANES
    lane = lax.broadcasted_iota(jnp.int32, (1, LANES), 1)
    nxt = pltpu.roll(cache_ref[e, :, 0:LANES], LANES - 1, 1)
    for j in range(ntile):
        cur = nxt
        if j + 1 < ntile:
            nxt = pltpu.roll(cache_ref[e, :, (j + 1) * LANES:(j + 2) * LANES], LANES - 1, 1)
            fill = nxt
        else:
            fill = new_col
        out_ref[e, :, j * LANES:(j + 1) * LANES] = jnp.where(lane < LANES - 1, cur, fill)


_NT = (((1,), (1,)), ((), ()))


def _sample_dilated_step(q_ref, kn_ref, vn_ref, cache_ref, out_cache_ref, o_ref, lse_ref, step, *, dil, bb):
    width, length = q_ref.shape[1], cache_ref.shape[2]
    sel = (lax.broadcasted_iota(jnp.int32, (8, width), 0)
           == lax.broadcasted_iota(jnp.int32, (8, width), 1) // HEAD_DIM)
    valid = lax.broadcasted_iota(jnp.int32, (1, length), 1) % dil == 0
    per_head = lambda x: jnp.sum(jnp.where(sel, x, 0.0), axis=0, keepdims=True)
    for e in range(bb):
        row = pl.ds(step * bb + e, 1)
        q, kn, vn = q_ref[row, :], kn_ref[row, :], vn_ref[row, :]
        qb = jnp.where(sel, q, 0.0)
        kt = cache_ref[e, 0:width, :].astype(BF16)
        vt = cache_ref[e, width:2 * width, :].astype(BF16)
        s = jnp.where(valid, _dot(qb.astype(BF16), kt), NEG)
        s_new = jnp.sum(qb * kn, axis=1, keepdims=True)
        m = jnp.maximum(jnp.max(s, axis=1, keepdims=True), s_new)
        p = jnp.exp(s - m)
        p_new = jnp.exp(s_new - m)
        l = per_head(jnp.sum(p, axis=1, keepdims=True) + p_new)
        o = lax.dot_general(p.astype(BF16), vt, _NT, preferred_element_type=F32)
        o_ref[row, :] = (per_head(o) + per_head(p_new) * vn) * (1.0 / l)
        lse_ref[row, :] = per_head(m) + jnp.log(l)
        _shift_rows(cache_ref, out_cache_ref, e, jnp.concatenate([_row_to_col(kn), _row_to_col(vn)], axis=0))


def _sample_dilated_kernel(q_ref, kn_ref, vn_ref, cache_ref, out_cache_ref, o_ref, lse_ref, *, dil, bb):
    _sample_dilated_step(q_ref, kn_ref, vn_ref, cache_ref, out_cache_ref, o_ref, lse_ref, pl.program_id(0),
                         dil=dil, bb=bb)


class _ShiftJob(NamedTuple):
    q: jax.Array
    k_new: jax.Array
    v_new: jax.Array
    cache: jax.Array
    first: int
    per_step: int
    dil: int
    prev_out: Optional[jax.Array]

    @property
    def static(self):
        return (4 if self.prev_out is None else 5, self.dil, self.per_step)


def _job_operands(job, step_of, inputs_before, outputs_before):
    count, wq = job.q.shape
    _, chans, length = job.cache.shape
    first_blk = job.first // job.per_step
    full = pl.BlockSpec((count, wq), lambda *ids: (0, 0))
    tile = pl.BlockSpec((job.per_step, chans, length), lambda *ids: (first_blk + step_of(*ids), 0, 0))
    in_specs, args, aliases = [full, full, full, tile], [job.q, job.k_new, job.v_new, job.cache], {}
    if job.prev_out is not None:
        in_specs.append(pl.BlockSpec(memory_space=pl.ANY))
        args.append(job.prev_out)
        aliases[inputs_before + 4] = outputs_before
    out_shape = [jax.ShapeDtypeStruct(job.cache.shape, F32)] + [jax.ShapeDtypeStruct((count, wq), F32)] * 2
    return in_specs, args, [tile, full, full], out_shape, aliases


def _split_job_refs(rest, job_static, n_main_out):
    n_in = job_static[0] if job_static else 0
    n_out = 3 if job_static else 0
    job_in, rest = rest[:n_in], rest[n_in:]
    return job_in, rest[:n_main_out], rest[n_main_out:n_main_out + n_out], rest[n_main_out + n_out:]


def _run_job(job_static, in_refs, out_refs, step):
    _, dil, per_step = job_static
    _sample_dilated_step(*in_refs[:4], *out_refs, step, dil=dil, bb=per_step)


def _sample_swa_kernel(q_ref, kn_ref, vn_ref, cache_ref, sink_ref, out_cache_ref, o_ref, qexp, oexp, *, bb):
    step = pl.program_id(0)
    nb, length = q_ref.shape[0], cache_ref.shape[2]
    low = lax.broadcasted_iota(jnp.int32, (1, LANES), 1) < HEAD_DIM

    @pl.when(step == 0)
    def _():
        for hq in range(B_Q_HEADS):
            slab = q_ref[:, (hq // 2) * LANES:(hq // 2 + 1) * LANES]
            src_low, dst_low = hq % 2 == 0, hq // B_GROUP == 0
            x = slab if src_low == dst_low else pltpu.roll(slab, HEAD_DIM, 1)
            qexp[hq * nb:(hq + 1) * nb, :] = jnp.where(low if dst_low else jnp.logical_not(low), x, 0.0)

    own_half = (lax.broadcasted_iota(jnp.int32, (B_Q_HEADS, LANES), 0) // B_GROUP
                == lax.broadcasted_iota(jnp.int32, (B_Q_HEADS, LANES), 1) // HEAD_DIM)
    valid = lax.broadcasted_iota(jnp.int32, (1, length), 1) >= 1
    sink = sink_ref[:, 0:1]
    for e in range(bb):
        b = step * bb + e
        heads = pl.ds(b, B_Q_HEADS, stride=nb)
        kn, vn = kn_ref[pl.ds(b, 1), :], vn_ref[pl.ds(b, 1), :]
        qb = qexp[heads, :]
        kt = cache_ref[e, 0:B_KV_WIDTH, :].astype(BF16)
        vt = cache_ref[e, B_KV_WIDTH:2 * B_KV_WIDTH, :].astype(BF16)
        s = jnp.where(valid, _dot(qb.astype(BF16), kt), NEG)
        s_new = jnp.sum(qb * kn, axis=1, keepdims=True)
        m = jnp.maximum(jnp.maximum(jnp.max(s, axis=1, keepdims=True), s_new), sink)
        p = jnp.exp(s - m)
        p_new = jnp.exp(s_new - m)
        l = jnp.sum(p, axis=1, keepdims=True) + p_new + jnp.exp(sink - m)
        o = lax.dot_general(p.astype(BF16), vt, _NT, preferred_element_type=F32)
        oexp[heads, :] = jnp.where(own_half, o + p_new * vn, 0.0) * (1.0 / l)
        _shift_rows(cache_ref, out_cache_ref, e, jnp.concatenate([_row_to_col(kn), _row_to_col(vn)], axis=0))

    @pl.when(step == pl.num_programs(0) - 1)
    def _():
        for j in range(B_Q_HEADS // 2):
            even, odd = oexp[2 * j * nb:(2 * j + 1) * nb, :], oexp[(2 * j + 1) * nb:(2 * j + 2) * nb, :]
            if 2 * j // B_GROUP == 0:
                odd = pltpu.roll(odd, HEAD_DIM, 1)
            else:
                even = pltpu.roll(even, HEAD_DIM, 1)
            o_ref[:, j * LANES:(j + 1) * LANES] = jnp.where(low, even, odd)


def _sample_attn(q, k_new, v_new, cache_t, *, dil=1, sinks=None):
    nb, chans, length = cache_t.shape
    wq, wkv = q.shape[1], k_new.shape[1]
    bb = max(1, min(8, 2048 // length))
    full = lambda w: pl.BlockSpec((nb, w), lambda s: (0, 0))
    tile = pl.BlockSpec((bb, chans, length), lambda s: (s, 0, 0))
    in_specs = [full(wq), full(wkv), full(wkv), tile]
    args = [q, k_new, v_new, cache_t]
    out_specs = [tile, full(wq)]
    out_shape = [jax.ShapeDtypeStruct(cache_t.shape, F32), jax.ShapeDtypeStruct((nb, wq), F32)]
    if sinks is None:
        kern = functools.partial(_sample_dilated_kernel, dil=dil, bb=bb)
        out_specs.append(full(wq))
        out_shape.append(jax.ShapeDtypeStruct((nb, wq), F32))
        scratch = []
    else:
        kern = functools.partial(_sample_swa_kernel, bb=bb)
        in_specs.append(pl.BlockSpec(sinks.shape, lambda s: (0, 0)))
        args.append(sinks)
        scratch = [pltpu.VMEM((B_Q_HEADS * nb, LANES), F32)] * 2
    return pl.pallas_call(
        kern, grid=(nb // bb,), in_specs=in_specs, out_specs=out_specs, out_shape=out_shape,
        scratch_shapes=scratch, compiler_params=_ARB(1),
        name=f"sample_attn_d{dil}_l{length}" if sinks is None else "sample_swa",
    )(*args)


def _rope_tables(pos):
    half = HEAD_DIM // 2
    inv = ROPE_THETA ** (-jnp.arange(half, dtype=F32) / half)
    ang = pos.astype(F32)[:, None] * inv[None, :]
    cos, sin = jnp.cos(ang), jnp.sin(ang)
    return jnp.tile(cos, (1, 4)), jnp.tile(jnp.concatenate([-sin, sin], axis=1), (1, 2))


def _to_tiles(cache):
    nb, length = cache.shape[0], cache.shape[1]
    return jnp.transpose(cache, (0, 2, 3, 4, 1)).reshape(nb, -1, length)


def _from_tiles(tiles, heads):
    nb, _, length = tiles.shape
    return jnp.transpose(tiles.reshape(nb, 2, heads, HEAD_DIM, length), (0, 4, 1, 2, 3))


def kernel(x_prompt, x_sample, c_prompt, c_sample, cache_a_kv_g0, cache_a_kv_g1, cache_a_kv_g2, cache_b_kv, ada_w,
           ada_b, g_pre, g_post, w_in_a, w_o_a, w_in_b, w_o_b, sinks_b, ada_kv_w, ada_kv_b, g_kv, w_kv):
    nbp, t, _ = x_prompt.shape
    nbs = x_sample.shape[0]
    assert x_sample.shape[1] == 1

    c_all = jnp.concatenate([c_prompt, c_sample], axis=0)
    mod = _ada(c_all, ada_w, ada_b)
    mod_kv = _ada(c_all, ada_kv_w[None], ada_kv_b[None])[0]

    def split(a, parts, sample):
        rows = a[nbp:][None] if sample else a[:nbp][:, None]
        return [rows[..., p * D_MODEL:(p + 1) * D_MODEL] for p in range(parts)]

    w_in_a_b, w_o_a_b = w_in_a[0].astype(BF16), w_o_a[0].astype(BF16)
    w_in_b_b, w_o_b_b, w_kv_b = w_in_b[0].astype(BF16), w_o_b[0].astype(BF16), w_kv.astype(BF16)
    g_pre0, g_pre1 = g_pre[0:1], g_pre[1:2]
    g_post0, g_post1 = g_post[0:1], g_post[1:2]
    g_kv_r = g_kv[None]
    sinks = jnp.broadcast_to(sinks_b[0][:, None], (B_Q_HEADS, LANES))

    xs = x_sample.reshape(1, nbs, D_MODEL)
    cos_s, sin_s = _rope_tables(jnp.full((nbs,), PAST_LEN, jnp.int32))
    sh0s, sc0s, gt0s = split(mod[0], 3, True)
    sh1s, sc1s, gt1s = split(mod[1], 3, True)
    kshs, kscs = split(mod_kv, 2, True)
    qs, ks, vs, gzs = _inproj_a(xs, sh0s, sc0s, g_pre0, w_in_a_b, cos_s, sin_s, sample=True)
    tiles = [_to_tiles(c[0]) for c in (cache_a_kv_g0, cache_a_kv_g1, cache_a_kv_g2)]

    def job(g, first, count, per_step, prev_out=None):
        rows, cols = slice(first, first + count), slice(g * GROUP_WIDTH, (g + 1) * GROUP_WIDTH)
        return _ShiftJob(qs[0, rows, cols], ks[0, rows, cols], vs[0, rows, cols], tiles[g], first, per_step,
                         A_GROUPS[g][1], prev_out)

    proj_steps = nbp * (t // min(TOKEN_TILE, t))

    cos_p, sin_p = _rope_tables(jnp.arange(t))
    sh0, sc0, gt0 = split(mod[0], 3, False)
    sh1, sc1, gt1 = split(mod[1], 3, False)
    ksh, ksc = split(mod_kv, 2, False)
    *qkv, gz, ca0, ca1, ca2, shifted2, o2a, lse2a = _inproj_a(
        x_prompt, sh0, sc0, g_pre0, w_in_a_b, cos_p, sin_p, sample=False, job=job(2, 0, proj_steps, 1))
    g1_steps = nbp * _dilated_steps(A_GROUPS[1][1])[1]
    attn = [_dilated_attn(*qkv[0:3], 0),
            _dilated_attn(*qkv[3:6], 1, job=job(0, 0, nbs, nbs // g1_steps)),
            _dilated_attn(*qkv[6:9], 2)]
    os, lses = [a[0] for a in attn], [a[1] for a in attn]
    shifted0, o0, lse0 = attn[1][2:]
    x1 = _outproj(os, lses, gz, x_prompt, gt0, g_post0, w_o_a_b)
    qb, gzb, kd, vd, cb, shifted1, o1, lse1 = _inproj_b(
        x1, sh1, sc1, ksh, ksc, g_pre1, g_kv_r, w_in_b_b, w_kv_b, cos_p, sin_p, sample=False,
        job=job(1, 0, nbs, nbs // proj_steps))
    ob, shifted2, o2b, lse2b = _swa_attn(qb, kd, vd, sinks,
                                         job=job(2, proj_steps, nbs - proj_steps, 1, prev_out=shifted2))
    y_prompt = _outproj([ob], None, gzb, x1, gt1, g_post1, w_o_b_b)
    new_a_prompt = [_from_tiles(c, A_HEADS_PER_GROUP)[None] for c in (ca0, ca1, ca2)]
    new_b_prompt = _from_tiles(cb, B_KV_HEADS)

    os = [o0, o1, jnp.concatenate([o2a, o2b], axis=0)]
    lses = [lse0, lse1, jnp.concatenate([lse2a, lse2b], axis=0)]
    os = [o.astype(BF16)[None, None] for o in os]
    lses = [l[None, None] for l in lses]
    new_a_sample = [_from_tiles(s, A_HEADS_PER_GROUP)[None] for s in (shifted0, shifted1, shifted2)]
    sh1, sc1, gt1, ksh, ksc, gz, gt0 = sh1s, sc1s, gt1s, kshs, kscs, gzs, gt0s
    xs1 = _outproj(os, lses, gz, xs, gt0, g_post0, w_o_a_b)
    qb, gzb, kb, vb = _inproj_b(xs1, sh1, sc1, ksh, ksc, g_pre1, g_kv_r, w_in_b_b, w_kv_b, cos_s, sin_s,
                                sample=True)
    shifted_b, ob = _sample_attn(qb[0], kb[0], vb[0], _to_tiles(cache_b_kv), sinks=sinks)
    y_sample = _outproj([ob.astype(BF16)[None]], None, gzb, xs1, gt1, g_post1, w_o_b_b).reshape(nbs, 1, D_MODEL)
    new_b_sample = _from_tiles(shifted_b, B_KV_HEADS)

    return (y_prompt, y_sample, *new_a_prompt, new_b_prompt, *new_a_sample, new_b_sample)
```

```python
import functools
from typing import NamedTuple, Optional

import jax
import jax.numpy as jnp
from jax import lax
from jax.experimental import pallas as pl
from jax.experimental.pallas import tpu as pltpu

D_MODEL = 1024
HEAD_DIM = 64
A_GROUPS = ((128, 1), (512, 4), (2048, 16))
A_HEADS_PER_GROUP = 4
GROUP_WIDTH = A_HEADS_PER_GROUP * HEAD_DIM
A_WIDTH = GROUP_WIDTH * len(A_GROUPS)
B_Q_HEADS = 16
B_KV_HEADS = 2
B_GROUP = B_Q_HEADS // B_KV_HEADS
B_WIDTH = B_Q_HEADS * HEAD_DIM
B_KV_WIDTH = B_KV_HEADS * HEAD_DIM
B_WINDOW = 128
BLOCK = 128
PAST_LEN = 16384
ROPE_THETA = 10000.0
EPS = 1e-6
NEG = -1e30
LANES = 128
LOG2E = 1.4426950408889634
LN2 = 0.6931471805599453
Q_SCALE = HEAD_DIM ** -0.5 * LOG2E

BF16 = jnp.bfloat16
F32 = jnp.float32
TOKEN_TILE = 512

_ARB = lambda n: pltpu.CompilerParams(dimension_semantics=("arbitrary",) * n)


def _rope(x, cos, sin_signed):
    lane = lax.broadcasted_iota(jnp.int32, (1, LANES), 1)
    first_half = (lane % HEAD_DIM) < (HEAD_DIM // 2)
    outs = []
    for j in range(x.shape[1] // LANES):
        xs = x[:, j * LANES:(j + 1) * LANES]
        partner = jnp.where(first_half, pltpu.roll(xs, LANES - 32, 1), pltpu.roll(xs, 32, 1))
        outs.append(xs * cos + partner * sin_signed)
    return outs[0] if len(outs) == 1 else jnp.concatenate(outs, axis=1)


def _normalize(x):
    return x * lax.rsqrt(jnp.mean(x * x, axis=-1, keepdims=True) + EPS)


def _dot(a, b):
    return jnp.dot(a, b, preferred_element_type=F32)


def _ada_kernel(c_ref, w_ref, b_ref, o_ref):
    s = jax.nn.silu(c_ref[...]).astype(BF16)
    o_ref[...] = _dot(s, w_ref[...].astype(BF16)) + b_ref[...]


def _ada(c, w, b):
    nl, _, n = w.shape
    m = c.shape[0]
    tn = 1024
    return pl.pallas_call(
        _ada_kernel,
        grid=(nl, n // tn),
        in_specs=[
            pl.BlockSpec((m, D_MODEL), lambda l, j: (0, 0)),
            pl.BlockSpec((None, D_MODEL, tn), lambda l, j: (l, 0, j)),
            pl.BlockSpec((None, 1, tn), lambda l, j: (l, 0, j)),
        ],
        out_specs=pl.BlockSpec((None, m, tn), lambda l, j: (l, 0, j)),
        out_shape=jax.ShapeDtypeStruct((nl, m, n), F32),
        compiler_params=_ARB(2),
        name="ada",
    )(c, w, b.reshape(nl, 1, n))


ROW_CHUNK = 512


def _row_chunks(tm):
    size = min(ROW_CHUNK, tm)
    return [slice(c * size, (c + 1) * size) for c in range(tm // size)]


def _mod_rows(ref, rows):
    return ref[0] if ref.shape[1] == 1 else ref[0, rows, :]


def _inproj_a_body(x_ref, sh_ref, sc_ref, g_ref, w_ref, cos_ref, sin_ref, rows):
    h = _normalize(x_ref[0, rows, :]) * g_ref[...]
    hb = (h * (1.0 + _mod_rows(sc_ref, rows)) + _mod_rows(sh_ref, rows)).astype(BF16)
    cos, sin = cos_ref[rows, :], sin_ref[rows, :]
    q = _rope(_dot(hb, w_ref[:, 0:A_WIDTH]), cos, sin) * Q_SCALE
    k = _rope(_dot(hb, w_ref[:, A_WIDTH:2 * A_WIDTH]), cos, sin)
    v = _dot(hb, w_ref[:, 2 * A_WIDTH:3 * A_WIDTH])
    gz = jax.nn.silu(_dot(hb, w_ref[:, 3 * A_WIDTH:4 * A_WIDTH]))
    return q, k, v, gz


def _inproj_a_prompt_kernel(x_ref, sh_ref, sc_ref, g_ref, w_ref, cos_ref, sin_ref, *rest, tm, nt, job):
    job_in, main_out, job_out, (scr,) = _split_job_refs(rest, job, 13)
    qkv_refs, (gz_ref, c0_ref, c1_ref, c2_ref) = main_out[:9], main_out[9:]
    i = pl.program_id(1)
    if job:
        _run_job(job, job_in, job_out, pl.program_id(0) * nt + i)
    slabs_per_group = GROUP_WIDTH // LANES
    for c, rows in enumerate(_row_chunks(tm)):
        n = rows.stop - rows.start
        q, k, v, gz = _inproj_a_body(x_ref, sh_ref, sc_ref, g_ref, w_ref, cos_ref, sin_ref, rows)
        gz_ref[0, rows, :] = gz.astype(BF16)
        for which, val in enumerate((q, k, v)):
            for j in range(A_WIDTH // LANES):
                scr[c, which, j] = val[:, j * LANES:(j + 1) * LANES]
            for g, (_, dil) in enumerate(A_GROUPS):
                out_ref = qkv_refs[3 * g + which]
                dst = slice(rows.start // dil, rows.stop // dil)
                for r in range(dil):
                    for h in range(slabs_per_group):
                        part = scr[c, which, g * slabs_per_group + h, pl.ds(r, n // dil, stride=dil), :]
                        out_ref[0, r, dst, h * LANES:(h + 1) * LANES] = part.astype(BF16)
        for g, c_ref in enumerate((c0_ref, c1_ref, c2_ref)):
            win = A_GROUPS[g][0]
            width = min(tm, win)
            first_tile = nt - max(win // tm, 1)
            lo = max(rows.start, tm - width)
            if lo >= rows.stop:
                continue

            @pl.when(i >= first_tile)
            def _(g=g, c_ref=c_ref, lo=lo, k=k, v=v, rows=rows, width=width):
                cols = slice(g * GROUP_WIDTH, (g + 1) * GROUP_WIDTH)
                dst = slice(lo - (tm - width), rows.stop - (tm - width))
                c_ref[0, 0:GROUP_WIDTH, dst] = k[lo - rows.start:, cols].T
                c_ref[0, GROUP_WIDTH:2 * GROUP_WIDTH, dst] = v[lo - rows.start:, cols].T


def _inproj_a_sample_kernel(x_ref, sh_ref, sc_ref, g_ref, w_ref, cos_ref, sin_ref,
                            q_ref, k_ref, v_ref, gz_ref):
    q, k, v, gz = _inproj_a_body(x_ref, sh_ref, sc_ref, g_ref, w_ref, cos_ref, sin_ref, slice(0, x_ref.shape[1]))
    q_ref[0] = q
    k_ref[0] = k
    v_ref[0] = v
    gz_ref[0] = gz.astype(BF16)


def _mod_spec(mod, tm):
    if mod.shape[1] == 1:
        return pl.BlockSpec((1, 1, D_MODEL), lambda b, i: (b, 0, 0))
    return pl.BlockSpec((1, tm, D_MODEL), lambda b, i: (b, i, 0))


def _inproj_a(x, shift, scale, g, w, cos, sin, *, sample, job=None):
    nb, t, _ = x.shape
    tm = min(TOKEN_TILE, t)
    nt = t // tm
    row = lambda width: pl.BlockSpec((1, tm, width), lambda b, i: (b, i, 0))
    in_specs = [
        row(D_MODEL), _mod_spec(shift, tm), _mod_spec(scale, tm),
        pl.BlockSpec((1, D_MODEL), lambda b, i: (0, 0)),
        pl.BlockSpec((D_MODEL, 4 * A_WIDTH), lambda b, i: (0, 0)),
        pl.BlockSpec((tm, LANES), lambda b, i: (i, 0)),
        pl.BlockSpec((tm, LANES), lambda b, i: (i, 0)),
    ]
    if sample:
        out_shape = [jax.ShapeDtypeStruct((nb, t, A_WIDTH), dt) for dt in (F32, F32, F32, BF16)]
        return pl.pallas_call(
            _inproj_a_sample_kernel, grid=(nb, nt), in_specs=in_specs,
            out_specs=[row(A_WIDTH)] * 4, out_shape=out_shape,
            compiler_params=_ARB(2), name="inproj_a_sample",
        )(x, shift, scale, g, w, cos, sin)
    out_shape, out_specs = [], []
    for _, dil in A_GROUPS:
        out_shape += [jax.ShapeDtypeStruct((nb, dil, t // dil, GROUP_WIDTH), BF16)] * 3
        out_specs += [pl.BlockSpec((1, dil, tm // dil, GROUP_WIDTH), lambda b, i: (b, 0, i, 0))] * 3
    out_shape.append(jax.ShapeDtypeStruct((nb, t, A_WIDTH), BF16))
    out_specs.append(row(A_WIDTH))
    for win, _ in A_GROUPS:
        rows = min(tm, win)
        first_tile = nt - max(win // tm, 1)
        out_shape.append(jax.ShapeDtypeStruct((nb, 2 * GROUP_WIDTH, min(win, t)), F32))
        out_specs.append(pl.BlockSpec(
            (1, 2 * GROUP_WIDTH, rows),
            lambda b, i, first_tile=first_tile: (b, 0, jnp.maximum(i - first_tile, 0))))
    args, aliases = [x, shift, scale, g, w, cos, sin], {}
    if job is not None:
        assert job.q.shape[0] == nb * nt * job.per_step
        j_in, j_args, j_out, j_shape, aliases = _job_operands(job, lambda b, i: b * nt + i, len(args),
                                                              len(out_specs))
        in_specs, args = in_specs + j_in, args + j_args
        out_specs, out_shape = out_specs + j_out, out_shape + j_shape
    return pl.pallas_call(
        functools.partial(_inproj_a_prompt_kernel, tm=tm, nt=nt, job=job.static if job else None),
        grid=(nb, nt), in_specs=in_specs, out_specs=out_specs, out_shape=out_shape,
        scratch_shapes=[pltpu.VMEM((tm // min(ROW_CHUNK, tm), 3, A_WIDTH // LANES, min(ROW_CHUNK, tm), LANES), F32)],
        input_output_aliases=aliases, compiler_params=_ARB(2), name="inproj_a_prompt",
    )(*args)


def _query_minus_key():
    kj = lax.broadcasted_iota(jnp.int32, (2 * BLOCK, BLOCK), 0)
    qi = lax.broadcasted_iota(jnp.int32, (2 * BLOCK, BLOCK), 1)
    return qi - kj


def _head_attend(st, mask, vt_ext, sink):
    st = jnp.where(mask, st, NEG)
    m = jnp.max(st, axis=0, keepdims=True)
    if sink is not None:
        m = jnp.maximum(m, sink)
    ext = _dot(vt_ext, jnp.exp2(st - m).astype(BF16))
    l = ext[HEAD_DIM:HEAD_DIM + 1, :]
    if sink is not None:
        l = l + jnp.exp2(sink - m)
    return ext[0:HEAD_DIM, :] * (1.0 / l), m, l


def _values_ext(vt_prev, vt_cur, head):
    rows = slice(head * HEAD_DIM, (head + 1) * HEAD_DIM)
    ones = jnp.ones((16, 2 * BLOCK), BF16)
    return jnp.concatenate([jnp.concatenate([vt_prev[rows], vt_cur[rows]], axis=1), ones], axis=0)


def _block_diag_queries(qt, heads, kv_of_head, kv_heads):
    zeros = jnp.zeros((HEAD_DIM, BLOCK), BF16)
    cols = []
    for h in range(heads):
        pieces = [zeros] * kv_heads
        pieces[kv_of_head(h)] = qt[h * HEAD_DIM:(h + 1) * HEAD_DIM]
        cols.append(jnp.concatenate(pieces, axis=0))
    return jnp.concatenate(cols, axis=1)


ATTN_BLOCKS_IN_FLIGHT = 8


def _dilated_attn_kernel(q_ref, k_ref, v_ref, *rest, job):
    job_in, (o_ref, lse_ref), job_out, (vt_ref,) = _split_job_refs(rest, job, 2)
    if job:
        _run_job(job, job_in, job_out, pl.program_id(0) * pl.num_programs(1) + pl.program_id(1))
    nres, m_rows = q_ref.shape[1], q_ref.shape[2]
    nblk = m_rows // BLOCK
    base = _query_minus_key()
    heads = A_HEADS_PER_GROUP
    for r in range(nres):
        def transpose_values(j, c, r=r):
            rows = pl.ds(pl.multiple_of(j * BLOCK, BLOCK), BLOCK)
            vt_ref[r, j] = v_ref[0, r, rows, :].T
            return c

        lax.fori_loop(0, nblk, transpose_values, 0)

    def block(r, i):
        prev = jnp.maximum(i - 1, 0)
        dist = base + (i - prev) * BLOCK
        mask = (dist >= 0) & (dist <= BLOCK)
        qrows = pl.ds(pl.multiple_of(i * BLOCK, BLOCK), BLOCK)
        k = k_ref[0, r, pl.ds(pl.multiple_of(prev * BLOCK, BLOCK), 2 * BLOCK), :]
        qd = _block_diag_queries(q_ref[0, r, qrows, :].T, heads, lambda h: h, heads)
        st_all = _dot(k, qd)
        vt_prev, vt_cur = vt_ref[r, prev], vt_ref[r, i]
        outs, lses = [], []
        for h in range(heads):
            o_t, m, l = _head_attend(st_all[:, h * BLOCK:(h + 1) * BLOCK], mask,
                                     _values_ext(vt_prev, vt_cur, h), None)
            outs.append(o_t)
            lses.append(jnp.broadcast_to(m * LN2 + jnp.log(l), (HEAD_DIM, BLOCK)))
        o_ref[0, r, qrows, :] = jnp.concatenate(outs, axis=0).T.astype(BF16)
        lse_ref[0, r, qrows, :] = jnp.concatenate(lses, axis=0).T

    if nres * nblk <= ATTN_BLOCKS_IN_FLIGHT:
        for r in range(nres):
            for i in range(nblk):
                block(r, i)
    else:
        assert nres == 1 and nblk % ATTN_BLOCKS_IN_FLIGHT == 0

        def chunk(c, carry):
            for u in range(ATTN_BLOCKS_IN_FLIGHT):
                block(0, c * ATTN_BLOCKS_IN_FLIGHT + u)
            return carry

        lax.fori_loop(0, nblk // ATTN_BLOCKS_IN_FLIGHT, chunk, 0)


def _dilated_steps(dil):
    nres = max(1, dil // 4)
    return nres, dil // nres


def _dilated_attn(q, k, v, g, job=None):
    nb, dil, m_rows, _ = q.shape
    nres, nsteps = _dilated_steps(dil)
    blk = pl.BlockSpec((1, nres, m_rows, GROUP_WIDTH), lambda b, r: (b, r, 0, 0))
    in_specs, args, aliases = [blk, blk, blk], [q, k, v], {}
    out_specs = [blk, blk]
    out_shape = [jax.ShapeDtypeStruct(q.shape, BF16), jax.ShapeDtypeStruct(q.shape, F32)]
    if job is not None:
        assert job.q.shape[0] == nb * nsteps * job.per_step
        j_in, j_args, j_out, j_shape, aliases = _job_operands(job, lambda b, r: b * nsteps + r, len(args), 2)
        in_specs, args = in_specs + j_in, args + j_args
        out_specs, out_shape = out_specs + j_out, out_shape + j_shape
    return pl.pallas_call(
        functools.partial(_dilated_attn_kernel, job=job.static if job else None),
        grid=(nb, nsteps), in_specs=in_specs, out_specs=out_specs, out_shape=out_shape,
        scratch_shapes=[pltpu.VMEM((nres, m_rows // BLOCK, GROUP_WIDTH, BLOCK), BF16)],
        input_output_aliases=aliases, compiler_params=_ARB(2), name=f"dilated_attn_g{g}",
    )(*args)


SWA_QUERY_TILE = 512


def _swa_attn_kernel(q_ref, k_ref, v_ref, sink_ref, *rest, job):
    job_in, (o_ref,), job_out, (vt_ref,) = _split_job_refs(rest, job, 1)
    j = pl.program_id(1)
    seq = k_ref.shape[1]
    nq = q_ref.shape[1] // BLOCK
    base = _query_minus_key()
    if job:
        new_cols = _run_job(job, job_in, job_out, pl.program_id(0) * pl.num_programs(1) + j, shift_now=False)

    @pl.when(j == 0)
    def _():
        def transpose_values(t, c):
            vt_ref[t] = v_ref[0, pl.ds(pl.multiple_of(t * BLOCK, BLOCK), BLOCK), :].T
            return c

        lax.fori_loop(0, seq // BLOCK, transpose_values, 0)

    for i in range(nq):
        cur = j * nq + i
        prev = jnp.maximum(cur - 1, 0)
        dist = base + (cur - prev) * BLOCK
        mask = (dist >= 0) & (dist < B_WINDOW)
        qrows = slice(i * BLOCK, (i + 1) * BLOCK)
        k = k_ref[0, pl.ds(pl.multiple_of(prev * BLOCK, BLOCK), 2 * BLOCK), :]
        qt = q_ref[0, qrows, :].T
        vt_prev, vt_cur = vt_ref[prev], vt_ref[cur]
        outs = []
        for kvh in range(B_KV_HEADS):
            qd = _block_diag_queries(qt[kvh * B_GROUP * HEAD_DIM:(kvh + 1) * B_GROUP * HEAD_DIM], B_GROUP,
                                     lambda h, kvh=kvh: kvh, B_KV_HEADS)
            st_all = _dot(k, qd)
            vt_ext = _values_ext(vt_prev, vt_cur, kvh)
            for gq in range(B_GROUP):
                hq = kvh * B_GROUP + gq
                o_t, _, _ = _head_attend(st_all[:, gq * BLOCK:(gq + 1) * BLOCK], mask, vt_ext,
                                         sink_ref[hq:hq + 1, :])
                outs.append(o_t)
        o_ref[0, qrows, :] = jnp.concatenate(outs, axis=0).T.astype(BF16)
        if job:
            _job_shift_part(job, job_in, job_out, new_cols, i, nq)


def _swa_attn(q, k, v, sinks, job=None):
    nb, t, _ = q.shape
    tq = min(SWA_QUERY_TILE, t)
    nj = t // tq
    qblk = pl.BlockSpec((1, tq, B_WIDTH), lambda b, j: (b, j, 0))
    kvblk = pl.BlockSpec((1, t, B_KV_WIDTH), lambda b, j: (b, 0, 0))
    in_specs = [qblk, kvblk, kvblk, pl.BlockSpec((B_Q_HEADS, LANES), lambda b, j: (0, 0))]
    args, aliases = [q, k, v, sinks], {}
    out_specs, out_shape = [qblk], [jax.ShapeDtypeStruct((nb, t, B_WIDTH), BF16)]
    if job is not None:
        assert job.q.shape[0] == nb * nj * job.per_step
        j_in, j_args, j_out, j_shape, aliases = _job_operands(job, lambda b, j: b * nj + j, len(args), 1)
        in_specs, args = in_specs + j_in, args + j_args
        out_specs, out_shape = out_specs + j_out, out_shape + j_shape
    return pl.pallas_call(
        functools.partial(_swa_attn_kernel, job=job.static if job else None),
        grid=(nb, nj), in_specs=in_specs, out_specs=out_specs, out_shape=out_shape,
        scratch_shapes=[pltpu.VMEM((t // BLOCK, B_KV_WIDTH, BLOCK), BF16)],
        input_output_aliases=aliases, compiler_params=_ARB(2), name="swa_attn",
    )(*args)


def _finish(a_parts, w_ref, x_ref, gate_ref, g_ref, y_ref):
    acc = None
    row = 0
    for a in a_parts:
        part = _dot(a, w_ref[row:row + a.shape[1], :])
        acc = part if acc is None else acc + part
        row += a.shape[1]
    y_ref[0] = x_ref[0] + gate_ref[0] * (_normalize(acc) * g_ref[...])


def _in_row_order(ref, scr):
    dil = ref.shape[1]
    if dil == 1:
        return ref[0, 0].astype(F32)
    nslab = ref.shape[3] // LANES
    for r in range(dil):
        for h in range(nslab):
            scr[h, pl.ds(r, ref.shape[2], stride=dil), :] = ref[0, r, :, h * LANES:(h + 1) * LANES].astype(F32)
    return jnp.concatenate([scr[h] for h in range(nslab)], axis=1)


def _outproj_mix_kernel(o0_ref, o1_ref, o2_ref, l0_ref, l1_ref, l2_ref, gz_ref, x_ref, gate_ref, g_ref, w_ref,
                        y_ref, *scratch):
    os = [_in_row_order(r, s) for r, s in zip((o0_ref, o1_ref, o2_ref), scratch[0:3])]
    lses = [_in_row_order(r, s) for r, s in zip((l0_ref, l1_ref, l2_ref), scratch[3:6])]
    top = jnp.maximum(jnp.maximum(lses[0], lses[1]), lses[2])
    es = [jnp.exp(l - top) for l in lses]
    inv = 1.0 / (es[0] + es[1] + es[2])
    parts = []
    for g in range(len(A_GROUPS)):
        gz = gz_ref[0, :, g * GROUP_WIDTH:(g + 1) * GROUP_WIDTH].astype(F32)
        parts.append((os[g] * (es[g] * inv) * gz).astype(BF16))
    _finish(parts, w_ref, x_ref, gate_ref, g_ref, y_ref)


def _outproj_kernel(o_ref, gz_ref, x_ref, gate_ref, g_ref, w_ref, y_ref):
    a = (o_ref[0].astype(F32) * gz_ref[0].astype(F32)).astype(BF16)
    _finish([a], w_ref, x_ref, gate_ref, g_ref, y_ref)


def _outproj(os, lses, gz, x, gate, g, w):
    nb, t, _ = x.shape
    tm = min(TOKEN_TILE, t)
    row = lambda width: pl.BlockSpec((1, tm, width), lambda b, i: (b, i, 0))
    width = w.shape[0]
    tail_specs = [row(width), row(D_MODEL), _mod_spec(gate, tm),
                  pl.BlockSpec((1, D_MODEL), lambda b, i: (0, 0)),
                  pl.BlockSpec((width, D_MODEL), lambda b, i: (0, 0))]
    scratch = []
    if lses is None:
        kern, name = _outproj_kernel, "outproj_b"
        in_specs = [row(width)] + tail_specs
        args = (os[0], gz, x, gate, g, w)
    else:
        kern, name = _outproj_mix_kernel, "outproj_a"
        split = lambda a: pl.BlockSpec((1, a.shape[1], tm // a.shape[1], GROUP_WIDTH), lambda b, i: (b, 0, i, 0))
        in_specs = [split(a) for a in (*os, *lses)] + tail_specs
        args = (*os, *lses, gz, x, gate, g, w)
        scratch = [pltpu.VMEM((GROUP_WIDTH // LANES, tm, LANES), F32)] * 6
    return pl.pallas_call(
        kern, grid=(nb, t // tm), in_specs=in_specs, out_specs=row(D_MODEL),
        out_shape=jax.ShapeDtypeStruct((nb, t, D_MODEL), F32), scratch_shapes=scratch,
        compiler_params=_ARB(2), name=name,
    )(*args)


def _inproj_b_body(x_ref, sh_ref, sc_ref, ksh_ref, ksc_ref, g_ref, gkv_ref, w_ref, wkv_ref, cos_ref, sin_ref, rows):
    xn = _normalize(x_ref[0, rows, :])
    hb = ((xn * g_ref[...]) * (1.0 + _mod_rows(sc_ref, rows)) + _mod_rows(sh_ref, rows)).astype(BF16)
    hk = ((xn * gkv_ref[...]) * (1.0 + _mod_rows(ksc_ref, rows)) + _mod_rows(ksh_ref, rows)).astype(BF16)
    cos, sin = cos_ref[rows, :], sin_ref[rows, :]
    q = _rope(_dot(hb, w_ref[:, 0:B_WIDTH]), cos, sin) * Q_SCALE
    gz = jax.nn.silu(_dot(hb, w_ref[:, B_WIDTH:2 * B_WIDTH]))
    kv = _dot(hk, wkv_ref[...])
    k = _rope(kv[:, 0:B_KV_WIDTH], cos, sin)
    v = kv[:, B_KV_WIDTH:2 * B_KV_WIDTH]
    return q, gz, k, v


def _inproj_b_prompt_kernel(x_ref, sh_ref, sc_ref, ksh_ref, ksc_ref, g_ref, gkv_ref, w_ref, wkv_ref, cos_ref,
                            sin_ref, *rest, tm, nt, job):
    job_in, (q_ref, gz_ref, kd_ref, vd_ref, c_ref), job_out, _ = _split_job_refs(rest, job, 5)
    if job:
        _run_job(job, job_in, job_out, pl.program_id(0) * nt + pl.program_id(1))
    for rows in _row_chunks(tm):
        q, gz, k, v = _inproj_b_body(x_ref, sh_ref, sc_ref, ksh_ref, ksc_ref, g_ref, gkv_ref, w_ref, wkv_ref,
                                     cos_ref, sin_ref, rows)
        q_ref[0, rows, :] = q.astype(BF16)
        gz_ref[0, rows, :] = gz.astype(BF16)
        kd_ref[0, rows, :] = k.astype(BF16)
        vd_ref[0, rows, :] = v.astype(BF16)
        if rows.stop == tm:
            @pl.when(pl.program_id(1) == nt - 1)
            def _(k=k, v=v, n=rows.stop - rows.start):
                c_ref[0, 0:B_KV_WIDTH, :] = k[n - B_WINDOW:, :].T
                c_ref[0, B_KV_WIDTH:2 * B_KV_WIDTH, :] = v[n - B_WINDOW:, :].T


def _inproj_b_sample_kernel(x_ref, sh_ref, sc_ref, ksh_ref, ksc_ref, g_ref, gkv_ref, w_ref, wkv_ref, cos_ref,
                            sin_ref, q_ref, gz_ref, k_ref, v_ref):
    q, gz, k, v = _inproj_b_body(x_ref, sh_ref, sc_ref, ksh_ref, ksc_ref, g_ref, gkv_ref, w_ref, wkv_ref, cos_ref,
                                 sin_ref, slice(0, x_ref.shape[1]))
    q_ref[0] = q
    gz_ref[0] = gz.astype(BF16)
    k_ref[0] = k
    v_ref[0] = v


def _inproj_b(x, shift, scale, kshift, kscale, g, gkv, w, wkv, cos, sin, *, sample, job=None):
    nb, t, _ = x.shape
    tm = min(TOKEN_TILE, t)
    nt = t // tm
    row = lambda width: pl.BlockSpec((1, tm, width), lambda b, i: (b, i, 0))
    vec = pl.BlockSpec((1, D_MODEL), lambda b, i: (0, 0))
    tab = pl.BlockSpec((tm, LANES), lambda b, i: (i, 0))
    in_specs = [row(D_MODEL), _mod_spec(shift, tm), _mod_spec(scale, tm), _mod_spec(kshift, tm),
                _mod_spec(kscale, tm), vec, vec,
                pl.BlockSpec((D_MODEL, 2 * B_WIDTH), lambda b, i: (0, 0)),
                pl.BlockSpec((D_MODEL, 2 * B_KV_WIDTH), lambda b, i: (0, 0)), tab, tab]
    args = (x, shift, scale, kshift, kscale, g, gkv, w, wkv, cos, sin)
    if sample:
        return pl.pallas_call(
            _inproj_b_sample_kernel, grid=(nb, nt), in_specs=in_specs,
            out_specs=[row(B_WIDTH), row(B_WIDTH), row(B_KV_WIDTH), row(B_KV_WIDTH)],
            out_shape=[jax.ShapeDtypeStruct((nb, t, B_WIDTH), F32), jax.ShapeDtypeStruct((nb, t, B_WIDTH), BF16),
                       jax.ShapeDtypeStruct((nb, t, B_KV_WIDTH), F32),
                       jax.ShapeDtypeStruct((nb, t, B_KV_WIDTH), F32)],
            compiler_params=_ARB(2), name="inproj_b_sample",
        )(*args)
    out_specs = [row(B_WIDTH), row(B_WIDTH), row(B_KV_WIDTH), row(B_KV_WIDTH),
                 pl.BlockSpec((1, 2 * B_KV_WIDTH, B_WINDOW), lambda b, i: (b, 0, 0))]
    out_shape = [jax.ShapeDtypeStruct((nb, t, B_WIDTH), BF16), jax.ShapeDtypeStruct((nb, t, B_WIDTH), BF16),
                 jax.ShapeDtypeStruct((nb, t, B_KV_WIDTH), BF16), jax.ShapeDtypeStruct((nb, t, B_KV_WIDTH), BF16),
                 jax.ShapeDtypeStruct((nb, 2 * B_KV_WIDTH, B_WINDOW), F32)]
    args, aliases = list(args), {}
    if job is not None:
        assert job.q.shape[0] == nb * nt * job.per_step
        j_in, j_args, j_out, j_shape, aliases = _job_operands(job, lambda b, i: b * nt + i, len(args),
                                                              len(out_specs))
        in_specs, args = in_specs + j_in, args + j_args
        out_specs, out_shape = out_specs + j_out, out_shape + j_shape
    return pl.pallas_call(
        functools.partial(_inproj_b_prompt_kernel, tm=tm, nt=nt, job=job.static if job else None),
        grid=(nb, nt), in_specs=in_specs, out_specs=out_specs, out_shape=out_shape,
        input_output_aliases=aliases, compiler_params=_ARB(2), name="inproj_b_prompt",
    )(*args)


def _eye():
    r = lax.broadcasted_iota(jnp.int32, (LANES, LANES), 0)
    c = lax.broadcasted_iota(jnp.int32, (LANES, LANES), 1)
    return r == c


def _row_to_col(row):
    eye = _eye()
    chunks = [jnp.sum(jnp.where(eye, row[:, j * LANES:(j + 1) * LANES], 0.0), axis=1, keepdims=True)
              for j in range(row.shape[1] // LANES)]
    return chunks[0] if len(chunks) == 1 else jnp.concatenate(chunks, axis=0)


def _shift_rows(cache_ref, out_ref, e, new_col, part=0, nparts=1):
    ntile = cache_ref.shape[2] // LANES
    first, last = part * ntile // nparts, (part + 1) * ntile // nparts
    if first == last:
        return
    lane = lax.broadcasted_iota(jnp.int32, (1, LANES), 1)
    nxt = pltpu.roll(cache_ref[e, :, first * LANES:(first + 1) * LANES], LANES - 1, 1)
    for j in range(first, last):
        cur = nxt
        if j + 1 < ntile:
            nxt = pltpu.roll(cache_ref[e, :, (j + 1) * LANES:(j + 2) * LANES], LANES - 1, 1)
            fill = nxt
        else:
            fill = new_col
        out_ref[e, :, j * LANES:(j + 1) * LANES] = jnp.where(lane < LANES - 1, cur, fill)


_NT = (((1,), (1,)), ((), ()))


def _sample_dilated_step(q_ref, kn_ref, vn_ref, cache_ref, out_cache_ref, o_ref, lse_ref, step, *, dil, bb,
                         shift_now=True):
    width, length = q_ref.shape[1], cache_ref.shape[2]
    new_cols = []
    sel = (lax.broadcasted_iota(jnp.int32, (8, width), 0)
           == lax.broadcasted_iota(jnp.int32, (8, width), 1) // HEAD_DIM)
    valid = lax.broadcasted_iota(jnp.int32, (1, length), 1) % dil == 0
    per_head = lambda x: jnp.sum(jnp.where(sel, x, 0.0), axis=0, keepdims=True)
    for e in range(bb):
        row = pl.ds(step * bb + e, 1)
        q, kn, vn = q_ref[row, :], kn_ref[row, :], vn_ref[row, :]
        qb = jnp.where(sel, q, 0.0)
        kt = cache_ref[e, 0:width, :].astype(BF16)
        vt = cache_ref[e, width:2 * width, :].astype(BF16)
        s = jnp.where(valid, _dot(qb.astype(BF16), kt), NEG)
        s_new = jnp.sum(qb * kn, axis=1, keepdims=True)
        m = jnp.maximum(jnp.max(s, axis=1, keepdims=True), s_new)
        p = jnp.exp2(s - m)
        p_new = jnp.exp2(s_new - m)
        l = per_head(jnp.sum(p, axis=1, keepdims=True) + p_new)
        o = lax.dot_general(p.astype(BF16), vt, _NT, preferred_element_type=F32)
        o_ref[row, :] = (per_head(o) + per_head(p_new) * vn) * (1.0 / l)
        lse_ref[row, :] = per_head(m) * LN2 + jnp.log(l)
        new_cols.append(jnp.concatenate([_row_to_col(kn), _row_to_col(vn)], axis=0))
        if shift_now:
            _shift_rows(cache_ref, out_cache_ref, e, new_cols[-1])
    return new_cols


def _sample_dilated_kernel(q_ref, kn_ref, vn_ref, cache_ref, out_cache_ref, o_ref, lse_ref, *, dil, bb):
    _sample_dilated_step(q_ref, kn_ref, vn_ref, cache_ref, out_cache_ref, o_ref, lse_ref, pl.program_id(0),
                         dil=dil, bb=bb)


class _ShiftJob(NamedTuple):
    q: jax.Array
    k_new: jax.Array
    v_new: jax.Array
    cache: jax.Array
    first: int
    per_step: int
    dil: int
    prev_out: Optional[jax.Array]

    @property
    def static(self):
        return (4 if self.prev_out is None else 5, self.dil, self.per_step)


def _job_operands(job, step_of, inputs_before, outputs_before):
    count, wq = job.q.shape
    _, chans, length = job.cache.shape
    first_blk = job.first // job.per_step
    full = pl.BlockSpec((count, wq), lambda *ids: (0, 0))
    tile = pl.BlockSpec((job.per_step, chans, length), lambda *ids: (first_blk + step_of(*ids), 0, 0))
    in_specs, args, aliases = [full, full, full, tile], [job.q, job.k_new, job.v_new, job.cache], {}
    if job.prev_out is not None:
        in_specs.append(pl.BlockSpec(memory_space=pl.ANY))
        args.append(job.prev_out)
        aliases[inputs_before + 4] = outputs_before
    out_shape = [jax.ShapeDtypeStruct(job.cache.shape, F32)] + [jax.ShapeDtypeStruct((count, wq), F32)] * 2
    return in_specs, args, [tile, full, full], out_shape, aliases


def _split_job_refs(rest, job_static, n_main_out):
    n_in = job_static[0] if job_static else 0
    n_out = 3 if job_static else 0
    job_in, rest = rest[:n_in], rest[n_in:]
    return job_in, rest[:n_main_out], rest[n_main_out:n_main_out + n_out], rest[n_main_out + n_out:]


def _run_job(job_static, in_refs, out_refs, step, shift_now=True):
    _, dil, per_step = job_static
    return _sample_dilated_step(*in_refs[:4], *out_refs, step, dil=dil, bb=per_step, shift_now=shift_now)


def _job_shift_part(job_static, in_refs, out_refs, new_cols, part, nparts):
    for e in range(job_static[2]):
        _shift_rows(in_refs[3], out_refs[0], e, new_cols[e], part, nparts)


def _sample_swa_kernel(q_ref, kn_ref, vn_ref, cache_ref, sink_ref, out_cache_ref, o_ref, qexp, oexp, *, bb):
    step = pl.program_id(0)
    nb, length = q_ref.shape[0], cache_ref.shape[2]
    low = lax.broadcasted_iota(jnp.int32, (1, LANES), 1) < HEAD_DIM

    @pl.when(step == 0)
    def _():
        for hq in range(B_Q_HEADS):
            slab = q_ref[:, (hq // 2) * LANES:(hq // 2 + 1) * LANES]
            src_low, dst_low = hq % 2 == 0, hq // B_GROUP == 0
            x = slab if src_low == dst_low else pltpu.roll(slab, HEAD_DIM, 1)
            qexp[hq * nb:(hq + 1) * nb, :] = jnp.where(low if dst_low else jnp.logical_not(low), x, 0.0)

    own_half = (lax.broadcasted_iota(jnp.int32, (B_Q_HEADS, LANES), 0) // B_GROUP
                == lax.broadcasted_iota(jnp.int32, (B_Q_HEADS, LANES), 1) // HEAD_DIM)
    valid = lax.broadcasted_iota(jnp.int32, (1, length), 1) >= 1
    sink = sink_ref[:, 0:1]
    for e in range(bb):
        b = step * bb + e
        heads = pl.ds(b, B_Q_HEADS, stride=nb)
        kn, vn = kn_ref[pl.ds(b, 1), :], vn_ref[pl.ds(b, 1), :]
        qb = qexp[heads, :]
        kt = cache_ref[e, 0:B_KV_WIDTH, :].astype(BF16)
        vt = cache_ref[e, B_KV_WIDTH:2 * B_KV_WIDTH, :].astype(BF16)
        s = jnp.where(valid, _dot(qb.astype(BF16), kt), NEG)
        s_new = jnp.sum(qb * kn, axis=1, keepdims=True)
        m = jnp.maximum(jnp.maximum(jnp.max(s, axis=1, keepdims=True), s_new), sink)
        p = jnp.exp2(s - m)
        p_new = jnp.exp2(s_new - m)
        l = jnp.sum(p, axis=1, keepdims=True) + p_new + jnp.exp2(sink - m)
        o = lax.dot_general(p.astype(BF16), vt, _NT, preferred_element_type=F32)
        oexp[heads, :] = jnp.where(own_half, o + p_new * vn, 0.0) * (1.0 / l)
        _shift_rows(cache_ref, out_cache_ref, e, jnp.concatenate([_row_to_col(kn), _row_to_col(vn)], axis=0))

    @pl.when(step == pl.num_programs(0) - 1)
    def _():
        for j in range(B_Q_HEADS // 2):
            even, odd = oexp[2 * j * nb:(2 * j + 1) * nb, :], oexp[(2 * j + 1) * nb:(2 * j + 2) * nb, :]
            if 2 * j // B_GROUP == 0:
                odd = pltpu.roll(odd, HEAD_DIM, 1)
            else:
                even = pltpu.roll(even, HEAD_DIM, 1)
            o_ref[:, j * LANES:(j + 1) * LANES] = jnp.where(low, even, odd)


def _sample_attn(q, k_new, v_new, cache_t, *, dil=1, sinks=None):
    nb, chans, length = cache_t.shape
    wq, wkv = q.shape[1], k_new.shape[1]
    bb = max(1, min(8, 2048 // length))
    full = lambda w: pl.BlockSpec((nb, w), lambda s: (0, 0))
    tile = pl.BlockSpec((bb, chans, length), lambda s: (s, 0, 0))
    in_specs = [full(wq), full(wkv), full(wkv), tile]
    args = [q, k_new, v_new, cache_t]
    out_specs = [tile, full(wq)]
    out_shape = [jax.ShapeDtypeStruct(cache_t.shape, F32), jax.ShapeDtypeStruct((nb, wq), F32)]
    if sinks is None:
        kern = functools.partial(_sample_dilated_kernel, dil=dil, bb=bb)
        out_specs.append(full(wq))
        out_shape.append(jax.ShapeDtypeStruct((nb, wq), F32))
        scratch = []
    else:
        kern = functools.partial(_sample_swa_kernel, bb=bb)
        in_specs.append(pl.BlockSpec(sinks.shape, lambda s: (0, 0)))
        args.append(sinks)
        scratch = [pltpu.VMEM((B_Q_HEADS * nb, LANES), F32)] * 2
    return pl.pallas_call(
        kern, grid=(nb // bb,), in_specs=in_specs, out_specs=out_specs, out_shape=out_shape,
        scratch_shapes=scratch, compiler_params=_ARB(1),
        name=f"sample_attn_d{dil}_l{length}" if sinks is None else "sample_swa",
    )(*args)


def _rope_tables(pos):
    half = HEAD_DIM // 2
    inv = ROPE_THETA ** (-jnp.arange(half, dtype=F32) / half)
    ang = pos.astype(F32)[:, None] * inv[None, :]
    cos, sin = jnp.cos(ang), jnp.sin(ang)
    return jnp.tile(cos, (1, 4)), jnp.tile(jnp.concatenate([-sin, sin], axis=1), (1, 2))


def _to_tiles(cache):
    nb, length = cache.shape[0], cache.shape[1]
    return jnp.transpose(cache, (0, 2, 3, 4, 1)).reshape(nb, -1, length)


def _from_tiles(tiles, heads):
    nb, _, length = tiles.shape
    return jnp.transpose(tiles.reshape(nb, 2, heads, HEAD_DIM, length), (0, 4, 1, 2, 3))


def kernel(x_prompt, x_sample, c_prompt, c_sample, cache_a_kv_g0, cache_a_kv_g1, cache_a_kv_g2, cache_b_kv, ada_w,
           ada_b, g_pre, g_post, w_in_a, w_o_a, w_in_b, w_o_b, sinks_b, ada_kv_w, ada_kv_b, g_kv, w_kv):
    nbp, t, _ = x_prompt.shape
    nbs = x_sample.shape[0]
    assert x_sample.shape[1] == 1

    c_all = jnp.concatenate([c_prompt, c_sample], axis=0)
    mod = _ada(c_all, ada_w, ada_b)
    mod_kv = _ada(c_all, ada_kv_w[None], ada_kv_b[None])[0]

    def split(a, parts, sample):
        rows = a[nbp:][None] if sample else a[:nbp][:, None]
        return [rows[..., p * D_MODEL:(p + 1) * D_MODEL] for p in range(parts)]

    w_in_a_b, w_o_a_b = w_in_a[0].astype(BF16), w_o_a[0].astype(BF16)
    w_in_b_b, w_o_b_b, w_kv_b = w_in_b[0].astype(BF16), w_o_b[0].astype(BF16), w_kv.astype(BF16)
    g_pre0, g_pre1 = g_pre[0:1], g_pre[1:2]
    g_post0, g_post1 = g_post[0:1], g_post[1:2]
    g_kv_r = g_kv[None]
    sinks = jnp.broadcast_to(sinks_b[0][:, None] * LOG2E, (B_Q_HEADS, LANES))

    xs = x_sample.reshape(1, nbs, D_MODEL)
    cos_s, sin_s = _rope_tables(jnp.full((nbs,), PAST_LEN, jnp.int32))
    sh0s, sc0s, gt0s = split(mod[0], 3, True)
    sh1s, sc1s, gt1s = split(mod[1], 3, True)
    kshs, kscs = split(mod_kv, 2, True)
    qs, ks, vs, gzs = _inproj_a(xs, sh0s, sc0s, g_pre0, w_in_a_b, cos_s, sin_s, sample=True)
    tiles = [_to_tiles(c[0]) for c in (cache_a_kv_g0, cache_a_kv_g1, cache_a_kv_g2)]

    def job(g, first, count, per_step, prev_out=None):
        rows, cols = slice(first, first + count), slice(g * GROUP_WIDTH, (g + 1) * GROUP_WIDTH)
        return _ShiftJob(qs[0, rows, cols], ks[0, rows, cols], vs[0, rows, cols], tiles[g], first, per_step,
                         A_GROUPS[g][1], prev_out)

    proj_steps = nbp * (t // min(TOKEN_TILE, t))

    cos_p, sin_p = _rope_tables(jnp.arange(t))
    sh0, sc0, gt0 = split(mod[0], 3, False)
    sh1, sc1, gt1 = split(mod[1], 3, False)
    ksh, ksc = split(mod_kv, 2, False)
    *qkv, gz, ca0, ca1, ca2, shifted2, o2a, lse2a = _inproj_a(
        x_prompt, sh0, sc0, g_pre0, w_in_a_b, cos_p, sin_p, sample=False, job=job(2, 0, proj_steps, 1))
    g1_steps = nbp * _dilated_steps(A_GROUPS[1][1])[1]
    attn = [_dilated_attn(*qkv[0:3], 0),
            _dilated_attn(*qkv[3:6], 1, job=job(0, 0, nbs, nbs // g1_steps)),
            _dilated_attn(*qkv[6:9], 2)]
    os, lses = [a[0] for a in attn], [a[1] for a in attn]
    shifted0, o0, lse0 = attn[1][2:]
    x1 = _outproj(os, lses, gz, x_prompt, gt0, g_post0, w_o_a_b)
    qb, gzb, kd, vd, cb, shifted1, o1, lse1 = _inproj_b(
        x1, sh1, sc1, ksh, ksc, g_pre1, g_kv_r, w_in_b_b, w_kv_b, cos_p, sin_p, sample=False,
        job=job(1, 0, nbs, nbs // proj_steps))
    ob, shifted2, o2b, lse2b = _swa_attn(qb, kd, vd, sinks,
                                         job=job(2, proj_steps, nbs - proj_steps, 1, prev_out=shifted2))
    y_prompt = _outproj([ob], None, gzb, x1, gt1, g_post1, w_o_b_b)
    new_a_prompt = [_from_tiles(c, A_HEADS_PER_GROUP)[None] for c in (ca0, ca1, ca2)]
    new_b_prompt = _from_tiles(cb, B_KV_HEADS)

    os = [o0, o1, jnp.concatenate([o2a, o2b], axis=0)]
    lses = [lse0, lse1, jnp.concatenate([lse2a, lse2b], axis=0)]
    os = [o.astype(BF16)[None, None] for o in os]
    lses = [l[None, None] for l in lses]
    new_a_sample = [_from_tiles(s, A_HEADS_PER_GROUP)[None] for s in (shifted0, shifted1, shifted2)]
    sh1, sc1, gt1, ksh, ksc, gz, gt0 = sh1s, sc1s, gt1s, kshs, kscs, gzs, gt0s
    xs1 = _outproj(os, lses, gz, xs, gt0, g_post0, w_o_a_b)
    qb, gzb, kb, vb = _inproj_b(xs1, sh1, sc1, ksh, ksc, g_pre1, g_kv_r, w_in_b_b, w_kv_b, cos_s, sin_s,
                                sample=True)
    shifted_b, ob = _sample_attn(qb[0], kb[0], vb[0], _to_tiles(cache_b_kv), sinks=sinks)
    y_sample = _outproj([ob.astype(BF16)[None]], None, gzb, xs1, gt1, g_post1, w_o_b_b).reshape(nbs, 1, D_MODEL)
    new_b_sample = _from_tiles(shifted_b, B_KV_HEADS)

    return (y_prompt, y_sample, *new_a_prompt, new_b_prompt, *new_a_sample, new_b_sample)
```

```python
import functools
from typing import NamedTuple, Optional

import jax
import jax.numpy as jnp
from jax import lax
from jax.experimental import pallas as pl
from jax.experimental.pallas import tpu as pltpu

D_MODEL = 1024
HEAD_DIM = 64
A_GROUPS = ((128, 1), (512, 4), (2048, 16))
A_HEADS_PER_GROUP = 4
GROUP_WIDTH = A_HEADS_PER_GROUP * HEAD_DIM
A_WIDTH = GROUP_WIDTH * len(A_GROUPS)
B_Q_HEADS = 16
B_KV_HEADS = 2
B_GROUP = B_Q_HEADS // B_KV_HEADS
B_WIDTH = B_Q_HEADS * HEAD_DIM
B_KV_WIDTH = B_KV_HEADS * HEAD_DIM
B_WINDOW = 128
BLOCK = 128
PAST_LEN = 16384
ROPE_THETA = 10000.0
EPS = 1e-6
NEG = -1e30
LANES = 128
LOG2E = 1.4426950408889634
LN2 = 0.6931471805599453
Q_SCALE = HEAD_DIM ** -0.5 * LOG2E

BF16 = jnp.bfloat16
F32 = jnp.float32
TOKEN_TILE = 512

_ARB = lambda n: pltpu.CompilerParams(dimension_semantics=("arbitrary",) * n)


HALF = HEAD_DIM // 2
PAIRED_QUARTERS = (0, 2, 1, 3)


def _paired_columns(w):
    d, n = w.shape
    return w.reshape(d, n // LANES, 4, HALF)[:, :, PAIRED_QUARTERS, :].reshape(d, n)


def _rope(x, cos, sin_signed):
    outs = []
    for j in range(x.shape[1] // LANES):
        xs = x[:, j * LANES:(j + 1) * LANES]
        outs.append(xs * cos + pltpu.roll(xs, HEAD_DIM, 1) * sin_signed)
    return outs[0] if len(outs) == 1 else jnp.concatenate(outs, axis=1)


def _standard_lanes(x):
    quarter = lax.broadcasted_iota(jnp.int32, (1, LANES), 1) // HALF
    outs = []
    for j in range(x.shape[1] // LANES):
        xs = x[:, j * LANES:(j + 1) * LANES]
        outs.append(jnp.where(quarter == 1, pltpu.roll(xs, LANES - HALF, 1),
                              jnp.where(quarter == 2, pltpu.roll(xs, HALF, 1), xs)))
    return outs[0] if len(outs) == 1 else jnp.concatenate(outs, axis=1)


def _store_standard_rows(ref, xt, dst):
    for s in range(xt.shape[0] // LANES):
        for quarter, src in enumerate(PAIRED_QUARTERS):
            row, src_row = s * LANES + quarter * HALF, s * LANES + src * HALF
            ref[0, row:row + HALF, dst] = xt[src_row:src_row + HALF]


def _normalize(x):
    return x * lax.rsqrt(jnp.mean(x * x, axis=-1, keepdims=True) + EPS)


def _dot(a, b):
    return jnp.dot(a, b, preferred_element_type=F32)


def _ada_kernel(c_ref, w_ref, b_ref, o_ref):
    s = jax.nn.silu(c_ref[...]).astype(BF16)
    o_ref[...] = _dot(s, w_ref[...].astype(BF16)) + b_ref[...]


def _ada(c, w, b):
    nl, _, n = w.shape
    m = c.shape[0]
    tn = 1024
    return pl.pallas_call(
        _ada_kernel,
        grid=(nl, n // tn),
        in_specs=[
            pl.BlockSpec((m, D_MODEL), lambda l, j: (0, 0)),
            pl.BlockSpec((None, D_MODEL, tn), lambda l, j: (l, 0, j)),
            pl.BlockSpec((None, 1, tn), lambda l, j: (l, 0, j)),
        ],
        out_specs=pl.BlockSpec((None, m, tn), lambda l, j: (l, 0, j)),
        out_shape=jax.ShapeDtypeStruct((nl, m, n), F32),
        compiler_params=_ARB(2),
        name="ada",
    )(c, w, b.reshape(nl, 1, n))


ROW_CHUNK = 512


def _row_chunks(tm):
    size = min(ROW_CHUNK, tm)
    return [slice(c * size, (c + 1) * size) for c in range(tm // size)]


def _mod_rows(ref, rows):
    return ref[0] if ref.shape[1] == 1 else ref[0, rows, :]


def _inproj_a_body(x_ref, sh_ref, sc_ref, g_ref, w_ref, cos_ref, sin_ref, rows):
    h = _normalize(x_ref[0, rows, :]) * g_ref[...]
    hb = (h * (1.0 + _mod_rows(sc_ref, rows)) + _mod_rows(sh_ref, rows)).astype(BF16)
    cos, sin = cos_ref[rows, :], sin_ref[rows, :]
    q = _rope(_dot(hb, w_ref[:, 0:A_WIDTH]), cos, sin) * Q_SCALE
    k = _rope(_dot(hb, w_ref[:, A_WIDTH:2 * A_WIDTH]), cos, sin)
    v = _dot(hb, w_ref[:, 2 * A_WIDTH:3 * A_WIDTH])
    gz = jax.nn.silu(_dot(hb, w_ref[:, 3 * A_WIDTH:4 * A_WIDTH]))
    return q, k, v, gz


def _inproj_a_prompt_kernel(x_ref, sh_ref, sc_ref, g_ref, w_ref, cos_ref, sin_ref, *rest, tm, nt, job):
    job_in, main_out, job_out, (scr,) = _split_job_refs(rest, job, 13)
    qkv_refs, (gz_ref, c0_ref, c1_ref, c2_ref) = main_out[:9], main_out[9:]
    i = pl.program_id(1)
    if job:
        _run_job(job, job_in, job_out, pl.program_id(0) * nt + i)
    slabs_per_group = GROUP_WIDTH // LANES
    for c, rows in enumerate(_row_chunks(tm)):
        n = rows.stop - rows.start
        q, k, v, gz = _inproj_a_body(x_ref, sh_ref, sc_ref, g_ref, w_ref, cos_ref, sin_ref, rows)
        gz_ref[0, rows, :] = gz.astype(BF16)
        for which, val in enumerate((q, k, v)):
            for j in range(A_WIDTH // LANES):
                scr[c, which, j] = val[:, j * LANES:(j + 1) * LANES]
            for g, (_, dil) in enumerate(A_GROUPS):
                out_ref = qkv_refs[3 * g + which]
                dst = slice(rows.start // dil, rows.stop // dil)
                for r in range(dil):
                    for h in range(slabs_per_group):
                        part = scr[c, which, g * slabs_per_group + h, pl.ds(r, n // dil, stride=dil), :]
                        out_ref[0, r, dst, h * LANES:(h + 1) * LANES] = part.astype(BF16)
        for g, c_ref in enumerate((c0_ref, c1_ref, c2_ref)):
            win = A_GROUPS[g][0]
            width = min(tm, win)
            first_tile = nt - max(win // tm, 1)
            lo = max(rows.start, tm - width)
            if lo >= rows.stop:
                continue

            @pl.when(i >= first_tile)
            def _(g=g, c_ref=c_ref, lo=lo, k=k, v=v, rows=rows, width=width):
                cols = slice(g * GROUP_WIDTH, (g + 1) * GROUP_WIDTH)
                dst = slice(lo - (tm - width), rows.stop - (tm - width))
                _store_standard_rows(c_ref, k[lo - rows.start:, cols].T, dst)
                c_ref[0, GROUP_WIDTH:2 * GROUP_WIDTH, dst] = v[lo - rows.start:, cols].T


def _inproj_a_sample_kernel(x_ref, sh_ref, sc_ref, g_ref, w_ref, cos_ref, sin_ref,
                            q_ref, k_ref, v_ref, gz_ref):
    q, k, v, gz = _inproj_a_body(x_ref, sh_ref, sc_ref, g_ref, w_ref, cos_ref, sin_ref, slice(0, x_ref.shape[1]))
    q_ref[0] = _standard_lanes(q)
    k_ref[0] = _standard_lanes(k)
    v_ref[0] = v
    gz_ref[0] = gz.astype(BF16)


def _mod_spec(mod, tm):
    if mod.shape[1] == 1:
        return pl.BlockSpec((1, 1, D_MODEL), lambda b, i: (b, 0, 0))
    return pl.BlockSpec((1, tm, D_MODEL), lambda b, i: (b, i, 0))


def _inproj_a(x, shift, scale, g, w, cos, sin, *, sample, job=None):
    nb, t, _ = x.shape
    tm = min(TOKEN_TILE, t)
    nt = t // tm
    row = lambda width: pl.BlockSpec((1, tm, width), lambda b, i: (b, i, 0))
    in_specs = [
        row(D_MODEL), _mod_spec(shift, tm), _mod_spec(scale, tm),
        pl.BlockSpec((1, D_MODEL), lambda b, i: (0, 0)),
        pl.BlockSpec((D_MODEL, 4 * A_WIDTH), lambda b, i: (0, 0)),
        pl.BlockSpec((tm, LANES), lambda b, i: (i, 0)),
        pl.BlockSpec((tm, LANES), lambda b, i: (i, 0)),
    ]
    if sample:
        out_shape = [jax.ShapeDtypeStruct((nb, t, A_WIDTH), dt) for dt in (F32, F32, F32, BF16)]
        return pl.pallas_call(
            _inproj_a_sample_kernel, grid=(nb, nt), in_specs=in_specs,
            out_specs=[row(A_WIDTH)] * 4, out_shape=out_shape,
            compiler_params=_ARB(2), name="inproj_a_sample",
        )(x, shift, scale, g, w, cos, sin)
    out_shape, out_specs = [], []
    for _, dil in A_GROUPS:
        out_shape += [jax.ShapeDtypeStruct((nb, dil, t // dil, GROUP_WIDTH), BF16)] * 3
        out_specs += [pl.BlockSpec((1, dil, tm // dil, GROUP_WIDTH), lambda b, i: (b, 0, i, 0))] * 3
    out_shape.append(jax.ShapeDtypeStruct((nb, t, A_WIDTH), BF16))
    out_specs.append(row(A_WIDTH))
    for win, _ in A_GROUPS:
        rows = min(tm, win)
        first_tile = nt - max(win // tm, 1)
        out_shape.append(jax.ShapeDtypeStruct((nb, 2 * GROUP_WIDTH, min(win, t)), F32))
        out_specs.append(pl.BlockSpec(
            (1, 2 * GROUP_WIDTH, rows),
            lambda b, i, first_tile=first_tile: (b, 0, jnp.maximum(i - first_tile, 0))))
    args, aliases = [x, shift, scale, g, w, cos, sin], {}
    if job is not None:
        assert job.q.shape[0] == nb * nt * job.per_step
        j_in, j_args, j_out, j_shape, aliases = _job_operands(job, lambda b, i: b * nt + i, len(args),
                                                              len(out_specs))
        in_specs, args = in_specs + j_in, args + j_args
        out_specs, out_shape = out_specs + j_out, out_shape + j_shape
    return pl.pallas_call(
        functools.partial(_inproj_a_prompt_kernel, tm=tm, nt=nt, job=job.static if job else None),
        grid=(nb, nt), in_specs=in_specs, out_specs=out_specs, out_shape=out_shape,
        scratch_shapes=[pltpu.VMEM((tm // min(ROW_CHUNK, tm), 3, A_WIDTH // LANES, min(ROW_CHUNK, tm), LANES), F32)],
        input_output_aliases=aliases, compiler_params=_ARB(2), name="inproj_a_prompt",
    )(*args)


def _query_minus_key():
    kj = lax.broadcasted_iota(jnp.int32, (2 * BLOCK, BLOCK), 0)
    qi = lax.broadcasted_iota(jnp.int32, (2 * BLOCK, BLOCK), 1)
    return qi - kj


def _head_attend(st, mask, vt_ext, sink):
    st = jnp.where(mask, st, NEG)
    m = jnp.max(st, axis=0, keepdims=True)
    if sink is not None:
        m = jnp.maximum(m, sink)
    ext = _dot(vt_ext, jnp.exp2(st - m).astype(BF16))
    l = ext[HEAD_DIM:HEAD_DIM + 1, :]
    if sink is not None:
        l = l + jnp.exp2(sink - m)
    return ext[0:HEAD_DIM, :] * (1.0 / l), m, l


def _values_ext(vt_prev, vt_cur, head):
    rows = slice(head * HEAD_DIM, (head + 1) * HEAD_DIM)
    ones = jnp.ones((16, 2 * BLOCK), BF16)
    return jnp.concatenate([jnp.concatenate([vt_prev[rows], vt_cur[rows]], axis=1), ones], axis=0)


def _block_diag_queries(qt, heads, kv_of_head, kv_heads):
    zeros = jnp.zeros((HALF, BLOCK), BF16)
    cols = []
    for h in range(heads):
        src = (h // 2) * LANES + (h % 2) * HALF
        lo, hi = qt[src:src + HALF], qt[src + HEAD_DIM:src + HEAD_DIM + HALF]
        kv = kv_of_head(h)
        pieces = []
        for slab in range(kv_heads // 2):
            for quarter in range(4):
                mine = slab == kv // 2 and quarter % 2 == kv % 2
                pieces.append((lo if quarter < 2 else hi) if mine else zeros)
        cols.append(jnp.concatenate(pieces, axis=0))
    return jnp.concatenate(cols, axis=1)


ATTN_BLOCKS_IN_FLIGHT = 8


def _dilated_attn_kernel(q_ref, k_ref, v_ref, *rest, job):
    job_in, (o_ref, lse_ref), job_out, (vt_ref,) = _split_job_refs(rest, job, 2)
    if job:
        _run_job(job, job_in, job_out, pl.program_id(0) * pl.num_programs(1) + pl.program_id(1))
    nres, m_rows = q_ref.shape[1], q_ref.shape[2]
    nblk = m_rows // BLOCK
    base = _query_minus_key()
    heads = A_HEADS_PER_GROUP
    for r in range(nres):
        def transpose_values(j, c, r=r):
            rows = pl.ds(pl.multiple_of(j * BLOCK, BLOCK), BLOCK)
            vt_ref[r, j] = v_ref[0, r, rows, :].T
            return c

        lax.fori_loop(0, nblk, transpose_values, 0)

    def block(r, i):
        prev = jnp.maximum(i - 1, 0)
        dist = base + (i - prev) * BLOCK
        mask = (dist >= 0) & (dist <= BLOCK)
        qrows = pl.ds(pl.multiple_of(i * BLOCK, BLOCK), BLOCK)
        k = k_ref[0, r, pl.ds(pl.multiple_of(prev * BLOCK, BLOCK), 2 * BLOCK), :]
        qd = _block_diag_queries(q_ref[0, r, qrows, :].T, heads, lambda h: h, heads)
        st_all = _dot(k, qd)
        vt_prev, vt_cur = vt_ref[r, prev], vt_ref[r, i]
        outs, lses = [], []
        for h in range(heads):
            o_t, m, l = _head_attend(st_all[:, h * BLOCK:(h + 1) * BLOCK], mask,
                                     _values_ext(vt_prev, vt_cur, h), None)
            outs.append(o_t)
            lses.append(jnp.broadcast_to(m * LN2 + jnp.log(l), (HEAD_DIM, BLOCK)))
        o_ref[0, r, qrows, :] = jnp.concatenate(outs, axis=0).T.astype(BF16)
        lse_ref[0, r, qrows, :] = jnp.concatenate(lses, axis=0).T

    if nres * nblk <= ATTN_BLOCKS_IN_FLIGHT:
        for r in range(nres):
            for i in range(nblk):
                block(r, i)
    else:
        assert nres == 1 and nblk % ATTN_BLOCKS_IN_FLIGHT == 0

        def chunk(c, carry):
            for u in range(ATTN_BLOCKS_IN_FLIGHT):
                block(0, c * ATTN_BLOCKS_IN_FLIGHT + u)
            return carry

        lax.fori_loop(0, nblk // ATTN_BLOCKS_IN_FLIGHT, chunk, 0)


def _dilated_steps(dil):
    nres = max(1, dil // 4)
    return nres, dil // nres


def _dilated_attn(q, k, v, g, job=None):
    nb, dil, m_rows, _ = q.shape
    nres, nsteps = _dilated_steps(dil)
    blk = pl.BlockSpec((1, nres, m_rows, GROUP_WIDTH), lambda b, r: (b, r, 0, 0))
    in_specs, args, aliases = [blk, blk, blk], [q, k, v], {}
    out_specs = [blk, blk]
    out_shape = [jax.ShapeDtypeStruct(q.shape, BF16), jax.ShapeDtypeStruct(q.shape, F32)]
    if job is not None:
        assert job.q.shape[0] == nb * nsteps * job.per_step
        j_in, j_args, j_out, j_shape, aliases = _job_operands(job, lambda b, r: b * nsteps + r, len(args), 2)
        in_specs, args = in_specs + j_in, args + j_args
        out_specs, out_shape = out_specs + j_out, out_shape + j_shape
    return pl.pallas_call(
        functools.partial(_dilated_attn_kernel, job=job.static if job else None),
        grid=(nb, nsteps), in_specs=in_specs, out_specs=out_specs, out_shape=out_shape,
        scratch_shapes=[pltpu.VMEM((nres, m_rows // BLOCK, GROUP_WIDTH, BLOCK), BF16)],
        input_output_aliases=aliases, compiler_params=_ARB(2), name=f"dilated_attn_g{g}",
    )(*args)


SWA_QUERY_TILE = 512


def _swa_attn_kernel(q_ref, k_ref, v_ref, sink_ref, gz_ref, x_ref, gate_ref, g_ref, w_ref, *rest, job):
    job_in, (y_ref,), job_out, (vt_ref, o_scr) = _split_job_refs(rest, job, 1)
    j = pl.program_id(1)
    seq = k_ref.shape[1]
    nq = q_ref.shape[1] // BLOCK
    base = _query_minus_key()
    if job:
        new_cols = _run_job(job, job_in, job_out, pl.program_id(0) * pl.num_programs(1) + j, shift_now=False)

    @pl.when(j == 0)
    def _():
        def transpose_values(t, c):
            vt_ref[t] = v_ref[0, pl.ds(pl.multiple_of(t * BLOCK, BLOCK), BLOCK), :].T
            return c

        lax.fori_loop(0, seq // BLOCK, transpose_values, 0)

    for i in range(nq):
        cur = j * nq + i
        prev = jnp.maximum(cur - 1, 0)
        dist = base + (cur - prev) * BLOCK
        mask = (dist >= 0) & (dist < B_WINDOW)
        qrows = slice(i * BLOCK, (i + 1) * BLOCK)
        k = k_ref[0, pl.ds(pl.multiple_of(prev * BLOCK, BLOCK), 2 * BLOCK), :]
        qt = q_ref[0, qrows, :].T
        vt_prev, vt_cur = vt_ref[prev], vt_ref[cur]
        outs = []
        for kvh in range(B_KV_HEADS):
            qd = _block_diag_queries(qt[kvh * B_GROUP * HEAD_DIM:(kvh + 1) * B_GROUP * HEAD_DIM], B_GROUP,
                                     lambda h, kvh=kvh: kvh, B_KV_HEADS)
            st_all = _dot(k, qd)
            vt_ext = _values_ext(vt_prev, vt_cur, kvh)
            for gq in range(B_GROUP):
                hq = kvh * B_GROUP + gq
                o_t, _, _ = _head_attend(st_all[:, gq * BLOCK:(gq + 1) * BLOCK], mask, vt_ext,
                                         sink_ref[hq:hq + 1, :])
                outs.append(o_t)
        o_scr[qrows, :] = jnp.concatenate(outs, axis=0).T.astype(BF16)
        if job:
            _job_shift_part(job, job_in, job_out, new_cols, i, nq)
    a = (o_scr[...].astype(F32) * gz_ref[0].astype(F32)).astype(BF16)
    _finish([a], w_ref, x_ref, gate_ref, g_ref, y_ref)


def _swa_attn(q, k, v, sinks, gz, x, gate, g, w, job=None):
    nb, t, _ = q.shape
    tq = min(SWA_QUERY_TILE, t)
    nj = t // tq
    qblk = pl.BlockSpec((1, tq, B_WIDTH), lambda b, j: (b, j, 0))
    xblk = pl.BlockSpec((1, tq, D_MODEL), lambda b, j: (b, j, 0))
    kvblk = pl.BlockSpec((1, t, B_KV_WIDTH), lambda b, j: (b, 0, 0))
    in_specs = [qblk, kvblk, kvblk, pl.BlockSpec((B_Q_HEADS, LANES), lambda b, j: (0, 0)),
                qblk, xblk, _mod_spec(gate, tq), pl.BlockSpec((1, D_MODEL), lambda b, j: (0, 0)),
                pl.BlockSpec((B_WIDTH, D_MODEL), lambda b, j: (0, 0))]
    args, aliases = [q, k, v, sinks, gz, x, gate, g, w], {}
    out_specs, out_shape = [xblk], [jax.ShapeDtypeStruct((nb, t, D_MODEL), F32)]
    if job is not None:
        assert job.q.shape[0] == nb * nj * job.per_step
        j_in, j_args, j_out, j_shape, aliases = _job_operands(job, lambda b, j: b * nj + j, len(args), 1)
        in_specs, args = in_specs + j_in, args + j_args
        out_specs, out_shape = out_specs + j_out, out_shape + j_shape
    return pl.pallas_call(
        functools.partial(_swa_attn_kernel, job=job.static if job else None),
        grid=(nb, nj), in_specs=in_specs, out_specs=out_specs, out_shape=out_shape,
        scratch_shapes=[pltpu.VMEM((t // BLOCK, B_KV_WIDTH, BLOCK), BF16), pltpu.VMEM((tq, B_WIDTH), BF16)],
        input_output_aliases=aliases, compiler_params=_ARB(2), name="swa_attn",
    )(*args)


def _finish(a_parts, w_ref, x_ref, gate_ref, g_ref, y_ref):
    acc = None
    row = 0
    for a in a_parts:
        part = _dot(a, w_ref[row:row + a.shape[1], :])
        acc = part if acc is None else acc + part
        row += a.shape[1]
    y_ref[0] = x_ref[0] + gate_ref[0] * (_normalize(acc) * g_ref[...])


def _in_row_order(ref, scr):
    dil = ref.shape[1]
    if dil == 1:
        return ref[0, 0].astype(F32)
    nslab = ref.shape[3] // LANES
    for r in range(dil):
        for h in range(nslab):
            scr[h, pl.ds(r, ref.shape[2], stride=dil), :] = ref[0, r, :, h * LANES:(h + 1) * LANES].astype(F32)
    return jnp.concatenate([scr[h] for h in range(nslab)], axis=1)


def _outproj_mix_kernel(o0_ref, o1_ref, o2_ref, l0_ref, l1_ref, l2_ref, gz_ref, x_ref, gate_ref, g_ref, w_ref,
                        y_ref, *scratch):
    os = [_in_row_order(r, s) for r, s in zip((o0_ref, o1_ref, o2_ref), scratch[0:3])]
    lses = [_in_row_order(r, s) for r, s in zip((l0_ref, l1_ref, l2_ref), scratch[3:6])]
    top = jnp.maximum(jnp.maximum(lses[0], lses[1]), lses[2])
    es = [jnp.exp(l - top) for l in lses]
    inv = 1.0 / (es[0] + es[1] + es[2])
    parts = []
    for g in range(len(A_GROUPS)):
        gz = gz_ref[0, :, g * GROUP_WIDTH:(g + 1) * GROUP_WIDTH].astype(F32)
        parts.append((os[g] * (es[g] * inv) * gz).astype(BF16))
    _finish(parts, w_ref, x_ref, gate_ref, g_ref, y_ref)


def _outproj_kernel(o_ref, gz_ref, x_ref, gate_ref, g_ref, w_ref, y_ref):
    a = (o_ref[0].astype(F32) * gz_ref[0].astype(F32)).astype(BF16)
    _finish([a], w_ref, x_ref, gate_ref, g_ref, y_ref)


def _outproj(os, lses, gz, x, gate, g, w):
    nb, t, _ = x.shape
    tm = min(TOKEN_TILE, t)
    row = lambda width: pl.BlockSpec((1, tm, width), lambda b, i: (b, i, 0))
    width = w.shape[0]
    tail_specs = [row(width), row(D_MODEL), _mod_spec(gate, tm),
                  pl.BlockSpec((1, D_MODEL), lambda b, i: (0, 0)),
                  pl.BlockSpec((width, D_MODEL), lambda b, i: (0, 0))]
    scratch = []
    if lses is None:
        kern, name = _outproj_kernel, "outproj_b"
        in_specs = [row(width)] + tail_specs
        args = (os[0], gz, x, gate, g, w)
    else:
        kern, name = _outproj_mix_kernel, "outproj_a"
        split = lambda a: pl.BlockSpec((1, a.shape[1], tm // a.shape[1], GROUP_WIDTH), lambda b, i: (b, 0, i, 0))
        in_specs = [split(a) for a in (*os, *lses)] + tail_specs
        args = (*os, *lses, gz, x, gate, g, w)
        scratch = [pltpu.VMEM((GROUP_WIDTH // LANES, tm, LANES), F32)] * 6
    return pl.pallas_call(
        kern, grid=(nb, t // tm), in_specs=in_specs, out_specs=row(D_MODEL),
        out_shape=jax.ShapeDtypeStruct((nb, t, D_MODEL), F32), scratch_shapes=scratch,
        compiler_params=_ARB(2), name=name,
    )(*args)


def _inproj_b_body(x_ref, sh_ref, sc_ref, ksh_ref, ksc_ref, g_ref, gkv_ref, w_ref, wkv_ref, cos_ref, sin_ref, rows):
    xn = _normalize(x_ref[0, rows, :])
    hb = ((xn * g_ref[...]) * (1.0 + _mod_rows(sc_ref, rows)) + _mod_rows(sh_ref, rows)).astype(BF16)
    hk = ((xn * gkv_ref[...]) * (1.0 + _mod_rows(ksc_ref, rows)) + _mod_rows(ksh_ref, rows)).astype(BF16)
    cos, sin = cos_ref[rows, :], sin_ref[rows, :]
    q = _rope(_dot(hb, w_ref[:, 0:B_WIDTH]), cos, sin) * Q_SCALE
    gz = jax.nn.silu(_dot(hb, w_ref[:, B_WIDTH:2 * B_WIDTH]))
    kv = _dot(hk, wkv_ref[...])
    k = _rope(kv[:, 0:B_KV_WIDTH], cos, sin)
    v = kv[:, B_KV_WIDTH:2 * B_KV_WIDTH]
    return q, gz, k, v


def _inproj_b_prompt_kernel(x_ref, sh_ref, sc_ref, ksh_ref, ksc_ref, g_ref, gkv_ref, w_ref, wkv_ref, cos_ref,
                            sin_ref, *rest, tm, nt, job):
    job_in, (q_ref, gz_ref, kd_ref, vd_ref, c_ref), job_out, _ = _split_job_refs(rest, job, 5)
    if job:
        _run_job(job, job_in, job_out, pl.program_id(0) * nt + pl.program_id(1))
    for rows in _row_chunks(tm):
        q, gz, k, v = _inproj_b_body(x_ref, sh_ref, sc_ref, ksh_ref, ksc_ref, g_ref, gkv_ref, w_ref, wkv_ref,
                                     cos_ref, sin_ref, rows)
        q_ref[0, rows, :] = q.astype(BF16)
        gz_ref[0, rows, :] = gz.astype(BF16)
        kd_ref[0, rows, :] = k.astype(BF16)
        vd_ref[0, rows, :] = v.astype(BF16)
        if rows.stop == tm:
            @pl.when(pl.program_id(1) == nt - 1)
            def _(k=k, v=v, n=rows.stop - rows.start):
                _store_standard_rows(c_ref, k[n - B_WINDOW:, :].T, slice(None))
                c_ref[0, B_KV_WIDTH:2 * B_KV_WIDTH, :] = v[n - B_WINDOW:, :].T


def _inproj_b_sample_kernel(x_ref, sh_ref, sc_ref, ksh_ref, ksc_ref, g_ref, gkv_ref, w_ref, wkv_ref, cos_ref,
                            sin_ref, q_ref, gz_ref, k_ref, v_ref):
    q, gz, k, v = _inproj_b_body(x_ref, sh_ref, sc_ref, ksh_ref, ksc_ref, g_ref, gkv_ref, w_ref, wkv_ref, cos_ref,
                                 sin_ref, slice(0, x_ref.shape[1]))
    q_ref[0] = _standard_lanes(q)
    gz_ref[0] = gz.astype(BF16)
    k_ref[0] = _standard_lanes(k)
    v_ref[0] = v


def _inproj_b(x, shift, scale, kshift, kscale, g, gkv, w, wkv, cos, sin, *, sample, job=None):
    nb, t, _ = x.shape
    tm = min(TOKEN_TILE, t)
    nt = t // tm
    row = lambda width: pl.BlockSpec((1, tm, width), lambda b, i: (b, i, 0))
    vec = pl.BlockSpec((1, D_MODEL), lambda b, i: (0, 0))
    tab = pl.BlockSpec((tm, LANES), lambda b, i: (i, 0))
    in_specs = [row(D_MODEL), _mod_spec(shift, tm), _mod_spec(scale, tm), _mod_spec(kshift, tm),
                _mod_spec(kscale, tm), vec, vec,
                pl.BlockSpec((D_MODEL, 2 * B_WIDTH), lambda b, i: (0, 0)),
                pl.BlockSpec((D_MODEL, 2 * B_KV_WIDTH), lambda b, i: (0, 0)), tab, tab]
    args = (x, shift, scale, kshift, kscale, g, gkv, w, wkv, cos, sin)
    if sample:
        return pl.pallas_call(
            _inproj_b_sample_kernel, grid=(nb, nt), in_specs=in_specs,
            out_specs=[row(B_WIDTH), row(B_WIDTH), row(B_KV_WIDTH), row(B_KV_WIDTH)],
            out_shape=[jax.ShapeDtypeStruct((nb, t, B_WIDTH), F32), jax.ShapeDtypeStruct((nb, t, B_WIDTH), BF16),
                       jax.ShapeDtypeStruct((nb, t, B_KV_WIDTH), F32),
                       jax.ShapeDtypeStruct((nb, t, B_KV_WIDTH), F32)],
            compiler_params=_ARB(2), name="inproj_b_sample",
        )(*args)
    out_specs = [row(B_WIDTH), row(B_WIDTH), row(B_KV_WIDTH), row(B_KV_WIDTH),
                 pl.BlockSpec((1, 2 * B_KV_WIDTH, B_WINDOW), lambda b, i: (b, 0, 0))]
    out_shape = [jax.ShapeDtypeStruct((nb, t, B_WIDTH), BF16), jax.ShapeDtypeStruct((nb, t, B_WIDTH), BF16),
                 jax.ShapeDtypeStruct((nb, t, B_KV_WIDTH), BF16), jax.ShapeDtypeStruct((nb, t, B_KV_WIDTH), BF16),
                 jax.ShapeDtypeStruct((nb, 2 * B_KV_WIDTH, B_WINDOW), F32)]
    args, aliases = list(args), {}
    if job is not None:
        assert job.q.shape[0] == nb * nt * job.per_step
        j_in, j_args, j_out, j_shape, aliases = _job_operands(job, lambda b, i: b * nt + i, len(args),
                                                              len(out_specs))
        in_specs, args = in_specs + j_in, args + j_args
        out_specs, out_shape = out_specs + j_out, out_shape + j_shape
    return pl.pallas_call(
        functools.partial(_inproj_b_prompt_kernel, tm=tm, nt=nt, job=job.static if job else None),
        grid=(nb, nt), in_specs=in_specs, out_specs=out_specs, out_shape=out_shape,
        input_output_aliases=aliases, compiler_params=_ARB(2), name="inproj_b_prompt",
    )(*args)


def _eye():
    r = lax.broadcasted_iota(jnp.int32, (LANES, LANES), 0)
    c = lax.broadcasted_iota(jnp.int32, (LANES, LANES), 1)
    return r == c


def _row_to_col(row):
    eye = _eye()
    chunks = [jnp.sum(jnp.where(eye, row[:, j * LANES:(j + 1) * LANES], 0.0), axis=1, keepdims=True)
              for j in range(row.shape[1] // LANES)]
    return chunks[0] if len(chunks) == 1 else jnp.concatenate(chunks, axis=0)


def _shift_rows(cache_ref, out_ref, e, new_col, part=0, nparts=1):
    ntile = cache_ref.shape[2] // LANES
    first, last = part * ntile // nparts, (part + 1) * ntile // nparts
    if first == last:
        return
    lane = lax.broadcasted_iota(jnp.int32, (1, LANES), 1)
    nxt = pltpu.roll(cache_ref[e, :, first * LANES:(first + 1) * LANES], LANES - 1, 1)
    for j in range(first, last):
        cur = nxt
        if j + 1 < ntile:
            nxt = pltpu.roll(cache_ref[e, :, (j + 1) * LANES:(j + 2) * LANES], LANES - 1, 1)
            fill = nxt
        else:
            fill = new_col
        out_ref[e, :, j * LANES:(j + 1) * LANES] = jnp.where(lane < LANES - 1, cur, fill)


_NT = (((1,), (1,)), ((), ()))


def _sample_dilated_step(q_ref, kn_ref, vn_ref, cache_ref, out_cache_ref, o_ref, lse_ref, step, *, dil, bb,
                         shift_now=True):
    width, length = q_ref.shape[1], cache_ref.shape[2]
    new_cols = []
    sel = (lax.broadcasted_iota(jnp.int32, (8, width), 0)
           == lax.broadcasted_iota(jnp.int32, (8, width), 1) // HEAD_DIM)
    valid = lax.broadcasted_iota(jnp.int32, (1, length), 1) % dil == 0
    per_head = lambda x: jnp.sum(jnp.where(sel, x, 0.0), axis=0, keepdims=True)
    for e in range(bb):
        row = pl.ds(step * bb + e, 1)
        q, kn, vn = q_ref[row, :], kn_ref[row, :], vn_ref[row, :]
        qb = jnp.where(sel, q, 0.0)
        kt = cache_ref[e, 0:width, :].astype(BF16)
        vt = cache_ref[e, width:2 * width, :].astype(BF16)
        s = jnp.where(valid, _dot(qb.astype(BF16), kt), NEG)
        s_new = jnp.sum(qb * kn, axis=1, keepdims=True)
        m = jnp.maximum(jnp.max(s, axis=1, keepdims=True), s_new)
        p = jnp.exp2(s - m)
        p_new = jnp.exp2(s_new - m)
        l = per_head(jnp.sum(p, axis=1, keepdims=True) + p_new)
        o = lax.dot_general(p.astype(BF16), vt, _NT, preferred_element_type=F32)
        o_ref[row, :] = (per_head(o) + per_head(p_new) * vn) * (1.0 / l)
        lse_ref[row, :] = per_head(m) * LN2 + jnp.log(l)
        new_cols.append(jnp.concatenate([_row_to_col(kn), _row_to_col(vn)], axis=0))
        if shift_now:
            _shift_rows(cache_ref, out_cache_ref, e, new_cols[-1])
    return new_cols


def _sample_dilated_kernel(q_ref, kn_ref, vn_ref, cache_ref, out_cache_ref, o_ref, lse_ref, *, dil, bb):
    _sample_dilated_step(q_ref, kn_ref, vn_ref, cache_ref, out_cache_ref, o_ref, lse_ref, pl.program_id(0),
                         dil=dil, bb=bb)


class _ShiftJob(NamedTuple):
    q: jax.Array
    k_new: jax.Array
    v_new: jax.Array
    cache: jax.Array
    first: int
    per_step: int
    dil: int
    prev_out: Optional[jax.Array]

    @property
    def static(self):
        return (4 if self.prev_out is None else 5, self.dil, self.per_step)


def _job_operands(job, step_of, inputs_before, outputs_before):
    count, wq = job.q.shape
    _, chans, length = job.cache.shape
    first_blk = job.first // job.per_step
    full = pl.BlockSpec((count, wq), lambda *ids: (0, 0))
    tile = pl.BlockSpec((job.per_step, chans, length), lambda *ids: (first_blk + step_of(*ids), 0, 0))
    in_specs, args, aliases = [full, full, full, tile], [job.q, job.k_new, job.v_new, job.cache], {}
    if job.prev_out is not None:
        in_specs.append(pl.BlockSpec(memory_space=pl.ANY))
        args.append(job.prev_out)
        aliases[inputs_before + 4] = outputs_before
    out_shape = [jax.ShapeDtypeStruct(job.cache.shape, F32)] + [jax.ShapeDtypeStruct((count, wq), F32)] * 2
    return in_specs, args, [tile, full, full], out_shape, aliases


def _split_job_refs(rest, job_static, n_main_out):
    n_in = job_static[0] if job_static else 0
    n_out = 3 if job_static else 0
    job_in, rest = rest[:n_in], rest[n_in:]
    return job_in, rest[:n_main_out], rest[n_main_out:n_main_out + n_out], rest[n_main_out + n_out:]


def _run_job(job_static, in_refs, out_refs, step, shift_now=True):
    _, dil, per_step = job_static
    return _sample_dilated_step(*in_refs[:4], *out_refs, step, dil=dil, bb=per_step, shift_now=shift_now)


def _job_shift_part(job_static, in_refs, out_refs, new_cols, part, nparts):
    for e in range(job_static[2]):
        _shift_rows(in_refs[3], out_refs[0], e, new_cols[e], part, nparts)


def _sample_swa_kernel(q_ref, kn_ref, vn_ref, cache_ref, sink_ref, out_cache_ref, o_ref, qexp, oexp, *, bb):
    step = pl.program_id(0)
    nb, length = q_ref.shape[0], cache_ref.shape[2]
    low = lax.broadcasted_iota(jnp.int32, (1, LANES), 1) < HEAD_DIM

    @pl.when(step == 0)
    def _():
        for hq in range(B_Q_HEADS):
            slab = q_ref[:, (hq // 2) * LANES:(hq // 2 + 1) * LANES]
            src_low, dst_low = hq % 2 == 0, hq // B_GROUP == 0
            x = slab if src_low == dst_low else pltpu.roll(slab, HEAD_DIM, 1)
            qexp[hq * nb:(hq + 1) * nb, :] = jnp.where(low if dst_low else jnp.logical_not(low), x, 0.0)

    own_half = (lax.broadcasted_iota(jnp.int32, (B_Q_HEADS, LANES), 0) // B_GROUP
                == lax.broadcasted_iota(jnp.int32, (B_Q_HEADS, LANES), 1) // HEAD_DIM)
    valid = lax.broadcasted_iota(jnp.int32, (1, length), 1) >= 1
    sink = sink_ref[:, 0:1]
    for e in range(bb):
        b = step * bb + e
        heads = pl.ds(b, B_Q_HEADS, stride=nb)
        kn, vn = kn_ref[pl.ds(b, 1), :], vn_ref[pl.ds(b, 1), :]
        qb = qexp[heads, :]
        kt = cache_ref[e, 0:B_KV_WIDTH, :].astype(BF16)
        vt = cache_ref[e, B_KV_WIDTH:2 * B_KV_WIDTH, :].astype(BF16)
        s = jnp.where(valid, _dot(qb.astype(BF16), kt), NEG)
        s_new = jnp.sum(qb * kn, axis=1, keepdims=True)
        m = jnp.maximum(jnp.maximum(jnp.max(s, axis=1, keepdims=True), s_new), sink)
        p = jnp.exp2(s - m)
        p_new = jnp.exp2(s_new - m)
        l = jnp.sum(p, axis=1, keepdims=True) + p_new + jnp.exp2(sink - m)
        o = lax.dot_general(p.astype(BF16), vt, _NT, preferred_element_type=F32)
        oexp[heads, :] = jnp.where(own_half, o + p_new * vn, 0.0) * (1.0 / l)
        _shift_rows(cache_ref, out_cache_ref, e, jnp.concatenate([_row_to_col(kn), _row_to_col(vn)], axis=0))

    @pl.when(step == pl.num_programs(0) - 1)
    def _():
        for j in range(B_Q_HEADS // 2):
            even, odd = oexp[2 * j * nb:(2 * j + 1) * nb, :], oexp[(2 * j + 1) * nb:(2 * j + 2) * nb, :]
            if 2 * j // B_GROUP == 0:
                odd = pltpu.roll(odd, HEAD_DIM, 1)
            else:
                even = pltpu.roll(even, HEAD_DIM, 1)
            o_ref[:, j * LANES:(j + 1) * LANES] = jnp.where(low, even, odd)


def _sample_attn(q, k_new, v_new, cache_t, *, dil=1, sinks=None):
    nb, chans, length = cache_t.shape
    wq, wkv = q.shape[1], k_new.shape[1]
    bb = max(1, min(8, 2048 // length))
    full = lambda w: pl.BlockSpec((nb, w), lambda s: (0, 0))
    tile = pl.BlockSpec((bb, chans, length), lambda s: (s, 0, 0))
    in_specs = [full(wq), full(wkv), full(wkv), tile]
    args = [q, k_new, v_new, cache_t]
    out_specs = [tile, full(wq)]
    out_shape = [jax.ShapeDtypeStruct(cache_t.shape, F32), jax.ShapeDtypeStruct((nb, wq), F32)]
    if sinks is None:
        kern = functools.partial(_sample_dilated_kernel, dil=dil, bb=bb)
        out_specs.append(full(wq))
        out_shape.append(jax.ShapeDtypeStruct((nb, wq), F32))
        scratch = []
    else:
        kern = functools.partial(_sample_swa_kernel, bb=bb)
        in_specs.append(pl.BlockSpec(sinks.shape, lambda s: (0, 0)))
        args.append(sinks)
        scratch = [pltpu.VMEM((B_Q_HEADS * nb, LANES), F32)] * 2
    return pl.pallas_call(
        kern, grid=(nb // bb,), in_specs=in_specs, out_specs=out_specs, out_shape=out_shape,
        scratch_shapes=scratch, compiler_params=_ARB(1),
        name=f"sample_attn_d{dil}_l{length}" if sinks is None else "sample_swa",
    )(*args)


def _rope_tables(pos):
    half = HEAD_DIM // 2
    inv = ROPE_THETA ** (-jnp.arange(half, dtype=F32) / half)
    ang = pos.astype(F32)[:, None] * inv[None, :]
    cos, sin = jnp.cos(ang), jnp.sin(ang)
    return jnp.tile(cos, (1, 4)), jnp.concatenate([-sin, -sin, sin, sin], axis=1)


def _to_tiles(cache):
    nb, length = cache.shape[0], cache.shape[1]
    return jnp.transpose(cache, (0, 2, 3, 4, 1)).reshape(nb, -1, length)


def _from_tiles(tiles, heads):
    nb, _, length = tiles.shape
    return jnp.transpose(tiles.reshape(nb, 2, heads, HEAD_DIM, length), (0, 4, 1, 2, 3))


def kernel(x_prompt, x_sample, c_prompt, c_sample, cache_a_kv_g0, cache_a_kv_g1, cache_a_kv_g2, cache_b_kv, ada_w,
           ada_b, g_pre, g_post, w_in_a, w_o_a, w_in_b, w_o_b, sinks_b, ada_kv_w, ada_kv_b, g_kv, w_kv):
    nbp, t, _ = x_prompt.shape
    nbs = x_sample.shape[0]
    assert x_sample.shape[1] == 1

    c_all = jnp.concatenate([c_prompt, c_sample], axis=0)
    mod = _ada(c_all, ada_w, ada_b)
    mod_kv = _ada(c_all, ada_kv_w[None], ada_kv_b[None])[0]

    def split(a, parts, sample):
        rows = a[nbp:][None] if sample else a[:nbp][:, None]
        return [rows[..., p * D_MODEL:(p + 1) * D_MODEL] for p in range(parts)]

    def paired(w, ncols):
        return jnp.concatenate([_paired_columns(w[:, :ncols]), w[:, ncols:]], axis=1).astype(BF16)

    w_in_a_b, w_o_a_b = paired(w_in_a[0], 2 * A_WIDTH), w_o_a[0].astype(BF16)
    w_in_b_b, w_o_b_b, w_kv_b = paired(w_in_b[0], B_WIDTH), w_o_b[0].astype(BF16), paired(w_kv, B_KV_WIDTH)
    g_pre0, g_pre1 = g_pre[0:1], g_pre[1:2]
    g_post0, g_post1 = g_post[0:1], g_post[1:2]
    g_kv_r = g_kv[None]
    sinks = jnp.broadcast_to(sinks_b[0][:, None] * LOG2E, (B_Q_HEADS, LANES))

    xs = x_sample.reshape(1, nbs, D_MODEL)
    cos_s, sin_s = _rope_tables(jnp.full((nbs,), PAST_LEN, jnp.int32))
    sh0s, sc0s, gt0s = split(mod[0], 3, True)
    sh1s, sc1s, gt1s = split(mod[1], 3, True)
    kshs, kscs = split(mod_kv, 2, True)
    qs, ks, vs, gzs = _inproj_a(xs, sh0s, sc0s, g_pre0, w_in_a_b, cos_s, sin_s, sample=True)
    tiles = [_to_tiles(c[0]) for c in (cache_a_kv_g0, cache_a_kv_g1, cache_a_kv_g2)]

    def job(g, first, count, per_step, prev_out=None):
        rows, cols = slice(first, first + count), slice(g * GROUP_WIDTH, (g + 1) * GROUP_WIDTH)
        return _ShiftJob(qs[0, rows, cols], ks[0, rows, cols], vs[0, rows, cols], tiles[g], first, per_step,
                         A_GROUPS[g][1], prev_out)

    proj_steps = nbp * (t // min(TOKEN_TILE, t))

    cos_p, sin_p = _rope_tables(jnp.arange(t))
    sh0, sc0, gt0 = split(mod[0], 3, False)
    sh1, sc1, gt1 = split(mod[1], 3, False)
    ksh, ksc = split(mod_kv, 2, False)
    *qkv, gz, ca0, ca1, ca2, shifted2, o2a, lse2a = _inproj_a(
        x_prompt, sh0, sc0, g_pre0, w_in_a_b, cos_p, sin_p, sample=False, job=job(2, 0, proj_steps, 1))
    g1_steps = nbp * _dilated_steps(A_GROUPS[1][1])[1]
    attn = [_dilated_attn(*qkv[0:3], 0),
            _dilated_attn(*qkv[3:6], 1, job=job(0, 0, nbs, nbs // g1_steps)),
            _dilated_attn(*qkv[6:9], 2)]
    os, lses = [a[0] for a in attn], [a[1] for a in attn]
    shifted0, o0, lse0 = attn[1][2:]
    x1 = _outproj(os, lses, gz, x_prompt, gt0, g_post0, w_o_a_b)
    qb, gzb, kd, vd, cb, shifted1, o1, lse1 = _inproj_b(
        x1, sh1, sc1, ksh, ksc, g_pre1, g_kv_r, w_in_b_b, w_kv_b, cos_p, sin_p, sample=False,
        job=job(1, 0, nbs, nbs // proj_steps))
    y_prompt, shifted2, o2b, lse2b = _swa_attn(qb, kd, vd, sinks, gzb, x1, gt1, g_post1, w_o_b_b,
                                               job=job(2, proj_steps, nbs - proj_steps, 1, prev_out=shifted2))
    new_a_prompt = [_from_tiles(c, A_HEADS_PER_GROUP)[None] for c in (ca0, ca1, ca2)]
    new_b_prompt = _from_tiles(cb, B_KV_HEADS)

    os = [o0, o1, jnp.concatenate([o2a, o2b], axis=0)]
    lses = [lse0, lse1, jnp.concatenate([lse2a, lse2b], axis=0)]
    os = [o.astype(BF16)[None, None] for o in os]
    lses = [l[None, None] for l in lses]
    new_a_sample = [_from_tiles(s, A_HEADS_PER_GROUP)[None] for s in (shifted0, shifted1, shifted2)]
    sh1, sc1, gt1, ksh, ksc, gz, gt0 = sh1s, sc1s, gt1s, kshs, kscs, gzs, gt0s
    xs1 = _outproj(os, lses, gz, xs, gt0, g_post0, w_o_a_b)
    qb, gzb, kb, vb = _inproj_b(xs1, sh1, sc1, ksh, ksc, g_pre1, g_kv_r, w_in_b_b, w_kv_b, cos_s, sin_s,
                                sample=True)
    shifted_b, ob = _sample_attn(qb[0], kb[0], vb[0], _to_tiles(cache_b_kv), sinks=sinks)
    y_sample = _outproj([ob.astype(BF16)[None]], None, gzb, xs1, gt1, g_post1, w_o_b_b).reshape(nbs, 1, D_MODEL)
    new_b_sample = _from_tiles(shifted_b, B_KV_HEADS)

    return (y_prompt, y_sample, *new_a_prompt, new_b_prompt, *new_a_sample, new_b_sample)
```

```python
import functools
from typing import NamedTuple, Optional

import jax
import jax.numpy as jnp
from jax import lax
from jax.experimental import pallas as pl
from jax.experimental.pallas import tpu as pltpu

D_MODEL = 1024
HEAD_DIM = 64
A_GROUPS = ((128, 1), (512, 4), (2048, 16))
A_HEADS_PER_GROUP = 4
GROUP_WIDTH = A_HEADS_PER_GROUP * HEAD_DIM
A_WIDTH = GROUP_WIDTH * len(A_GROUPS)
B_Q_HEADS = 16
B_KV_HEADS = 2
B_GROUP = B_Q_HEADS // B_KV_HEADS
B_WIDTH = B_Q_HEADS * HEAD_DIM
B_KV_WIDTH = B_KV_HEADS * HEAD_DIM
B_WINDOW = 128
BLOCK = 128
PAST_LEN = 16384
ROPE_THETA = 10000.0
EPS = 1e-6
NEG = -1e30
LANES = 128
LOG2E = 1.4426950408889634
LN2 = 0.6931471805599453
Q_SCALE = HEAD_DIM ** -0.5 * LOG2E

BF16 = jnp.bfloat16
F32 = jnp.float32
TOKEN_TILE = 512

_ARB = lambda n: pltpu.CompilerParams(dimension_semantics=("arbitrary",) * n)


HALF = HEAD_DIM // 2
PAIRED_QUARTERS = (0, 2, 1, 3)


def _paired_columns(w):
    d, n = w.shape
    return w.reshape(d, n // LANES, 4, HALF)[:, :, PAIRED_QUARTERS, :].reshape(d, n)


def _rope(x, cos, sin_signed):
    outs = []
    for j in range(x.shape[1] // LANES):
        xs = x[:, j * LANES:(j + 1) * LANES]
        outs.append(xs * cos + pltpu.roll(xs, HEAD_DIM, 1) * sin_signed)
    return outs[0] if len(outs) == 1 else jnp.concatenate(outs, axis=1)


def _standard_lanes(x):
    quarter = lax.broadcasted_iota(jnp.int32, (1, LANES), 1) // HALF
    outs = []
    for j in range(x.shape[1] // LANES):
        xs = x[:, j * LANES:(j + 1) * LANES]
        outs.append(jnp.where(quarter == 1, pltpu.roll(xs, LANES - HALF, 1),
                              jnp.where(quarter == 2, pltpu.roll(xs, HALF, 1), xs)))
    return outs[0] if len(outs) == 1 else jnp.concatenate(outs, axis=1)


def _store_standard_rows(ref, xt, dst):
    for s in range(xt.shape[0] // LANES):
        for quarter, src in enumerate(PAIRED_QUARTERS):
            row, src_row = s * LANES + quarter * HALF, s * LANES + src * HALF
            ref[0, row:row + HALF, dst] = xt[src_row:src_row + HALF]


def _normalize(x):
    return x * lax.rsqrt(jnp.mean(x * x, axis=-1, keepdims=True) + EPS)


def _dot(a, b):
    return jnp.dot(a, b, preferred_element_type=F32)


def _ada_kernel(c_ref, w_ref, b_ref, o_ref):
    s = jax.nn.silu(c_ref[...]).astype(BF16)
    o_ref[...] = _dot(s, w_ref[...].astype(BF16)) + b_ref[...]


def _ada(c, w, b):
    nl, _, n = w.shape
    m = c.shape[0]
    tn = 1024
    return pl.pallas_call(
        _ada_kernel,
        grid=(nl, n // tn),
        in_specs=[
            pl.BlockSpec((m, D_MODEL), lambda l, j: (0, 0)),
            pl.BlockSpec((None, D_MODEL, tn), lambda l, j: (l, 0, j)),
            pl.BlockSpec((None, 1, tn), lambda l, j: (l, 0, j)),
        ],
        out_specs=pl.BlockSpec((None, m, tn), lambda l, j: (l, 0, j)),
        out_shape=jax.ShapeDtypeStruct((nl, m, n), F32),
        compiler_params=_ARB(2),
        name="ada",
    )(c, w, b.reshape(nl, 1, n))


ROW_CHUNK = 512


def _row_chunks(tm):
    size = min(ROW_CHUNK, tm)
    return [slice(c * size, (c + 1) * size) for c in range(tm // size)]


def _mod_rows(ref, rows):
    return ref[0] if ref.shape[1] == 1 else ref[0, rows, :]


def _inproj_a_body(x_ref, sh_ref, sc_ref, g_ref, w_ref, cos_ref, sin_ref, rows):
    h = _normalize(x_ref[0, rows, :]) * g_ref[...]
    hb = (h * (1.0 + _mod_rows(sc_ref, rows)) + _mod_rows(sh_ref, rows)).astype(BF16)
    cos, sin = cos_ref[rows, :], sin_ref[rows, :]
    q = _rope(_dot(hb, w_ref[:, 0:A_WIDTH]), cos, sin) * Q_SCALE
    k = _rope(_dot(hb, w_ref[:, A_WIDTH:2 * A_WIDTH]), cos, sin)
    v = _dot(hb, w_ref[:, 2 * A_WIDTH:3 * A_WIDTH])
    gz = jax.nn.silu(_dot(hb, w_ref[:, 3 * A_WIDTH:4 * A_WIDTH]))
    return q, k, v, gz


def _inproj_a_prompt_kernel(x_ref, sh_ref, sc_ref, g_ref, w_ref, cos_ref, sin_ref, *rest, tm, nt, job):
    job_in, main_out, job_out, (scr,) = _split_job_refs(rest, job, 13)
    qkv_refs, (gz_ref, c0_ref, c1_ref, c2_ref) = main_out[:9], main_out[9:]
    i = pl.program_id(1)
    if job:
        _run_job(job, job_in, job_out, pl.program_id(0) * nt + i)
    slabs_per_group = GROUP_WIDTH // LANES
    for c, rows in enumerate(_row_chunks(tm)):
        n = rows.stop - rows.start
        q, k, v, gz = _inproj_a_body(x_ref, sh_ref, sc_ref, g_ref, w_ref, cos_ref, sin_ref, rows)
        gz_ref[0, rows, :] = gz.astype(BF16)
        for which, val in enumerate((q, k, v)):
            for j in range(A_WIDTH // LANES):
                scr[c, which, j] = val[:, j * LANES:(j + 1) * LANES]
            for g, (_, dil) in enumerate(A_GROUPS):
                out_ref = qkv_refs[3 * g + which]
                dst = slice(rows.start // dil, rows.stop // dil)
                for r in range(dil):
                    for h in range(slabs_per_group):
                        part = scr[c, which, g * slabs_per_group + h, pl.ds(r, n // dil, stride=dil), :]
                        out_ref[0, r, dst, h * LANES:(h + 1) * LANES] = part.astype(BF16)
        for g, c_ref in enumerate((c0_ref, c1_ref, c2_ref)):
            win = A_GROUPS[g][0]
            width = min(tm, win)
            first_tile = nt - max(win // tm, 1)
            lo = max(rows.start, tm - width)
            if lo >= rows.stop:
                continue

            @pl.when(i >= first_tile)
            def _(g=g, c_ref=c_ref, lo=lo, k=k, v=v, rows=rows, width=width):
                cols = slice(g * GROUP_WIDTH, (g + 1) * GROUP_WIDTH)
                dst = slice(lo - (tm - width), rows.stop - (tm - width))
                _store_standard_rows(c_ref, k[lo - rows.start:, cols].T, dst)
                c_ref[0, GROUP_WIDTH:2 * GROUP_WIDTH, dst] = v[lo - rows.start:, cols].T


def _inproj_a_sample_kernel(x_ref, sh_ref, sc_ref, g_ref, w_ref, cos_ref, sin_ref,
                            q_ref, k_ref, v_ref, gz_ref):
    q, k, v, gz = _inproj_a_body(x_ref, sh_ref, sc_ref, g_ref, w_ref, cos_ref, sin_ref, slice(0, x_ref.shape[1]))
    q_ref[0] = _standard_lanes(q)
    k_ref[0] = _standard_lanes(k)
    v_ref[0] = v
    gz_ref[0] = gz.astype(BF16)


def _mod_spec(mod, tm):
    if mod.shape[1] == 1:
        return pl.BlockSpec((1, 1, D_MODEL), lambda b, i: (b, 0, 0))
    return pl.BlockSpec((1, tm, D_MODEL), lambda b, i: (b, i, 0))


def _inproj_a(x, shift, scale, g, w, cos, sin, *, sample, job=None):
    nb, t, _ = x.shape
    tm = min(TOKEN_TILE, t)
    nt = t // tm
    row = lambda width: pl.BlockSpec((1, tm, width), lambda b, i: (b, i, 0))
    in_specs = [
        row(D_MODEL), _mod_spec(shift, tm), _mod_spec(scale, tm),
        pl.BlockSpec((1, D_MODEL), lambda b, i: (0, 0)),
        pl.BlockSpec((D_MODEL, 4 * A_WIDTH), lambda b, i: (0, 0)),
        pl.BlockSpec((tm, LANES), lambda b, i: (i, 0)),
        pl.BlockSpec((tm, LANES), lambda b, i: (i, 0)),
    ]
    if sample:
        out_shape = [jax.ShapeDtypeStruct((nb, t, A_WIDTH), dt) for dt in (F32, F32, F32, BF16)]
        return pl.pallas_call(
            _inproj_a_sample_kernel, grid=(nb, nt), in_specs=in_specs,
            out_specs=[row(A_WIDTH)] * 4, out_shape=out_shape,
            compiler_params=_ARB(2), name="inproj_a_sample",
        )(x, shift, scale, g, w, cos, sin)
    out_shape, out_specs = [], []
    for _, dil in A_GROUPS:
        out_shape += [jax.ShapeDtypeStruct((nb, dil, t // dil, GROUP_WIDTH), BF16)] * 3
        out_specs += [pl.BlockSpec((1, dil, tm // dil, GROUP_WIDTH), lambda b, i: (b, 0, i, 0))] * 3
    out_shape.append(jax.ShapeDtypeStruct((nb, t, A_WIDTH), BF16))
    out_specs.append(row(A_WIDTH))
    for win, _ in A_GROUPS:
        rows = min(tm, win)
        first_tile = nt - max(win // tm, 1)
        out_shape.append(jax.ShapeDtypeStruct((nb, 2 * GROUP_WIDTH, min(win, t)), F32))
        out_specs.append(pl.BlockSpec(
            (1, 2 * GROUP_WIDTH, rows),
            lambda b, i, first_tile=first_tile: (b, 0, jnp.maximum(i - first_tile, 0))))
    args, aliases = [x, shift, scale, g, w, cos, sin], {}
    if job is not None:
        assert job.q.shape[0] == nb * nt * job.per_step
        j_in, j_args, j_out, j_shape, aliases = _job_operands(job, lambda b, i: b * nt + i, len(args),
                                                              len(out_specs))
        in_specs, args = in_specs + j_in, args + j_args
        out_specs, out_shape = out_specs + j_out, out_shape + j_shape
    return pl.pallas_call(
        functools.partial(_inproj_a_prompt_kernel, tm=tm, nt=nt, job=job.static if job else None),
        grid=(nb, nt), in_specs=in_specs, out_specs=out_specs, out_shape=out_shape,
        scratch_shapes=[pltpu.VMEM((tm // min(ROW_CHUNK, tm), 3, A_WIDTH // LANES, min(ROW_CHUNK, tm), LANES), F32)],
        input_output_aliases=aliases, compiler_params=_ARB(2), name="inproj_a_prompt",
    )(*args)


def _query_minus_key():
    kj = lax.broadcasted_iota(jnp.int32, (2 * BLOCK, BLOCK), 0)
    qi = lax.broadcasted_iota(jnp.int32, (2 * BLOCK, BLOCK), 1)
    return qi - kj


def _head_attend(st, mask, vt_ext, sink):
    st = jnp.where(mask, st, NEG)
    m = jnp.max(st, axis=0, keepdims=True)
    if sink is not None:
        m = jnp.maximum(m, sink)
    ext = _dot(vt_ext, jnp.exp2(st - m).astype(BF16))
    l = ext[HEAD_DIM:HEAD_DIM + 1, :]
    if sink is not None:
        l = l + jnp.exp2(sink - m)
    return ext[0:HEAD_DIM, :] * (1.0 / l), m, l


def _values_ext(vt_prev, vt_cur, head):
    rows = slice(head * HEAD_DIM, (head + 1) * HEAD_DIM)
    ones = jnp.ones((16, 2 * BLOCK), BF16)
    return jnp.concatenate([jnp.concatenate([vt_prev[rows], vt_cur[rows]], axis=1), ones], axis=0)


def _block_diag_queries(qt, heads, kv_of_head, kv_heads):
    zeros = jnp.zeros((HALF, BLOCK), BF16)
    cols = []
    for h in range(heads):
        src = (h // 2) * LANES + (h % 2) * HALF
        lo, hi = qt[src:src + HALF], qt[src + HEAD_DIM:src + HEAD_DIM + HALF]
        kv = kv_of_head(h)
        pieces = []
        for slab in range(kv_heads // 2):
            for quarter in range(4):
                mine = slab == kv // 2 and quarter % 2 == kv % 2
                pieces.append((lo if quarter < 2 else hi) if mine else zeros)
        cols.append(jnp.concatenate(pieces, axis=0))
    return jnp.concatenate(cols, axis=1)


ATTN_BLOCKS_IN_FLIGHT = 8


def _dilated_attn_kernel(q_ref, k_ref, v_ref, *rest, job):
    job_in, (o_ref, lse_ref), job_out, (vt_ref,) = _split_job_refs(rest, job, 2)
    if job:
        _run_job(job, job_in, job_out, pl.program_id(0) * pl.num_programs(1) + pl.program_id(1))
    nres, m_rows = q_ref.shape[1], q_ref.shape[2]
    nblk = m_rows // BLOCK
    base = _query_minus_key()
    heads = A_HEADS_PER_GROUP
    for r in range(nres):
        def transpose_values(j, c, r=r):
            rows = pl.ds(pl.multiple_of(j * BLOCK, BLOCK), BLOCK)
            vt_ref[r, j] = v_ref[0, r, rows, :].T
            return c

        lax.fori_loop(0, nblk, transpose_values, 0)

    def block(r, i):
        prev = jnp.maximum(i - 1, 0)
        dist = base + (i - prev) * BLOCK
        mask = (dist >= 0) & (dist <= BLOCK)
        qrows = pl.ds(pl.multiple_of(i * BLOCK, BLOCK), BLOCK)
        k = k_ref[0, r, pl.ds(pl.multiple_of(prev * BLOCK, BLOCK), 2 * BLOCK), :]
        qd = _block_diag_queries(q_ref[0, r, qrows, :].T, heads, lambda h: h, heads)
        st_all = _dot(k, qd)
        vt_prev, vt_cur = vt_ref[r, prev], vt_ref[r, i]
        outs, lses = [], []
        for h in range(heads):
            o_t, m, l = _head_attend(st_all[:, h * BLOCK:(h + 1) * BLOCK], mask,
                                     _values_ext(vt_prev, vt_cur, h), None)
            outs.append(o_t)
            lses.append(jnp.broadcast_to(m * LN2 + jnp.log(l), (HEAD_DIM, BLOCK)))
        o_ref[0, r, qrows, :] = jnp.concatenate(outs, axis=0).T.astype(BF16)
        lse_ref[0, r, qrows, :] = jnp.concatenate(lses, axis=0).T

    if nres * nblk <= ATTN_BLOCKS_IN_FLIGHT:
        for r in range(nres):
            for i in range(nblk):
                block(r, i)
    else:
        assert nres == 1 and nblk % ATTN_BLOCKS_IN_FLIGHT == 0

        def chunk(c, carry):
            for u in range(ATTN_BLOCKS_IN_FLIGHT):
                block(0, c * ATTN_BLOCKS_IN_FLIGHT + u)
            return carry

        lax.fori_loop(0, nblk // ATTN_BLOCKS_IN_FLIGHT, chunk, 0)


def _dilated_steps(dil):
    nres = max(1, dil // 4)
    return nres, dil // nres


def _dilated_attn(q, k, v, g, job=None):
    nb, dil, m_rows, _ = q.shape
    nres, nsteps = _dilated_steps(dil)
    blk = pl.BlockSpec((1, nres, m_rows, GROUP_WIDTH), lambda b, r: (b, r, 0, 0))
    in_specs, args, aliases = [blk, blk, blk], [q, k, v], {}
    out_specs = [blk, blk]
    out_shape = [jax.ShapeDtypeStruct(q.shape, BF16), jax.ShapeDtypeStruct(q.shape, F32)]
    if job is not None:
        assert job.q.shape[0] == nb * nsteps * job.per_step
        j_in, j_args, j_out, j_shape, aliases = _job_operands(job, lambda b, r: b * nsteps + r, len(args), 2)
        in_specs, args = in_specs + j_in, args + j_args
        out_specs, out_shape = out_specs + j_out, out_shape + j_shape
    return pl.pallas_call(
        functools.partial(_dilated_attn_kernel, job=job.static if job else None),
        grid=(nb, nsteps), in_specs=in_specs, out_specs=out_specs, out_shape=out_shape,
        scratch_shapes=[pltpu.VMEM((nres, m_rows // BLOCK, GROUP_WIDTH, BLOCK), BF16)],
        input_output_aliases=aliases, compiler_params=_ARB(2), name=f"dilated_attn_g{g}",
    )(*args)


SWA_QUERY_TILE = 512


def _swa_attn_kernel(q_ref, k_ref, v_ref, sink_ref, gz_ref, x_ref, gate_ref, g_ref, w_ref, *rest, job):
    job_in, (y_ref,), job_out, (vt_ref, o_scr) = _split_job_refs(rest, job, 1)
    j = pl.program_id(1)
    seq = k_ref.shape[1]
    nq = q_ref.shape[1] // BLOCK
    base = _query_minus_key()
    if job:
        new_cols = _run_job(job, job_in, job_out, pl.program_id(0) * pl.num_programs(1) + j, shift_now=False)

    @pl.when(j == 0)
    def _():
        def transpose_values(t, c):
            vt_ref[t] = v_ref[0, pl.ds(pl.multiple_of(t * BLOCK, BLOCK), BLOCK), :].T
            return c

        lax.fori_loop(0, seq // BLOCK, transpose_values, 0)

    for i in range(nq):
        cur = j * nq + i
        prev = jnp.maximum(cur - 1, 0)
        dist = base + (cur - prev) * BLOCK
        mask = (dist >= 0) & (dist < B_WINDOW)
        qrows = slice(i * BLOCK, (i + 1) * BLOCK)
        k = k_ref[0, pl.ds(pl.multiple_of(prev * BLOCK, BLOCK), 2 * BLOCK), :]
        qt = q_ref[0, qrows, :].T
        vt_prev, vt_cur = vt_ref[prev], vt_ref[cur]
        outs = []
        for kvh in range(B_KV_HEADS):
            qd = _block_diag_queries(qt[kvh * B_GROUP * HEAD_DIM:(kvh + 1) * B_GROUP * HEAD_DIM], B_GROUP,
                                     lambda h, kvh=kvh: kvh, B_KV_HEADS)
            st_all = _dot(k, qd)
            vt_ext = _values_ext(vt_prev, vt_cur, kvh)
            for gq in range(B_GROUP):
                hq = kvh * B_GROUP + gq
                o_t, _, _ = _head_attend(st_all[:, gq * BLOCK:(gq + 1) * BLOCK], mask, vt_ext,
                                         sink_ref[hq:hq + 1, :])
                outs.append(o_t)
        o_scr[qrows, :] = jnp.concatenate(outs, axis=0).T.astype(BF16)
        if job:
            _job_shift_part(job, job_in, job_out, new_cols, i, nq)
    a = (o_scr[...].astype(F32) * gz_ref[0].astype(F32)).astype(BF16)
    _finish([a], w_ref, x_ref, gate_ref, g_ref, y_ref)


def _swa_attn(q, k, v, sinks, gz, x, gate, g, w, job=None):
    nb, t, _ = q.shape
    tq = min(SWA_QUERY_TILE, t)
    nj = t // tq
    qblk = pl.BlockSpec((1, tq, B_WIDTH), lambda b, j: (b, j, 0))
    xblk = pl.BlockSpec((1, tq, D_MODEL), lambda b, j: (b, j, 0))
    kvblk = pl.BlockSpec((1, t, B_KV_WIDTH), lambda b, j: (b, 0, 0))
    in_specs = [qblk, kvblk, kvblk, pl.BlockSpec((B_Q_HEADS, LANES), lambda b, j: (0, 0)),
                qblk, xblk, _mod_spec(gate, tq), pl.BlockSpec((1, D_MODEL), lambda b, j: (0, 0)),
                pl.BlockSpec((B_WIDTH, D_MODEL), lambda b, j: (0, 0))]
    args, aliases = [q, k, v, sinks, gz, x, gate, g, w], {}
    out_specs, out_shape = [xblk], [jax.ShapeDtypeStruct((nb, t, D_MODEL), F32)]
    if job is not None:
        assert job.q.shape[0] == nb * nj * job.per_step
        j_in, j_args, j_out, j_shape, aliases = _job_operands(job, lambda b, j: b * nj + j, len(args), 1)
        in_specs, args = in_specs + j_in, args + j_args
        out_specs, out_shape = out_specs + j_out, out_shape + j_shape
    return pl.pallas_call(
        functools.partial(_swa_attn_kernel, job=job.static if job else None),
        grid=(nb, nj), in_specs=in_specs, out_specs=out_specs, out_shape=out_shape,
        scratch_shapes=[pltpu.VMEM((t // BLOCK, B_KV_WIDTH, BLOCK), BF16), pltpu.VMEM((tq, B_WIDTH), BF16)],
        input_output_aliases=aliases, compiler_params=_ARB(2), name="swa_attn",
    )(*args)


def _finish_value(a_parts, w_ref, x, gate, g):
    acc = None
    row = 0
    for a in a_parts:
        part = _dot(a, w_ref[row:row + a.shape[1], :])
        acc = part if acc is None else acc + part
        row += a.shape[1]
    return x + gate * (_normalize(acc) * g)


def _finish(a_parts, w_ref, x_ref, gate_ref, g_ref, y_ref):
    y_ref[0] = _finish_value(a_parts, w_ref, x_ref[0], gate_ref[0], g_ref[...])


def _in_row_order(ref, scr):
    dil = ref.shape[1]
    if dil == 1:
        return ref[0, 0].astype(F32)
    nslab = ref.shape[3] // LANES
    for r in range(dil):
        for h in range(nslab):
            scr[h, pl.ds(r, ref.shape[2], stride=dil), :] = ref[0, r, :, h * LANES:(h + 1) * LANES].astype(F32)
    return jnp.concatenate([scr[h] for h in range(nslab)], axis=1)


def _mixture_parts(o_refs, l_refs, gz_ref, scratch):
    os = [_in_row_order(r, s) for r, s in zip(o_refs, scratch[0:3])]
    lses = [_in_row_order(r, s) for r, s in zip(l_refs, scratch[3:6])]
    top = jnp.maximum(jnp.maximum(lses[0], lses[1]), lses[2])
    es = [jnp.exp(l - top) for l in lses]
    inv = 1.0 / (es[0] + es[1] + es[2])
    parts = []
    for g in range(len(A_GROUPS)):
        gz = gz_ref[0, :, g * GROUP_WIDTH:(g + 1) * GROUP_WIDTH].astype(F32)
        parts.append((os[g] * (es[g] * inv) * gz).astype(BF16))
    return parts


def _outproj_mix_kernel(o0_ref, o1_ref, o2_ref, l0_ref, l1_ref, l2_ref, gz_ref, x_ref, gate_ref, g_ref, w_ref,
                        y_ref, *scratch):
    parts = _mixture_parts((o0_ref, o1_ref, o2_ref), (l0_ref, l1_ref, l2_ref), gz_ref, scratch)
    _finish(parts, w_ref, x_ref, gate_ref, g_ref, y_ref)


def _outproj_kernel(o_ref, gz_ref, x_ref, gate_ref, g_ref, w_ref, y_ref):
    a = (o_ref[0].astype(F32) * gz_ref[0].astype(F32)).astype(BF16)
    _finish([a], w_ref, x_ref, gate_ref, g_ref, y_ref)


def _outproj(os, lses, gz, x, gate, g, w):
    nb, t, _ = x.shape
    tm = min(TOKEN_TILE, t)
    row = lambda width: pl.BlockSpec((1, tm, width), lambda b, i: (b, i, 0))
    width = w.shape[0]
    tail_specs = [row(width), row(D_MODEL), _mod_spec(gate, tm),
                  pl.BlockSpec((1, D_MODEL), lambda b, i: (0, 0)),
                  pl.BlockSpec((width, D_MODEL), lambda b, i: (0, 0))]
    scratch = []
    if lses is None:
        kern, name = _outproj_kernel, "outproj_b"
        in_specs = [row(width)] + tail_specs
        args = (os[0], gz, x, gate, g, w)
    else:
        kern, name = _outproj_mix_kernel, "outproj_a"
        split = lambda a: pl.BlockSpec((1, a.shape[1], tm // a.shape[1], GROUP_WIDTH), lambda b, i: (b, 0, i, 0))
        in_specs = [split(a) for a in (*os, *lses)] + tail_specs
        args = (*os, *lses, gz, x, gate, g, w)
        scratch = [pltpu.VMEM((GROUP_WIDTH // LANES, tm, LANES), F32)] * 6
    return pl.pallas_call(
        kern, grid=(nb, t // tm), in_specs=in_specs, out_specs=row(D_MODEL),
        out_shape=jax.ShapeDtypeStruct((nb, t, D_MODEL), F32), scratch_shapes=scratch,
        compiler_params=_ARB(2), name=name,
    )(*args)


def _inproj_b_body(x, sh_ref, sc_ref, ksh_ref, ksc_ref, g_ref, gkv_ref, w_ref, wkv_ref, cos_ref, sin_ref, rows):
    xn = _normalize(x)
    hb = ((xn * g_ref[...]) * (1.0 + _mod_rows(sc_ref, rows)) + _mod_rows(sh_ref, rows)).astype(BF16)
    hk = ((xn * gkv_ref[...]) * (1.0 + _mod_rows(ksc_ref, rows)) + _mod_rows(ksh_ref, rows)).astype(BF16)
    cos, sin = cos_ref[rows, :], sin_ref[rows, :]
    q = _rope(_dot(hb, w_ref[:, 0:B_WIDTH]), cos, sin) * Q_SCALE
    gz = jax.nn.silu(_dot(hb, w_ref[:, B_WIDTH:2 * B_WIDTH]))
    kv = _dot(hk, wkv_ref[...])
    k = _rope(kv[:, 0:B_KV_WIDTH], cos, sin)
    v = kv[:, B_KV_WIDTH:2 * B_KV_WIDTH]
    return q, gz, k, v


def _mid_prompt_kernel(o0_ref, o1_ref, o2_ref, l0_ref, l1_ref, l2_ref, gza_ref, x_ref, gate_ref, gpost_ref, wo_ref,
                       sh_ref, sc_ref, ksh_ref, ksc_ref, g_ref, gkv_ref, w_ref, wkv_ref, cos_ref, sin_ref,
                       *rest, tm, nt, job):
    job_in, (x1_ref, q_ref, gz_ref, kd_ref, vd_ref, c_ref), job_out, scratch = _split_job_refs(rest, job, 6)
    if job:
        _run_job(job, job_in, job_out, pl.program_id(0) * nt + pl.program_id(1))
    parts = _mixture_parts((o0_ref, o1_ref, o2_ref), (l0_ref, l1_ref, l2_ref), gza_ref, scratch)
    x1 = _finish_value(parts, wo_ref, x_ref[0], gate_ref[0], gpost_ref[...])
    x1_ref[0] = x1
    q, gz, k, v = _inproj_b_body(x1, sh_ref, sc_ref, ksh_ref, ksc_ref, g_ref, gkv_ref, w_ref, wkv_ref,
                                 cos_ref, sin_ref, slice(0, tm))
    q_ref[0] = q.astype(BF16)
    gz_ref[0] = gz.astype(BF16)
    kd_ref[0] = k.astype(BF16)
    vd_ref[0] = v.astype(BF16)

    @pl.when(pl.program_id(1) == nt - 1)
    def _():
        _store_standard_rows(c_ref, k[tm - B_WINDOW:, :].T, slice(None))
        c_ref[0, B_KV_WIDTH:2 * B_KV_WIDTH, :] = v[tm - B_WINDOW:, :].T


def _inproj_b_sample_kernel(x_ref, sh_ref, sc_ref, ksh_ref, ksc_ref, g_ref, gkv_ref, w_ref, wkv_ref, cos_ref,
                            sin_ref, q_ref, gz_ref, k_ref, v_ref):
    q, gz, k, v = _inproj_b_body(x_ref[0], sh_ref, sc_ref, ksh_ref, ksc_ref, g_ref, gkv_ref, w_ref, wkv_ref,
                                 cos_ref, sin_ref, slice(0, x_ref.shape[1]))
    q_ref[0] = _standard_lanes(q)
    gz_ref[0] = gz.astype(BF16)
    k_ref[0] = _standard_lanes(k)
    v_ref[0] = v


def _inproj_b_sample(x, shift, scale, kshift, kscale, g, gkv, w, wkv, cos, sin):
    nb, t, _ = x.shape
    tm = min(TOKEN_TILE, t)
    row = lambda width: pl.BlockSpec((1, tm, width), lambda b, i: (b, i, 0))
    vec = pl.BlockSpec((1, D_MODEL), lambda b, i: (0, 0))
    tab = pl.BlockSpec((tm, LANES), lambda b, i: (i, 0))
    in_specs = [row(D_MODEL), _mod_spec(shift, tm), _mod_spec(scale, tm), _mod_spec(kshift, tm),
                _mod_spec(kscale, tm), vec, vec,
                pl.BlockSpec((D_MODEL, 2 * B_WIDTH), lambda b, i: (0, 0)),
                pl.BlockSpec((D_MODEL, 2 * B_KV_WIDTH), lambda b, i: (0, 0)), tab, tab]
    return pl.pallas_call(
        _inproj_b_sample_kernel, grid=(nb, t // tm), in_specs=in_specs,
        out_specs=[row(B_WIDTH), row(B_WIDTH), row(B_KV_WIDTH), row(B_KV_WIDTH)],
        out_shape=[jax.ShapeDtypeStruct((nb, t, B_WIDTH), F32), jax.ShapeDtypeStruct((nb, t, B_WIDTH), BF16),
                   jax.ShapeDtypeStruct((nb, t, B_KV_WIDTH), F32), jax.ShapeDtypeStruct((nb, t, B_KV_WIDTH), F32)],
        compiler_params=_ARB(2), name="inproj_b_sample",
    )(x, shift, scale, kshift, kscale, g, gkv, w, wkv, cos, sin)


def _mid_prompt(os, lses, gza, x, gate, gpost, wo, shift, scale, kshift, kscale, g, gkv, w, wkv, cos, sin, job=None):
    nb, t, _ = x.shape
    tm = min(TOKEN_TILE, t)
    nt = t // tm
    row = lambda width: pl.BlockSpec((1, tm, width), lambda b, i: (b, i, 0))
    vec = pl.BlockSpec((1, D_MODEL), lambda b, i: (0, 0))
    tab = pl.BlockSpec((tm, LANES), lambda b, i: (i, 0))
    full = lambda a: pl.BlockSpec(a.shape, lambda b, i: (0, 0))
    split = lambda a: pl.BlockSpec((1, a.shape[1], tm // a.shape[1], GROUP_WIDTH), lambda b, i: (b, 0, i, 0))
    in_specs = ([split(a) for a in (*os, *lses)]
                + [row(A_WIDTH), row(D_MODEL), _mod_spec(gate, tm), vec, full(wo)]
                + [_mod_spec(m, tm) for m in (shift, scale, kshift, kscale)] + [vec, vec, full(w), full(wkv), tab, tab])
    args = [*os, *lses, gza, x, gate, gpost, wo, shift, scale, kshift, kscale, g, gkv, w, wkv, cos, sin]
    out_specs = [row(D_MODEL), row(B_WIDTH), row(B_WIDTH), row(B_KV_WIDTH), row(B_KV_WIDTH),
                 pl.BlockSpec((1, 2 * B_KV_WIDTH, B_WINDOW), lambda b, i: (b, 0, 0))]
    out_shape = [jax.ShapeDtypeStruct((nb, t, D_MODEL), F32),
                 jax.ShapeDtypeStruct((nb, t, B_WIDTH), BF16), jax.ShapeDtypeStruct((nb, t, B_WIDTH), BF16),
                 jax.ShapeDtypeStruct((nb, t, B_KV_WIDTH), BF16), jax.ShapeDtypeStruct((nb, t, B_KV_WIDTH), BF16),
                 jax.ShapeDtypeStruct((nb, 2 * B_KV_WIDTH, B_WINDOW), F32)]
    aliases = {}
    if job is not None:
        assert job.q.shape[0] == nb * nt * job.per_step
        j_in, j_args, j_out, j_shape, aliases = _job_operands(job, lambda b, i: b * nt + i, len(args),
                                                              len(out_specs))
        in_specs, args = in_specs + j_in, args + j_args
        out_specs, out_shape = out_specs + j_out, out_shape + j_shape
    return pl.pallas_call(
        functools.partial(_mid_prompt_kernel, tm=tm, nt=nt, job=job.static if job else None),
        grid=(nb, nt), in_specs=in_specs, out_specs=out_specs, out_shape=out_shape,
        scratch_shapes=[pltpu.VMEM((GROUP_WIDTH // LANES, tm, LANES), F32)] * 6,
        input_output_aliases=aliases, compiler_params=_ARB(2), name="mid_prompt",
    )(*args)


def _eye():
    r = lax.broadcasted_iota(jnp.int32, (LANES, LANES), 0)
    c = lax.broadcasted_iota(jnp.int32, (LANES, LANES), 1)
    return r == c


def _row_to_col(row):
    eye = _eye()
    chunks = [jnp.sum(jnp.where(eye, row[:, j * LANES:(j + 1) * LANES], 0.0), axis=1, keepdims=True)
              for j in range(row.shape[1] // LANES)]
    return chunks[0] if len(chunks) == 1 else jnp.concatenate(chunks, axis=0)


def _shift_rows(cache_ref, out_ref, e, new_col, part=0, nparts=1):
    ntile = cache_ref.shape[2] // LANES
    first, last = part * ntile // nparts, (part + 1) * ntile // nparts
    if first == last:
        return
    lane = lax.broadcasted_iota(jnp.int32, (1, LANES), 1)
    nxt = pltpu.roll(cache_ref[e, :, first * LANES:(first + 1) * LANES], LANES - 1, 1)
    for j in range(first, last):
        cur = nxt
        if j + 1 < ntile:
            nxt = pltpu.roll(cache_ref[e, :, (j + 1) * LANES:(j + 2) * LANES], LANES - 1, 1)
            fill = nxt
        else:
            fill = new_col
        out_ref[e, :, j * LANES:(j + 1) * LANES] = jnp.where(lane < LANES - 1, cur, fill)


_NT = (((1,), (1,)), ((), ()))


def _sample_dilated_step(q_ref, kn_ref, vn_ref, cache_ref, out_cache_ref, o_ref, lse_ref, step, *, dil, bb,
                         shift_now=True):
    width, length = q_ref.shape[1], cache_ref.shape[2]
    new_cols = []
    sel = (lax.broadcasted_iota(jnp.int32, (8, width), 0)
           == lax.broadcasted_iota(jnp.int32, (8, width), 1) // HEAD_DIM)
    valid = lax.broadcasted_iota(jnp.int32, (1, length), 1) % dil == 0
    per_head = lambda x: jnp.sum(jnp.where(sel, x, 0.0), axis=0, keepdims=True)
    for e in range(bb):
        row = pl.ds(step * bb + e, 1)
        q, kn, vn = q_ref[row, :], kn_ref[row, :], vn_ref[row, :]
        qb = jnp.where(sel, q, 0.0)
        kt = cache_ref[e, 0:width, :].astype(BF16)
        vt = cache_ref[e, width:2 * width, :].astype(BF16)
        s = jnp.where(valid, _dot(qb.astype(BF16), kt), NEG)
        s_new = jnp.sum(qb * kn, axis=1, keepdims=True)
        m = jnp.maximum(jnp.max(s, axis=1, keepdims=True), s_new)
        p = jnp.exp2(s - m)
        p_new = jnp.exp2(s_new - m)
        l = per_head(jnp.sum(p, axis=1, keepdims=True) + p_new)
        o = lax.dot_general(p.astype(BF16), vt, _NT, preferred_element_type=F32)
        o_ref[row, :] = (per_head(o) + per_head(p_new) * vn) * (1.0 / l)
        lse_ref[row, :] = per_head(m) * LN2 + jnp.log(l)
        new_cols.append(jnp.concatenate([_row_to_col(kn), _row_to_col(vn)], axis=0))
        if shift_now:
            _shift_rows(cache_ref, out_cache_ref, e, new_cols[-1])
    return new_cols


def _sample_dilated_kernel(q_ref, kn_ref, vn_ref, cache_ref, out_cache_ref, o_ref, lse_ref, *, dil, bb):
    _sample_dilated_step(q_ref, kn_ref, vn_ref, cache_ref, out_cache_ref, o_ref, lse_ref, pl.program_id(0),
                         dil=dil, bb=bb)


class _ShiftJob(NamedTuple):
    q: jax.Array
    k_new: jax.Array
    v_new: jax.Array
    cache: jax.Array
    first: int
    per_step: int
    dil: int
    prev_out: Optional[jax.Array]

    @property
    def static(self):
        return (4 if self.prev_out is None else 5, self.dil, self.per_step)


def _job_operands(job, step_of, inputs_before, outputs_before):
    count, wq = job.q.shape
    _, chans, length = job.cache.shape
    first_blk = job.first // job.per_step
    full = pl.BlockSpec((count, wq), lambda *ids: (0, 0))
    tile = pl.BlockSpec((job.per_step, chans, length), lambda *ids: (first_blk + step_of(*ids), 0, 0))
    in_specs, args, aliases = [full, full, full, tile], [job.q, job.k_new, job.v_new, job.cache], {}
    if job.prev_out is not None:
        in_specs.append(pl.BlockSpec(memory_space=pl.ANY))
        args.append(job.prev_out)
        aliases[inputs_before + 4] = outputs_before
    out_shape = [jax.ShapeDtypeStruct(job.cache.shape, F32)] + [jax.ShapeDtypeStruct((count, wq), F32)] * 2
    return in_specs, args, [tile, full, full], out_shape, aliases


def _split_job_refs(rest, job_static, n_main_out):
    n_in = job_static[0] if job_static else 0
    n_out = 3 if job_static else 0
    job_in, rest = rest[:n_in], rest[n_in:]
    return job_in, rest[:n_main_out], rest[n_main_out:n_main_out + n_out], rest[n_main_out + n_out:]


def _run_job(job_static, in_refs, out_refs, step, shift_now=True):
    _, dil, per_step = job_static
    return _sample_dilated_step(*in_refs[:4], *out_refs, step, dil=dil, bb=per_step, shift_now=shift_now)


def _job_shift_part(job_static, in_refs, out_refs, new_cols, part, nparts):
    for e in range(job_static[2]):
        _shift_rows(in_refs[3], out_refs[0], e, new_cols[e], part, nparts)


def _sample_swa_kernel(q_ref, kn_ref, vn_ref, cache_ref, sink_ref, out_cache_ref, o_ref, qexp, oexp, *, bb):
    step = pl.program_id(0)
    nb, length = q_ref.shape[0], cache_ref.shape[2]
    low = lax.broadcasted_iota(jnp.int32, (1, LANES), 1) < HEAD_DIM

    @pl.when(step == 0)
    def _():
        for hq in range(B_Q_HEADS):
            slab = q_ref[:, (hq // 2) * LANES:(hq // 2 + 1) * LANES]
            src_low, dst_low = hq % 2 == 0, hq // B_GROUP == 0
            x = slab if src_low == dst_low else pltpu.roll(slab, HEAD_DIM, 1)
            qexp[hq * nb:(hq + 1) * nb, :] = jnp.where(low if dst_low else jnp.logical_not(low), x, 0.0)

    own_half = (lax.broadcasted_iota(jnp.int32, (B_Q_HEADS, LANES), 0) // B_GROUP
                == lax.broadcasted_iota(jnp.int32, (B_Q_HEADS, LANES), 1) // HEAD_DIM)
    valid = lax.broadcasted_iota(jnp.int32, (1, length), 1) >= 1
    sink = sink_ref[:, 0:1]
    for e in range(bb):
        b = step * bb + e
        heads = pl.ds(b, B_Q_HEADS, stride=nb)
        kn, vn = kn_ref[pl.ds(b, 1), :], vn_ref[pl.ds(b, 1), :]
        qb = qexp[heads, :]
        kt = cache_ref[e, 0:B_KV_WIDTH, :].astype(BF16)
        vt = cache_ref[e, B_KV_WIDTH:2 * B_KV_WIDTH, :].astype(BF16)
        s = jnp.where(valid, _dot(qb.astype(BF16), kt), NEG)
        s_new = jnp.sum(qb * kn, axis=1, keepdims=True)
        m = jnp.maximum(jnp.maximum(jnp.max(s, axis=1, keepdims=True), s_new), sink)
        p = jnp.exp2(s - m)
        p_new = jnp.exp2(s_new - m)
        l = jnp.sum(p, axis=1, keepdims=True) + p_new + jnp.exp2(sink - m)
        o = lax.dot_general(p.astype(BF16), vt, _NT, preferred_element_type=F32)
        oexp[heads, :] = jnp.where(own_half, o + p_new * vn, 0.0) * (1.0 / l)
        _shift_rows(cache_ref, out_cache_ref, e, jnp.concatenate([_row_to_col(kn), _row_to_col(vn)], axis=0))

    @pl.when(step == pl.num_programs(0) - 1)
    def _():
        for j in range(B_Q_HEADS // 2):
            even, odd = oexp[2 * j * nb:(2 * j + 1) * nb, :], oexp[(2 * j + 1) * nb:(2 * j + 2) * nb, :]
            if 2 * j // B_GROUP == 0:
                odd = pltpu.roll(odd, HEAD_DIM, 1)
            else:
                even = pltpu.roll(even, HEAD_DIM, 1)
            o_ref[:, j * LANES:(j + 1) * LANES] = jnp.where(low, even, odd)


def _sample_attn(q, k_new, v_new, cache_t, *, dil=1, sinks=None):
    nb, chans, length = cache_t.shape
    wq, wkv = q.shape[1], k_new.shape[1]
    bb = max(1, min(8, 2048 // length))
    full = lambda w: pl.BlockSpec((nb, w), lambda s: (0, 0))
    tile = pl.BlockSpec((bb, chans, length), lambda s: (s, 0, 0))
    in_specs = [full(wq), full(wkv), full(wkv), tile]
    args = [q, k_new, v_new, cache_t]
    out_specs = [tile, full(wq)]
    out_shape = [jax.ShapeDtypeStruct(cache_t.shape, F32), jax.ShapeDtypeStruct((nb, wq), F32)]
    if sinks is None:
        kern = functools.partial(_sample_dilated_kernel, dil=dil, bb=bb)
        out_specs.append(full(wq))
        out_shape.append(jax.ShapeDtypeStruct((nb, wq), F32))
        scratch = []
    else:
        kern = functools.partial(_sample_swa_kernel, bb=bb)
        in_specs.append(pl.BlockSpec(sinks.shape, lambda s: (0, 0)))
        args.append(sinks)
        scratch = [pltpu.VMEM((B_Q_HEADS * nb, LANES), F32)] * 2
    return pl.pallas_call(
        kern, grid=(nb // bb,), in_specs=in_specs, out_specs=out_specs, out_shape=out_shape,
        scratch_shapes=scratch, compiler_params=_ARB(1),
        name=f"sample_attn_d{dil}_l{length}" if sinks is None else "sample_swa",
    )(*args)


def _rope_tables(pos):
    half = HEAD_DIM // 2
    inv = ROPE_THETA ** (-jnp.arange(half, dtype=F32) / half)
    ang = pos.astype(F32)[:, None] * inv[None, :]
    cos, sin = jnp.cos(ang), jnp.sin(ang)
    return jnp.tile(cos, (1, 4)), jnp.concatenate([-sin, -sin, sin, sin], axis=1)


def _to_tiles(cache):
    nb, length = cache.shape[0], cache.shape[1]
    return jnp.transpose(cache, (0, 2, 3, 4, 1)).reshape(nb, -1, length)


def _from_tiles(tiles, heads):
    nb, _, length = tiles.shape
    return jnp.transpose(tiles.reshape(nb, 2, heads, HEAD_DIM, length), (0, 4, 1, 2, 3))


def kernel(x_prompt, x_sample, c_prompt, c_sample, cache_a_kv_g0, cache_a_kv_g1, cache_a_kv_g2, cache_b_kv, ada_w,
           ada_b, g_pre, g_post, w_in_a, w_o_a, w_in_b, w_o_b, sinks_b, ada_kv_w, ada_kv_b, g_kv, w_kv):
    nbp, t, _ = x_prompt.shape
    nbs = x_sample.shape[0]
    assert x_sample.shape[1] == 1

    c_all = jnp.concatenate([c_prompt, c_sample], axis=0)
    mod = _ada(c_all, ada_w, ada_b)
    mod_kv = _ada(c_all, ada_kv_w[None], ada_kv_b[None])[0]

    def split(a, parts, sample):
        rows = a[nbp:][None] if sample else a[:nbp][:, None]
        return [rows[..., p * D_MODEL:(p + 1) * D_MODEL] for p in range(parts)]

    def paired(w, ncols):
        return jnp.concatenate([_paired_columns(w[:, :ncols]), w[:, ncols:]], axis=1).astype(BF16)

    w_in_a_b, w_o_a_b = paired(w_in_a[0], 2 * A_WIDTH), w_o_a[0].astype(BF16)
    w_in_b_b, w_o_b_b, w_kv_b = paired(w_in_b[0], B_WIDTH), w_o_b[0].astype(BF16), paired(w_kv, B_KV_WIDTH)
    g_pre0, g_pre1 = g_pre[0:1], g_pre[1:2]
    g_post0, g_post1 = g_post[0:1], g_post[1:2]
    g_kv_r = g_kv[None]
    sinks = jnp.broadcast_to(sinks_b[0][:, None] * LOG2E, (B_Q_HEADS, LANES))

    xs = x_sample.reshape(1, nbs, D_MODEL)
    cos_s, sin_s = _rope_tables(jnp.full((nbs,), PAST_LEN, jnp.int32))
    sh0s, sc0s, gt0s = split(mod[0], 3, True)
    sh1s, sc1s, gt1s = split(mod[1], 3, True)
    kshs, kscs = split(mod_kv, 2, True)
    qs, ks, vs, gzs = _inproj_a(xs, sh0s, sc0s, g_pre0, w_in_a_b, cos_s, sin_s, sample=True)
    tiles = [_to_tiles(c[0]) for c in (cache_a_kv_g0, cache_a_kv_g1, cache_a_kv_g2)]

    def job(g, first, count, per_step, prev_out=None):
        rows, cols = slice(first, first + count), slice(g * GROUP_WIDTH, (g + 1) * GROUP_WIDTH)
        return _ShiftJob(qs[0, rows, cols], ks[0, rows, cols], vs[0, rows, cols], tiles[g], first, per_step,
                         A_GROUPS[g][1], prev_out)

    proj_steps = nbp * (t // min(TOKEN_TILE, t))

    cos_p, sin_p = _rope_tables(jnp.arange(t))
    sh0, sc0, gt0 = split(mod[0], 3, False)
    sh1, sc1, gt1 = split(mod[1], 3, False)
    ksh, ksc = split(mod_kv, 2, False)
    *qkv, gz, ca0, ca1, ca2, shifted2, o2a, lse2a = _inproj_a(
        x_prompt, sh0, sc0, g_pre0, w_in_a_b, cos_p, sin_p, sample=False, job=job(2, 0, proj_steps, 1))
    g1_steps = nbp * _dilated_steps(A_GROUPS[1][1])[1]
    attn = [_dilated_attn(*qkv[0:3], 0),
            _dilated_attn(*qkv[3:6], 1, job=job(0, 0, nbs, nbs // g1_steps)),
            _dilated_attn(*qkv[6:9], 2)]
    os, lses = [a[0] for a in attn], [a[1] for a in attn]
    shifted0, o0, lse0 = attn[1][2:]
    x1, qb, gzb, kd, vd, cb, shifted1, o1, lse1 = _mid_prompt(
        os, lses, gz, x_prompt, gt0, g_post0, w_o_a_b, sh1, sc1, ksh, ksc, g_pre1, g_kv_r, w_in_b_b, w_kv_b,
        cos_p, sin_p, job=job(1, 0, nbs, nbs // proj_steps))
    y_prompt, shifted2, o2b, lse2b = _swa_attn(qb, kd, vd, sinks, gzb, x1, gt1, g_post1, w_o_b_b,
                                               job=job(2, proj_steps, nbs - proj_steps, 1, prev_out=shifted2))
    new_a_prompt = [_from_tiles(c, A_HEADS_PER_GROUP)[None] for c in (ca0, ca1, ca2)]
    new_b_prompt = _from_tiles(cb, B_KV_HEADS)

    os = [o0, o1, jnp.concatenate([o2a, o2b], axis=0)]
    lses = [lse0, lse1, jnp.concatenate([lse2a, lse2b], axis=0)]
    os = [o.astype(BF16)[None, None] for o in os]
    lses = [l[None, None] for l in lses]
    new_a_sample = [_from_tiles(s, A_HEADS_PER_GROUP)[None] for s in (shifted0, shifted1, shifted2)]
    sh1, sc1, gt1, ksh, ksc, gz, gt0 = sh1s, sc1s, gt1s, kshs, kscs, gzs, gt0s
    xs1 = _outproj(os, lses, gz, xs, gt0, g_post0, w_o_a_b)
    qb, gzb, kb, vb = _inproj_b_sample(xs1, sh1, sc1, ksh, ksc, g_pre1, g_kv_r, w_in_b_b, w_kv_b, cos_s, sin_s)
    shifted_b, ob = _sample_attn(qb[0], kb[0], vb[0], _to_tiles(cache_b_kv), sinks=sinks)
    y_sample = _outproj([ob.astype(BF16)[None]], None, gzb, xs1, gt1, g_post1, w_o_b_b).reshape(nbs, 1, D_MODEL)
    new_b_sample = _from_tiles(shifted_b, B_KV_HEADS)

    return (y_prompt, y_sample, *new_a_prompt, new_b_prompt, *new_a_sample, new_b_sample)
```

```python
import functools
from typing import NamedTuple, Optional

import jax
import jax.numpy as jnp
from jax import lax
from jax.experimental import pallas as pl
from jax.experimental.pallas import tpu as pltpu

D_MODEL = 1024
HEAD_DIM = 64
A_GROUPS = ((128, 1), (512, 4), (2048, 16))
A_HEADS_PER_GROUP = 4
GROUP_WIDTH = A_HEADS_PER_GROUP * HEAD_DIM
A_WIDTH = GROUP_WIDTH * len(A_GROUPS)
B_Q_HEADS = 16
B_KV_HEADS = 2
B_GROUP = B_Q_HEADS // B_KV_HEADS
B_WIDTH = B_Q_HEADS * HEAD_DIM
B_KV_WIDTH = B_KV_HEADS * HEAD_DIM
B_WINDOW = 128
BLOCK = 128
PAST_LEN = 16384
ROPE_THETA = 10000.0
EPS = 1e-6
NEG = -1e30
LANES = 128
LOG2E = 1.4426950408889634
LN2 = 0.6931471805599453
Q_SCALE = HEAD_DIM ** -0.5 * LOG2E

BF16 = jnp.bfloat16
F32 = jnp.float32
TOKEN_TILE = 512

_ARB = lambda n: pltpu.CompilerParams(dimension_semantics=("arbitrary",) * n)


HALF = HEAD_DIM // 2
PAIRED_QUARTERS = (0, 2, 1, 3)


def _paired_columns(w):
    d, n = w.shape
    return w.reshape(d, n // LANES, 4, HALF)[:, :, PAIRED_QUARTERS, :].reshape(d, n)


def _rope(x, cos, sin_signed):
    outs = []
    for j in range(x.shape[1] // LANES):
        xs = x[:, j * LANES:(j + 1) * LANES]
        outs.append(xs * cos + pltpu.roll(xs, HEAD_DIM, 1) * sin_signed)
    return outs[0] if len(outs) == 1 else jnp.concatenate(outs, axis=1)


def _standard_lanes(x):
    quarter = lax.broadcasted_iota(jnp.int32, (1, LANES), 1) // HALF
    outs = []
    for j in range(x.shape[1] // LANES):
        xs = x[:, j * LANES:(j + 1) * LANES]
        outs.append(jnp.where(quarter == 1, pltpu.roll(xs, LANES - HALF, 1),
                              jnp.where(quarter == 2, pltpu.roll(xs, HALF, 1), xs)))
    return outs[0] if len(outs) == 1 else jnp.concatenate(outs, axis=1)


def _store_standard_rows(ref, xt, dst):
    for s in range(xt.shape[0] // LANES):
        for quarter, src in enumerate(PAIRED_QUARTERS):
            row, src_row = s * LANES + quarter * HALF, s * LANES + src * HALF
            ref[0, row:row + HALF, dst] = xt[src_row:src_row + HALF]


def _normalize(x):
    return x * lax.rsqrt(jnp.mean(x * x, axis=-1, keepdims=True) + EPS)


def _dot(a, b):
    return jnp.dot(a, b, preferred_element_type=F32)


def _ada_kernel(c_ref, w_ref, b_ref, o_ref):
    s = jax.nn.silu(c_ref[...]).astype(BF16)
    o_ref[...] = _dot(s, w_ref[...].astype(BF16)) + b_ref[...]


def _ada(c, w, b):
    nl, _, n = w.shape
    m = c.shape[0]
    tn = 1024
    return pl.pallas_call(
        _ada_kernel,
        grid=(nl, n // tn),
        in_specs=[
            pl.BlockSpec((m, D_MODEL), lambda l, j: (0, 0)),
            pl.BlockSpec((None, D_MODEL, tn), lambda l, j: (l, 0, j)),
            pl.BlockSpec((None, 1, tn), lambda l, j: (l, 0, j)),
        ],
        out_specs=pl.BlockSpec((None, m, tn), lambda l, j: (l, 0, j)),
        out_shape=jax.ShapeDtypeStruct((nl, m, n), F32),
        compiler_params=_ARB(2),
        name="ada",
    )(c, w, b.reshape(nl, 1, n))


ROW_CHUNK = 512


def _row_chunks(tm):
    size = min(ROW_CHUNK, tm)
    return [slice(c * size, (c + 1) * size) for c in range(tm // size)]


def _mod_rows(ref, rows):
    return ref[0] if ref.shape[1] == 1 else ref[0, rows, :]


def _inproj_a_body(x_ref, sh_ref, sc_ref, g_ref, w_ref, cos_ref, sin_ref, rows):
    h = _normalize(x_ref[0, rows, :]) * g_ref[...]
    hb = (h * (1.0 + _mod_rows(sc_ref, rows)) + _mod_rows(sh_ref, rows)).astype(BF16)
    cos, sin = cos_ref[rows, :], sin_ref[rows, :]
    q = _rope(_dot(hb, w_ref[:, 0:A_WIDTH]), cos, sin) * Q_SCALE
    k = _rope(_dot(hb, w_ref[:, A_WIDTH:2 * A_WIDTH]), cos, sin)
    v = _dot(hb, w_ref[:, 2 * A_WIDTH:3 * A_WIDTH])
    gz = jax.nn.silu(_dot(hb, w_ref[:, 3 * A_WIDTH:4 * A_WIDTH]))
    return q, k, v, gz


def _inproj_a_prompt_kernel(x_ref, sh_ref, sc_ref, g_ref, w_ref, cos_ref, sin_ref, *rest, tm, nt, job):
    job_in, main_out, job_out, (scr,) = _split_job_refs(rest, job, 13)
    qkv_refs, (gz_ref, c0_ref, c1_ref, c2_ref) = main_out[:9], main_out[9:]
    i = pl.program_id(1)
    if job:
        _run_job(job, job_in, job_out, pl.program_id(0) * nt + i)
    slabs_per_group = GROUP_WIDTH // LANES
    for c, rows in enumerate(_row_chunks(tm)):
        n = rows.stop - rows.start
        q, k, v, gz = _inproj_a_body(x_ref, sh_ref, sc_ref, g_ref, w_ref, cos_ref, sin_ref, rows)
        gz_ref[0, rows, :] = gz.astype(BF16)
        for which, val in enumerate((q, k, v)):
            for j in range(A_WIDTH // LANES):
                scr[c, which, j] = val[:, j * LANES:(j + 1) * LANES]
            for g, (_, dil) in enumerate(A_GROUPS):
                out_ref = qkv_refs[3 * g + which]
                dst = slice(rows.start // dil, rows.stop // dil)
                for r in range(dil):
                    for h in range(slabs_per_group):
                        part = scr[c, which, g * slabs_per_group + h, pl.ds(r, n // dil, stride=dil), :]
                        out_ref[0, r, dst, h * LANES:(h + 1) * LANES] = part.astype(BF16)
        for g, c_ref in enumerate((c0_ref, c1_ref, c2_ref)):
            win = A_GROUPS[g][0]
            width = min(tm, win)
            first_tile = nt - max(win // tm, 1)
            lo = max(rows.start, tm - width)
            if lo >= rows.stop:
                continue

            @pl.when(i >= first_tile)
            def _(g=g, c_ref=c_ref, lo=lo, k=k, v=v, rows=rows, width=width):
                cols = slice(g * GROUP_WIDTH, (g + 1) * GROUP_WIDTH)
                dst = slice(lo - (tm - width), rows.stop - (tm - width))
                _store_standard_rows(c_ref, k[lo - rows.start:, cols].T, dst)
                c_ref[0, GROUP_WIDTH:2 * GROUP_WIDTH, dst] = v[lo - rows.start:, cols].T


def _inproj_a_sample_kernel(x_ref, sh_ref, sc_ref, g_ref, w_ref, cos_ref, sin_ref,
                            q_ref, k_ref, v_ref, gz_ref):
    q, k, v, gz = _inproj_a_body(x_ref, sh_ref, sc_ref, g_ref, w_ref, cos_ref, sin_ref, slice(0, x_ref.shape[1]))
    q_ref[0] = _standard_lanes(q)
    k_ref[0] = _standard_lanes(k)
    v_ref[0] = v
    gz_ref[0] = gz.astype(BF16)


def _mod_spec(mod, tm):
    if mod.shape[1] == 1:
        return pl.BlockSpec((1, 1, D_MODEL), lambda b, i: (b, 0, 0))
    return pl.BlockSpec((1, tm, D_MODEL), lambda b, i: (b, i, 0))


def _inproj_a(x, shift, scale, g, w, cos, sin, *, sample, job=None):
    nb, t, _ = x.shape
    tm = min(TOKEN_TILE, t)
    nt = t // tm
    row = lambda width: pl.BlockSpec((1, tm, width), lambda b, i: (b, i, 0))
    in_specs = [
        row(D_MODEL), _mod_spec(shift, tm), _mod_spec(scale, tm),
        pl.BlockSpec((1, D_MODEL), lambda b, i: (0, 0)),
        pl.BlockSpec((D_MODEL, 4 * A_WIDTH), lambda b, i: (0, 0)),
        pl.BlockSpec((tm, LANES), lambda b, i: (i, 0)),
        pl.BlockSpec((tm, LANES), lambda b, i: (i, 0)),
    ]
    if sample:
        out_shape = [jax.ShapeDtypeStruct((nb, t, A_WIDTH), dt) for dt in (F32, F32, F32, BF16)]
        return pl.pallas_call(
            _inproj_a_sample_kernel, grid=(nb, nt), in_specs=in_specs,
            out_specs=[row(A_WIDTH)] * 4, out_shape=out_shape,
            compiler_params=_ARB(2), name="inproj_a_sample",
        )(x, shift, scale, g, w, cos, sin)
    out_shape, out_specs = [], []
    for _, dil in A_GROUPS:
        out_shape += [jax.ShapeDtypeStruct((nb, dil, t // dil, GROUP_WIDTH), BF16)] * 3
        out_specs += [pl.BlockSpec((1, dil, tm // dil, GROUP_WIDTH), lambda b, i: (b, 0, i, 0))] * 3
    out_shape.append(jax.ShapeDtypeStruct((nb, t, A_WIDTH), BF16))
    out_specs.append(row(A_WIDTH))
    for win, _ in A_GROUPS:
        rows = min(tm, win)
        first_tile = nt - max(win // tm, 1)
        out_shape.append(jax.ShapeDtypeStruct((nb, 2 * GROUP_WIDTH, min(win, t)), F32))
        out_specs.append(pl.BlockSpec(
            (1, 2 * GROUP_WIDTH, rows),
            lambda b, i, first_tile=first_tile: (b, 0, jnp.maximum(i - first_tile, 0))))
    args, aliases = [x, shift, scale, g, w, cos, sin], {}
    if job is not None:
        assert job.q.shape[0] == nb * nt * job.per_step
        j_in, j_args, j_out, j_shape, aliases = _job_operands(job, lambda b, i: b * nt + i, len(args),
                                                              len(out_specs))
        in_specs, args = in_specs + j_in, args + j_args
        out_specs, out_shape = out_specs + j_out, out_shape + j_shape
    return pl.pallas_call(
        functools.partial(_inproj_a_prompt_kernel, tm=tm, nt=nt, job=job.static if job else None),
        grid=(nb, nt), in_specs=in_specs, out_specs=out_specs, out_shape=out_shape,
        scratch_shapes=[pltpu.VMEM((tm // min(ROW_CHUNK, tm), 3, A_WIDTH // LANES, min(ROW_CHUNK, tm), LANES), F32)],
        input_output_aliases=aliases, compiler_params=_ARB(2), name="inproj_a_prompt",
    )(*args)


def _query_minus_key():
    kj = lax.broadcasted_iota(jnp.int32, (2 * BLOCK, BLOCK), 0)
    qi = lax.broadcasted_iota(jnp.int32, (2 * BLOCK, BLOCK), 1)
    return qi - kj


def _head_attend(st, mask, vt_ext, sink):
    st = jnp.where(mask, st, NEG)
    m = jnp.max(st, axis=0, keepdims=True)
    if sink is not None:
        m = jnp.maximum(m, sink)
    ext = _dot(vt_ext, jnp.exp2(st - m).astype(BF16))
    l = ext[HEAD_DIM:HEAD_DIM + 1, :]
    if sink is not None:
        l = l + jnp.exp2(sink - m)
    return ext[0:HEAD_DIM, :] * (1.0 / l), m, l


def _values_ext(vt_prev, vt_cur, head):
    rows = slice(head * HEAD_DIM, (head + 1) * HEAD_DIM)
    ones = jnp.ones((16, 2 * BLOCK), BF16)
    return jnp.concatenate([jnp.concatenate([vt_prev[rows], vt_cur[rows]], axis=1), ones], axis=0)


def _block_diag_queries(qt, heads, kv_of_head, kv_heads):
    zeros = jnp.zeros((HALF, BLOCK), BF16)
    cols = []
    for h in range(heads):
        src = (h // 2) * LANES + (h % 2) * HALF
        lo, hi = qt[src:src + HALF], qt[src + HEAD_DIM:src + HEAD_DIM + HALF]
        kv = kv_of_head(h)
        pieces = []
        for slab in range(kv_heads // 2):
            for quarter in range(4):
                mine = slab == kv // 2 and quarter % 2 == kv % 2
                pieces.append((lo if quarter < 2 else hi) if mine else zeros)
        cols.append(jnp.concatenate(pieces, axis=0))
    return jnp.concatenate(cols, axis=1)


ATTN_BLOCKS_IN_FLIGHT = 16


def _dilated_attn_kernel(q_ref, k_ref, v_ref, *rest, job):
    job_in, (o_ref, lse_ref), job_out, (vt_ref,) = _split_job_refs(rest, job, 2)
    if job:
        _run_job(job, job_in, job_out, pl.program_id(0) * pl.num_programs(1) + pl.program_id(1))
    nres, m_rows = q_ref.shape[1], q_ref.shape[2]
    nblk = m_rows // BLOCK
    base = _query_minus_key()
    heads = A_HEADS_PER_GROUP
    for r in range(nres):
        def transpose_values(j, c, r=r):
            rows = pl.ds(pl.multiple_of(j * BLOCK, BLOCK), BLOCK)
            vt_ref[r, j] = v_ref[0, r, rows, :].T
            return c

        lax.fori_loop(0, nblk, transpose_values, 0)

    def block(r, i):
        prev = jnp.maximum(i - 1, 0)
        dist = base + (i - prev) * BLOCK
        mask = (dist >= 0) & (dist <= BLOCK)
        qrows = pl.ds(pl.multiple_of(i * BLOCK, BLOCK), BLOCK)
        k = k_ref[0, r, pl.ds(pl.multiple_of(prev * BLOCK, BLOCK), 2 * BLOCK), :]
        qd = _block_diag_queries(q_ref[0, r, qrows, :].T, heads, lambda h: h, heads)
        st_all = _dot(k, qd)
        vt_prev, vt_cur = vt_ref[r, prev], vt_ref[r, i]
        outs, lses = [], []
        for h in range(heads):
            o_t, m, l = _head_attend(st_all[:, h * BLOCK:(h + 1) * BLOCK], mask,
                                     _values_ext(vt_prev, vt_cur, h), None)
            outs.append(o_t)
            lses.append(jnp.broadcast_to(m * LN2 + jnp.log(l), (HEAD_DIM, BLOCK)))
        o_ref[0, r, qrows, :] = jnp.concatenate(outs, axis=0).T.astype(BF16)
        lse_ref[0, r, qrows, :] = jnp.concatenate(lses, axis=0).T

    if nres * nblk <= ATTN_BLOCKS_IN_FLIGHT:
        for r in range(nres):
            for i in range(nblk):
                block(r, i)
    else:
        assert nres == 1 and nblk % ATTN_BLOCKS_IN_FLIGHT == 0

        def chunk(c, carry):
            for u in range(ATTN_BLOCKS_IN_FLIGHT):
                block(0, c * ATTN_BLOCKS_IN_FLIGHT + u)
            return carry

        lax.fori_loop(0, nblk // ATTN_BLOCKS_IN_FLIGHT, chunk, 0)


def _dilated_steps(dil):
    nres = max(1, dil // 2)
    return nres, dil // nres


def _dilated_attn(q, k, v, g, job=None):
    nb, dil, m_rows, _ = q.shape
    nres, nsteps = _dilated_steps(dil)
    blk = pl.BlockSpec((1, nres, m_rows, GROUP_WIDTH), lambda b, r: (b, r, 0, 0))
    in_specs, args, aliases = [blk, blk, blk], [q, k, v], {}
    out_specs = [blk, blk]
    out_shape = [jax.ShapeDtypeStruct(q.shape, BF16), jax.ShapeDtypeStruct(q.shape, F32)]
    if job is not None:
        assert job.q.shape[0] == nb * nsteps * job.per_step
        j_in, j_args, j_out, j_shape, aliases = _job_operands(job, lambda b, r: b * nsteps + r, len(args), 2)
        in_specs, args = in_specs + j_in, args + j_args
        out_specs, out_shape = out_specs + j_out, out_shape + j_shape
    return pl.pallas_call(
        functools.partial(_dilated_attn_kernel, job=job.static if job else None),
        grid=(nb, nsteps), in_specs=in_specs, out_specs=out_specs, out_shape=out_shape,
        scratch_shapes=[pltpu.VMEM((nres, m_rows // BLOCK, GROUP_WIDTH, BLOCK), BF16)],
        input_output_aliases=aliases, compiler_params=_ARB(2), name=f"dilated_attn_g{g}",
    )(*args)


SWA_QUERY_TILE = 512


def _swa_attn_kernel(q_ref, k_ref, v_ref, sink_ref, gz_ref, x_ref, gate_ref, g_ref, w_ref, *rest, job):
    job_in, (y_ref,), job_out, (vt_ref, o_scr) = _split_job_refs(rest, job, 1)
    j = pl.program_id(1)
    seq = k_ref.shape[1]
    nq = q_ref.shape[1] // BLOCK
    base = _query_minus_key()
    if job:
        new_cols = _run_job(job, job_in, job_out, pl.program_id(0) * pl.num_programs(1) + j, shift_now=False)

    @pl.when(j == 0)
    def _():
        def transpose_values(t, c):
            vt_ref[t] = v_ref[0, pl.ds(pl.multiple_of(t * BLOCK, BLOCK), BLOCK), :].T
            return c

        lax.fori_loop(0, seq // BLOCK, transpose_values, 0)

    for i in range(nq):
        cur = j * nq + i
        prev = jnp.maximum(cur - 1, 0)
        dist = base + (cur - prev) * BLOCK
        mask = (dist >= 0) & (dist < B_WINDOW)
        qrows = slice(i * BLOCK, (i + 1) * BLOCK)
        k = k_ref[0, pl.ds(pl.multiple_of(prev * BLOCK, BLOCK), 2 * BLOCK), :]
        qt = q_ref[0, qrows, :].T
        vt_prev, vt_cur = vt_ref[prev], vt_ref[cur]
        outs = []
        for kvh in range(B_KV_HEADS):
            qd = _block_diag_queries(qt[kvh * B_GROUP * HEAD_DIM:(kvh + 1) * B_GROUP * HEAD_DIM], B_GROUP,
                                     lambda h, kvh=kvh: kvh, B_KV_HEADS)
            st_all = _dot(k, qd)
            vt_ext = _values_ext(vt_prev, vt_cur, kvh)
            for gq in range(B_GROUP):
                hq = kvh * B_GROUP + gq
                o_t, _, _ = _head_attend(st_all[:, gq * BLOCK:(gq + 1) * BLOCK], mask, vt_ext,
                                         sink_ref[hq:hq + 1, :])
                outs.append(o_t)
        o_scr[qrows, :] = jnp.concatenate(outs, axis=0).T.astype(BF16)
        if job:
            _job_shift_part(job, job_in, job_out, new_cols, i, nq)
    a = (o_scr[...].astype(F32) * gz_ref[0].astype(F32)).astype(BF16)
    _finish([a], w_ref, x_ref, gate_ref, g_ref, y_ref)


def _swa_attn(q, k, v, sinks, gz, x, gate, g, w, job=None):
    nb, t, _ = q.shape
    tq = min(SWA_QUERY_TILE, t)
    nj = t // tq
    qblk = pl.BlockSpec((1, tq, B_WIDTH), lambda b, j: (b, j, 0))
    xblk = pl.BlockSpec((1, tq, D_MODEL), lambda b, j: (b, j, 0))
    kvblk = pl.BlockSpec((1, t, B_KV_WIDTH), lambda b, j: (b, 0, 0))
    in_specs = [qblk, kvblk, kvblk, pl.BlockSpec((B_Q_HEADS, LANES), lambda b, j: (0, 0)),
                qblk, xblk, _mod_spec(gate, tq), pl.BlockSpec((1, D_MODEL), lambda b, j: (0, 0)),
                pl.BlockSpec((B_WIDTH, D_MODEL), lambda b, j: (0, 0))]
    args, aliases = [q, k, v, sinks, gz, x, gate, g, w], {}
    out_specs, out_shape = [xblk], [jax.ShapeDtypeStruct((nb, t, D_MODEL), F32)]
    if job is not None:
        assert job.q.shape[0] == nb * nj * job.per_step
        j_in, j_args, j_out, j_shape, aliases = _job_operands(job, lambda b, j: b * nj + j, len(args), 1)
        in_specs, args = in_specs + j_in, args + j_args
        out_specs, out_shape = out_specs + j_out, out_shape + j_shape
    return pl.pallas_call(
        functools.partial(_swa_attn_kernel, job=job.static if job else None),
        grid=(nb, nj), in_specs=in_specs, out_specs=out_specs, out_shape=out_shape,
        scratch_shapes=[pltpu.VMEM((t // BLOCK, B_KV_WIDTH, BLOCK), BF16), pltpu.VMEM((tq, B_WIDTH), BF16)],
        input_output_aliases=aliases, compiler_params=_ARB(2), name="swa_attn",
    )(*args)


def _finish_value(a_parts, w_ref, x, gate, g):
    acc = None
    row = 0
    for a in a_parts:
        part = _dot(a, w_ref[row:row + a.shape[1], :])
        acc = part if acc is None else acc + part
        row += a.shape[1]
    return x + gate * (_normalize(acc) * g)


def _finish(a_parts, w_ref, x_ref, gate_ref, g_ref, y_ref):
    y_ref[0] = _finish_value(a_parts, w_ref, x_ref[0], gate_ref[0], g_ref[...])


def _in_row_order(ref, scr):
    dil = ref.shape[1]
    if dil == 1:
        return ref[0, 0].astype(F32)
    nslab = ref.shape[3] // LANES
    for r in range(dil):
        for h in range(nslab):
            scr[h, pl.ds(r, ref.shape[2], stride=dil), :] = ref[0, r, :, h * LANES:(h + 1) * LANES].astype(F32)
    return jnp.concatenate([scr[h] for h in range(nslab)], axis=1)


def _mixture_parts(o_refs, l_refs, gz_ref, scratch):
    os = [_in_row_order(r, s) for r, s in zip(o_refs, scratch[0:3])]
    lses = [_in_row_order(r, s) for r, s in zip(l_refs, scratch[3:6])]
    top = jnp.maximum(jnp.maximum(lses[0], lses[1]), lses[2])
    es = [jnp.exp(l - top) for l in lses]
    inv = 1.0 / (es[0] + es[1] + es[2])
    parts = []
    for g in range(len(A_GROUPS)):
        gz = gz_ref[0, :, g * GROUP_WIDTH:(g + 1) * GROUP_WIDTH].astype(F32)
        parts.append((os[g] * (es[g] * inv) * gz).astype(BF16))
    return parts


def _outproj_mix_kernel(o0_ref, o1_ref, o2_ref, l0_ref, l1_ref, l2_ref, gz_ref, x_ref, gate_ref, g_ref, w_ref,
                        y_ref, *scratch):
    parts = _mixture_parts((o0_ref, o1_ref, o2_ref), (l0_ref, l1_ref, l2_ref), gz_ref, scratch)
    _finish(parts, w_ref, x_ref, gate_ref, g_ref, y_ref)


def _outproj_kernel(o_ref, gz_ref, x_ref, gate_ref, g_ref, w_ref, y_ref):
    a = (o_ref[0].astype(F32) * gz_ref[0].astype(F32)).astype(BF16)
    _finish([a], w_ref, x_ref, gate_ref, g_ref, y_ref)


def _outproj(os, lses, gz, x, gate, g, w):
    nb, t, _ = x.shape
    tm = min(TOKEN_TILE, t)
    row = lambda width: pl.BlockSpec((1, tm, width), lambda b, i: (b, i, 0))
    width = w.shape[0]
    tail_specs = [row(width), row(D_MODEL), _mod_spec(gate, tm),
                  pl.BlockSpec((1, D_MODEL), lambda b, i: (0, 0)),
                  pl.BlockSpec((width, D_MODEL), lambda b, i: (0, 0))]
    scratch = []
    if lses is None:
        kern, name = _outproj_kernel, "outproj_b"
        in_specs = [row(width)] + tail_specs
        args = (os[0], gz, x, gate, g, w)
    else:
        kern, name = _outproj_mix_kernel, "outproj_a"
        split = lambda a: pl.BlockSpec((1, a.shape[1], tm // a.shape[1], GROUP_WIDTH), lambda b, i: (b, 0, i, 0))
        in_specs = [split(a) for a in (*os, *lses)] + tail_specs
        args = (*os, *lses, gz, x, gate, g, w)
        scratch = [pltpu.VMEM((GROUP_WIDTH // LANES, tm, LANES), F32)] * 6
    return pl.pallas_call(
        kern, grid=(nb, t // tm), in_specs=in_specs, out_specs=row(D_MODEL),
        out_shape=jax.ShapeDtypeStruct((nb, t, D_MODEL), F32), scratch_shapes=scratch,
        compiler_params=_ARB(2), name=name,
    )(*args)


def _inproj_b_body(x, sh_ref, sc_ref, ksh_ref, ksc_ref, g_ref, gkv_ref, w_ref, wkv_ref, cos_ref, sin_ref, rows):
    xn = _normalize(x)
    hb = ((xn * g_ref[...]) * (1.0 + _mod_rows(sc_ref, rows)) + _mod_rows(sh_ref, rows)).astype(BF16)
    hk = ((xn * gkv_ref[...]) * (1.0 + _mod_rows(ksc_ref, rows)) + _mod_rows(ksh_ref, rows)).astype(BF16)
    cos, sin = cos_ref[rows, :], sin_ref[rows, :]
    q = _rope(_dot(hb, w_ref[:, 0:B_WIDTH]), cos, sin) * Q_SCALE
    gz = jax.nn.silu(_dot(hb, w_ref[:, B_WIDTH:2 * B_WIDTH]))
    kv = _dot(hk, wkv_ref[...])
    k = _rope(kv[:, 0:B_KV_WIDTH], cos, sin)
    v = kv[:, B_KV_WIDTH:2 * B_KV_WIDTH]
    return q, gz, k, v


def _mid_prompt_kernel(o0_ref, o1_ref, o2_ref, l0_ref, l1_ref, l2_ref, gza_ref, x_ref, gate_ref, gpost_ref, wo_ref,
                       sh_ref, sc_ref, ksh_ref, ksc_ref, g_ref, gkv_ref, w_ref, wkv_ref, cos_ref, sin_ref,
                       *rest, tm, nt, job):
    job_in, (x1_ref, q_ref, gz_ref, kd_ref, vd_ref, c_ref), job_out, scratch = _split_job_refs(rest, job, 6)
    if job:
        _run_job(job, job_in, job_out, pl.program_id(0) * nt + pl.program_id(1))
    parts = _mixture_parts((o0_ref, o1_ref, o2_ref), (l0_ref, l1_ref, l2_ref), gza_ref, scratch)
    x1 = _finish_value(parts, wo_ref, x_ref[0], gate_ref[0], gpost_ref[...])
    x1_ref[0] = x1
    q, gz, k, v = _inproj_b_body(x1, sh_ref, sc_ref, ksh_ref, ksc_ref, g_ref, gkv_ref, w_ref, wkv_ref,
                                 cos_ref, sin_ref, slice(0, tm))
    q_ref[0] = q.astype(BF16)
    gz_ref[0] = gz.astype(BF16)
    kd_ref[0] = k.astype(BF16)
    vd_ref[0] = v.astype(BF16)

    @pl.when(pl.program_id(1) == nt - 1)
    def _():
        _store_standard_rows(c_ref, k[tm - B_WINDOW:, :].T, slice(None))
        c_ref[0, B_KV_WIDTH:2 * B_KV_WIDTH, :] = v[tm - B_WINDOW:, :].T


def _inproj_b_sample_kernel(x_ref, sh_ref, sc_ref, ksh_ref, ksc_ref, g_ref, gkv_ref, w_ref, wkv_ref, cos_ref,
                            sin_ref, q_ref, gz_ref, k_ref, v_ref):
    q, gz, k, v = _inproj_b_body(x_ref[0], sh_ref, sc_ref, ksh_ref, ksc_ref, g_ref, gkv_ref, w_ref, wkv_ref,
                                 cos_ref, sin_ref, slice(0, x_ref.shape[1]))
    q_ref[0] = _standard_lanes(q)
    gz_ref[0] = gz.astype(BF16)
    k_ref[0] = _standard_lanes(k)
    v_ref[0] = v


def _inproj_b_sample(x, shift, scale, kshift, kscale, g, gkv, w, wkv, cos, sin):
    nb, t, _ = x.shape
    tm = min(TOKEN_TILE, t)
    row = lambda width: pl.BlockSpec((1, tm, width), lambda b, i: (b, i, 0))
    vec = pl.BlockSpec((1, D_MODEL), lambda b, i: (0, 0))
    tab = pl.BlockSpec((tm, LANES), lambda b, i: (i, 0))
    in_specs = [row(D_MODEL), _mod_spec(shift, tm), _mod_spec(scale, tm), _mod_spec(kshift, tm),
                _mod_spec(kscale, tm), vec, vec,
                pl.BlockSpec((D_MODEL, 2 * B_WIDTH), lambda b, i: (0, 0)),
                pl.BlockSpec((D_MODEL, 2 * B_KV_WIDTH), lambda b, i: (0, 0)), tab, tab]
    return pl.pallas_call(
        _inproj_b_sample_kernel, grid=(nb, t // tm), in_specs=in_specs,
        out_specs=[row(B_WIDTH), row(B_WIDTH), row(B_KV_WIDTH), row(B_KV_WIDTH)],
        out_shape=[jax.ShapeDtypeStruct((nb, t, B_WIDTH), F32), jax.ShapeDtypeStruct((nb, t, B_WIDTH), BF16),
                   jax.ShapeDtypeStruct((nb, t, B_KV_WIDTH), F32), jax.ShapeDtypeStruct((nb, t, B_KV_WIDTH), F32)],
        compiler_params=_ARB(2), name="inproj_b_sample",
    )(x, shift, scale, kshift, kscale, g, gkv, w, wkv, cos, sin)


def _mid_prompt(os, lses, gza, x, gate, gpost, wo, shift, scale, kshift, kscale, g, gkv, w, wkv, cos, sin, job=None):
    nb, t, _ = x.shape
    tm = min(TOKEN_TILE, t)
    nt = t // tm
    row = lambda width: pl.BlockSpec((1, tm, width), lambda b, i: (b, i, 0))
    vec = pl.BlockSpec((1, D_MODEL), lambda b, i: (0, 0))
    tab = pl.BlockSpec((tm, LANES), lambda b, i: (i, 0))
    full = lambda a: pl.BlockSpec(a.shape, lambda b, i: (0, 0))
    split = lambda a: pl.BlockSpec((1, a.shape[1], tm // a.shape[1], GROUP_WIDTH), lambda b, i: (b, 0, i, 0))
    in_specs = ([split(a) for a in (*os, *lses)]
                + [row(A_WIDTH), row(D_MODEL), _mod_spec(gate, tm), vec, full(wo)]
                + [_mod_spec(m, tm) for m in (shift, scale, kshift, kscale)] + [vec, vec, full(w), full(wkv), tab, tab])
    args = [*os, *lses, gza, x, gate, gpost, wo, shift, scale, kshift, kscale, g, gkv, w, wkv, cos, sin]
    out_specs = [row(D_MODEL), row(B_WIDTH), row(B_WIDTH), row(B_KV_WIDTH), row(B_KV_WIDTH),
                 pl.BlockSpec((1, 2 * B_KV_WIDTH, B_WINDOW), lambda b, i: (b, 0, 0))]
    out_shape = [jax.ShapeDtypeStruct((nb, t, D_MODEL), F32),
                 jax.ShapeDtypeStruct((nb, t, B_WIDTH), BF16), jax.ShapeDtypeStruct((nb, t, B_WIDTH), BF16),
                 jax.ShapeDtypeStruct((nb, t, B_KV_WIDTH), BF16), jax.ShapeDtypeStruct((nb, t, B_KV_WIDTH), BF16),
                 jax.ShapeDtypeStruct((nb, 2 * B_KV_WIDTH, B_WINDOW), F32)]
    aliases = {}
    if job is not None:
        assert job.q.shape[0] == nb * nt * job.per_step
        j_in, j_args, j_out, j_shape, aliases = _job_operands(job, lambda b, i: b * nt + i, len(args),
                                                              len(out_specs))
        in_specs, args = in_specs + j_in, args + j_args
        out_specs, out_shape = out_specs + j_out, out_shape + j_shape
    return pl.pallas_call(
        functools.partial(_mid_prompt_kernel, tm=tm, nt=nt, job=job.static if job else None),
        grid=(nb, nt), in_specs=in_specs, out_specs=out_specs, out_shape=out_shape,
        scratch_shapes=[pltpu.VMEM((GROUP_WIDTH // LANES, tm, LANES), F32)] * 6,
        input_output_aliases=aliases, compiler_params=_ARB(2), name="mid_prompt",
    )(*args)


def _eye():
    r = lax.broadcasted_iota(jnp.int32, (LANES, LANES), 0)
    c = lax.broadcasted_iota(jnp.int32, (LANES, LANES), 1)
    return r == c


def _row_to_col(row):
    eye = _eye()
    chunks = [jnp.sum(jnp.where(eye, row[:, j * LANES:(j + 1) * LANES], 0.0), axis=1, keepdims=True)
              for j in range(row.shape[1] // LANES)]
    return chunks[0] if len(chunks) == 1 else jnp.concatenate(chunks, axis=0)


def _shift_rows(cache_ref, out_ref, e, new_col, part=0, nparts=1):
    ntile = cache_ref.shape[2] // LANES
    first, last = part * ntile // nparts, (part + 1) * ntile // nparts
    if first == last:
        return
    lane = lax.broadcasted_iota(jnp.int32, (1, LANES), 1)
    nxt = pltpu.roll(cache_ref[e, :, first * LANES:(first + 1) * LANES], LANES - 1, 1)
    for j in range(first, last):
        cur = nxt
        if j + 1 < ntile:
            nxt = pltpu.roll(cache_ref[e, :, (j + 1) * LANES:(j + 2) * LANES], LANES - 1, 1)
            fill = nxt
        else:
            fill = new_col
        out_ref[e, :, j * LANES:(j + 1) * LANES] = jnp.where(lane < LANES - 1, cur, fill)


_NT = (((1,), (1,)), ((), ()))


def _sample_dilated_step(q_ref, kn_ref, vn_ref, cache_ref, out_cache_ref, o_ref, lse_ref, step, *, dil, bb,
                         shift_now=True):
    width, length = q_ref.shape[1], cache_ref.shape[2]
    new_cols = []
    sel = (lax.broadcasted_iota(jnp.int32, (8, width), 0)
           == lax.broadcasted_iota(jnp.int32, (8, width), 1) // HEAD_DIM)
    valid = lax.broadcasted_iota(jnp.int32, (1, length), 1) % dil == 0
    per_head = lambda x: jnp.sum(jnp.where(sel, x, 0.0), axis=0, keepdims=True)
    for e in range(bb):
        row = pl.ds(step * bb + e, 1)
        q, kn, vn = q_ref[row, :], kn_ref[row, :], vn_ref[row, :]
        qb = jnp.where(sel, q, 0.0)
        kt = cache_ref[e, 0:width, :].astype(BF16)
        vt = cache_ref[e, width:2 * width, :].astype(BF16)
        s = jnp.where(valid, _dot(qb.astype(BF16), kt), NEG)
        s_new = jnp.sum(qb * kn, axis=1, keepdims=True)
        m = jnp.maximum(jnp.max(s, axis=1, keepdims=True), s_new)
        p = jnp.exp2(s - m)
        p_new = jnp.exp2(s_new - m)
        l = per_head(jnp.sum(p, axis=1, keepdims=True) + p_new)
        o = lax.dot_general(p.astype(BF16), vt, _NT, preferred_element_type=F32)
        o_ref[row, :] = (per_head(o) + per_head(p_new) * vn) * (1.0 / l)
        lse_ref[row, :] = per_head(m) * LN2 + jnp.log(l)
        new_cols.append(jnp.concatenate([_row_to_col(kn), _row_to_col(vn)], axis=0))
        if shift_now:
            _shift_rows(cache_ref, out_cache_ref, e, new_cols[-1])
    return new_cols


def _sample_dilated_kernel(q_ref, kn_ref, vn_ref, cache_ref, out_cache_ref, o_ref, lse_ref, *, dil, bb):
    _sample_dilated_step(q_ref, kn_ref, vn_ref, cache_ref, out_cache_ref, o_ref, lse_ref, pl.program_id(0),
                         dil=dil, bb=bb)


class _ShiftJob(NamedTuple):
    q: jax.Array
    k_new: jax.Array
    v_new: jax.Array
    cache: jax.Array
    first: int
    per_step: int
    dil: int
    prev_out: Optional[jax.Array]

    @property
    def static(self):
        return (4 if self.prev_out is None else 5, self.dil, self.per_step)


def _job_operands(job, step_of, inputs_before, outputs_before):
    count, wq = job.q.shape
    _, chans, length = job.cache.shape
    first_blk = job.first // job.per_step
    full = pl.BlockSpec((count, wq), lambda *ids: (0, 0))
    tile = pl.BlockSpec((job.per_step, chans, length), lambda *ids: (first_blk + step_of(*ids), 0, 0))
    in_specs, args, aliases = [full, full, full, tile], [job.q, job.k_new, job.v_new, job.cache], {}
    if job.prev_out is not None:
        in_specs.append(pl.BlockSpec(memory_space=pl.ANY))
        args.append(job.prev_out)
        aliases[inputs_before + 4] = outputs_before
    out_shape = [jax.ShapeDtypeStruct(job.cache.shape, F32)] + [jax.ShapeDtypeStruct((count, wq), F32)] * 2
    return in_specs, args, [tile, full, full], out_shape, aliases


def _split_job_refs(rest, job_static, n_main_out):
    n_in = job_static[0] if job_static else 0
    n_out = 3 if job_static else 0
    job_in, rest = rest[:n_in], rest[n_in:]
    return job_in, rest[:n_main_out], rest[n_main_out:n_main_out + n_out], rest[n_main_out + n_out:]


def _run_job(job_static, in_refs, out_refs, step, shift_now=True):
    _, dil, per_step = job_static
    return _sample_dilated_step(*in_refs[:4], *out_refs, step, dil=dil, bb=per_step, shift_now=shift_now)


def _job_shift_part(job_static, in_refs, out_refs, new_cols, part, nparts):
    for e in range(job_static[2]):
        _shift_rows(in_refs[3], out_refs[0], e, new_cols[e], part, nparts)


def _sample_swa_kernel(q_ref, kn_ref, vn_ref, cache_ref, sink_ref, out_cache_ref, o_ref, qexp, oexp, *, bb):
    step = pl.program_id(0)
    nb, length = q_ref.shape[0], cache_ref.shape[2]
    low = lax.broadcasted_iota(jnp.int32, (1, LANES), 1) < HEAD_DIM

    @pl.when(step == 0)
    def _():
        for hq in range(B_Q_HEADS):
            slab = q_ref[:, (hq // 2) * LANES:(hq // 2 + 1) * LANES]
            src_low, dst_low = hq % 2 == 0, hq // B_GROUP == 0
            x = slab if src_low == dst_low else pltpu.roll(slab, HEAD_DIM, 1)
            qexp[hq * nb:(hq + 1) * nb, :] = jnp.where(low if dst_low else jnp.logical_not(low), x, 0.0)

    own_half = (lax.broadcasted_iota(jnp.int32, (B_Q_HEADS, LANES), 0) // B_GROUP
                == lax.broadcasted_iota(jnp.int32, (B_Q_HEADS, LANES), 1) // HEAD_DIM)
    valid = lax.broadcasted_iota(jnp.int32, (1, length), 1) >= 1
    sink = sink_ref[:, 0:1]
    for e in range(bb):
        b = step * bb + e
        heads = pl.ds(b, B_Q_HEADS, stride=nb)
        kn, vn = kn_ref[pl.ds(b, 1), :], vn_ref[pl.ds(b, 1), :]
        qb = qexp[heads, :]
        kt = cache_ref[e, 0:B_KV_WIDTH, :].astype(BF16)
        vt = cache_ref[e, B_KV_WIDTH:2 * B_KV_WIDTH, :].astype(BF16)
        s = jnp.where(valid, _dot(qb.astype(BF16), kt), NEG)
        s_new = jnp.sum(qb * kn, axis=1, keepdims=True)
        m = jnp.maximum(jnp.maximum(jnp.max(s, axis=1, keepdims=True), s_new), sink)
        p = jnp.exp2(s - m)
        p_new = jnp.exp2(s_new - m)
        l = jnp.sum(p, axis=1, keepdims=True) + p_new + jnp.exp2(sink - m)
        o = lax.dot_general(p.astype(BF16), vt, _NT, preferred_element_type=F32)
        oexp[heads, :] = jnp.where(own_half, o + p_new * vn, 0.0) * (1.0 / l)
        _shift_rows(cache_ref, out_cache_ref, e, jnp.concatenate([_row_to_col(kn), _row_to_col(vn)], axis=0))

    @pl.when(step == pl.num_programs(0) - 1)
    def _():
        for j in range(B_Q_HEADS // 2):
            even, odd = oexp[2 * j * nb:(2 * j + 1) * nb, :], oexp[(2 * j + 1) * nb:(2 * j + 2) * nb, :]
            if 2 * j // B_GROUP == 0:
                odd = pltpu.roll(odd, HEAD_DIM, 1)
            else:
                even = pltpu.roll(even, HEAD_DIM, 1)
            o_ref[:, j * LANES:(j + 1) * LANES] = jnp.where(low, even, odd)


def _sample_attn(q, k_new, v_new, cache_t, *, dil=1, sinks=None):
    nb, chans, length = cache_t.shape
    wq, wkv = q.shape[1], k_new.shape[1]
    bb = max(1, min(8, 2048 // length))
    full = lambda w: pl.BlockSpec((nb, w), lambda s: (0, 0))
    tile = pl.BlockSpec((bb, chans, length), lambda s: (s, 0, 0))
    in_specs = [full(wq), full(wkv), full(wkv), tile]
    args = [q, k_new, v_new, cache_t]
    out_specs = [tile, full(wq)]
    out_shape = [jax.ShapeDtypeStruct(cache_t.shape, F32), jax.ShapeDtypeStruct((nb, wq), F32)]
    if sinks is None:
        kern = functools.partial(_sample_dilated_kernel, dil=dil, bb=bb)
        out_specs.append(full(wq))
        out_shape.append(jax.ShapeDtypeStruct((nb, wq), F32))
        scratch = []
    else:
        kern = functools.partial(_sample_swa_kernel, bb=bb)
        in_specs.append(pl.BlockSpec(sinks.shape, lambda s: (0, 0)))
        args.append(sinks)
        scratch = [pltpu.VMEM((B_Q_HEADS * nb, LANES), F32)] * 2
    return pl.pallas_call(
        kern, grid=(nb // bb,), in_specs=in_specs, out_specs=out_specs, out_shape=out_shape,
        scratch_shapes=scratch, compiler_params=_ARB(1),
        name=f"sample_attn_d{dil}_l{length}" if sinks is None else "sample_swa",
    )(*args)


def _rope_tables(pos):
    half = HEAD_DIM // 2
    inv = ROPE_THETA ** (-jnp.arange(half, dtype=F32) / half)
    ang = pos.astype(F32)[:, None] * jnp.tile(inv, 4)[None, :]
    sign = jnp.where(jnp.arange(LANES) < HEAD_DIM, -1.0, 1.0).astype(F32)
    return jnp.cos(ang), jnp.sin(ang) * sign[None, :]


def _to_tiles(cache):
    nb, length = cache.shape[0], cache.shape[1]
    return jnp.transpose(cache, (0, 2, 3, 4, 1)).reshape(nb, -1, length)


def _from_tiles(tiles, heads):
    nb, _, length = tiles.shape
    return jnp.transpose(tiles.reshape(nb, 2, heads, HEAD_DIM, length), (0, 4, 1, 2, 3))


def kernel(x_prompt, x_sample, c_prompt, c_sample, cache_a_kv_g0, cache_a_kv_g1, cache_a_kv_g2, cache_b_kv, ada_w,
           ada_b, g_pre, g_post, w_in_a, w_o_a, w_in_b, w_o_b, sinks_b, ada_kv_w, ada_kv_b, g_kv, w_kv):
    nbp, t, _ = x_prompt.shape
    nbs = x_sample.shape[0]
    assert x_sample.shape[1] == 1

    c_all = jnp.concatenate([c_prompt, c_sample], axis=0)
    mod = _ada(c_all, ada_w, ada_b)
    mod_kv = _ada(c_all, ada_kv_w[None], ada_kv_b[None])[0]

    def split(a, parts, sample):
        rows = a[nbp:][None] if sample else a[:nbp][:, None]
        return [rows[..., p * D_MODEL:(p + 1) * D_MODEL] for p in range(parts)]

    def paired(w, ncols):
        return jnp.concatenate([_paired_columns(w[:, :ncols]), w[:, ncols:]], axis=1).astype(BF16)

    w_in_a_b, w_o_a_b = paired(w_in_a[0], 2 * A_WIDTH), w_o_a[0].astype(BF16)
    w_in_b_b, w_o_b_b, w_kv_b = paired(w_in_b[0], B_WIDTH), w_o_b[0].astype(BF16), paired(w_kv, B_KV_WIDTH)
    g_pre0, g_pre1 = g_pre[0:1], g_pre[1:2]
    g_post0, g_post1 = g_post[0:1], g_post[1:2]
    g_kv_r = g_kv[None]
    sinks = jnp.broadcast_to(sinks_b[0][:, None] * LOG2E, (B_Q_HEADS, LANES))

    xs = x_sample.reshape(1, nbs, D_MODEL)
    cos_s, sin_s = _rope_tables(jnp.full((nbs,), PAST_LEN, jnp.int32))
    sh0s, sc0s, gt0s = split(mod[0], 3, True)
    sh1s, sc1s, gt1s = split(mod[1], 3, True)
    kshs, kscs = split(mod_kv, 2, True)
    qs, ks, vs, gzs = _inproj_a(xs, sh0s, sc0s, g_pre0, w_in_a_b, cos_s, sin_s, sample=True)
    tiles = [_to_tiles(c[0]) for c in (cache_a_kv_g0, cache_a_kv_g1, cache_a_kv_g2)]

    def job(g, first, count, per_step, prev_out=None):
        rows, cols = slice(first, first + count), slice(g * GROUP_WIDTH, (g + 1) * GROUP_WIDTH)
        return _ShiftJob(qs[0, rows, cols], ks[0, rows, cols], vs[0, rows, cols], tiles[g], first, per_step,
                         A_GROUPS[g][1], prev_out)

    proj_steps = nbp * (t // min(TOKEN_TILE, t))

    cos_p, sin_p = _rope_tables(jnp.arange(t))
    sh0, sc0, gt0 = split(mod[0], 3, False)
    sh1, sc1, gt1 = split(mod[1], 3, False)
    ksh, ksc = split(mod_kv, 2, False)
    *qkv, gz, ca0, ca1, ca2, shifted2, o2a, lse2a = _inproj_a(
        x_prompt, sh0, sc0, g_pre0, w_in_a_b, cos_p, sin_p, sample=False, job=job(2, 0, proj_steps, 1))
    g1_steps = nbp * _dilated_steps(A_GROUPS[1][1])[1]
    attn = [_dilated_attn(*qkv[0:3], 0),
            _dilated_attn(*qkv[3:6], 1, job=job(0, 0, nbs, nbs // g1_steps)),
            _dilated_attn(*qkv[6:9], 2)]
    os, lses = [a[0] for a in attn], [a[1] for a in attn]
    shifted0, o0, lse0 = attn[1][2:]
    x1, qb, gzb, kd, vd, cb, shifted1, o1, lse1 = _mid_prompt(
        os, lses, gz, x_prompt, gt0, g_post0, w_o_a_b, sh1, sc1, ksh, ksc, g_pre1, g_kv_r, w_in_b_b, w_kv_b,
        cos_p, sin_p, job=job(1, 0, nbs, nbs // proj_steps))
    y_prompt, shifted2, o2b, lse2b = _swa_attn(qb, kd, vd, sinks, gzb, x1, gt1, g_post1, w_o_b_b,
                                               job=job(2, proj_steps, nbs - proj_steps, 1, prev_out=shifted2))
    new_a_prompt = [_from_tiles(c, A_HEADS_PER_GROUP)[None] for c in (ca0, ca1, ca2)]
    new_b_prompt = _from_tiles(cb, B_KV_HEADS)

    os = [o0, o1, jnp.concatenate([o2a, o2b], axis=0)]
    lses = [lse0, lse1, jnp.concatenate([lse2a, lse2b], axis=0)]
    os = [o.astype(BF16)[None, None] for o in os]
    lses = [l[None, None] for l in lses]
    new_a_sample = [_from_tiles(s, A_HEADS_PER_GROUP)[None] for s in (shifted0, shifted1, shifted2)]
    sh1, sc1, gt1, ksh, ksc, gz, gt0 = sh1s, sc1s, gt1s, kshs, kscs, gzs, gt0s
    xs1 = _outproj(os, lses, gz, xs, gt0, g_post0, w_o_a_b)
    qb, gzb, kb, vb = _inproj_b_sample(xs1, sh1, sc1, ksh, ksc, g_pre1, g_kv_r, w_in_b_b, w_kv_b, cos_s, sin_s)
    shifted_b, ob = _sample_attn(qb[0], kb[0], vb[0], _to_tiles(cache_b_kv), sinks=sinks)
    y_sample = _outproj([ob.astype(BF16)[None]], None, gzb, xs1, gt1, g_post1, w_o_b_b).reshape(nbs, 1, D_MODEL)
    new_b_sample = _from_tiles(shifted_b, B_KV_HEADS)

    return (y_prompt, y_sample, *new_a_prompt, new_b_prompt, *new_a_sample, new_b_sample)
```

```python
import functools
from typing import NamedTuple, Optional

import jax
import jax.numpy as jnp
from jax import lax
from jax.experimental import pallas as pl
from jax.experimental.pallas import tpu as pltpu

D_MODEL = 1024
HEAD_DIM = 64
A_GROUPS = ((128, 1), (512, 4), (2048, 16))
A_HEADS_PER_GROUP = 4
GROUP_WIDTH = A_HEADS_PER_GROUP * HEAD_DIM
A_WIDTH = GROUP_WIDTH * len(A_GROUPS)
B_Q_HEADS = 16
B_KV_HEADS = 2
B_GROUP = B_Q_HEADS // B_KV_HEADS
B_WIDTH = B_Q_HEADS * HEAD_DIM
B_KV_WIDTH = B_KV_HEADS * HEAD_DIM
B_WINDOW = 128
BLOCK = 128
PAST_LEN = 16384
ROPE_THETA = 10000.0
EPS = 1e-6
NEG = -1e30
LANES = 128
LOG2E = 1.4426950408889634
LN2 = 0.6931471805599453
Q_SCALE = HEAD_DIM ** -0.5 * LOG2E

BF16 = jnp.bfloat16
F32 = jnp.float32
TOKEN_TILE = 512

_ARB = lambda n: pltpu.CompilerParams(dimension_semantics=("arbitrary",) * n)


HALF = HEAD_DIM // 2
PAIRED_QUARTERS = (0, 2, 1, 3)


def _paired_columns(w):
    d, n = w.shape
    return w.reshape(d, n // LANES, 4, HALF)[:, :, PAIRED_QUARTERS, :].reshape(d, n)


def _rope(x, cos, sin_signed):
    outs = []
    for j in range(x.shape[1] // LANES):
        xs = x[:, j * LANES:(j + 1) * LANES]
        outs.append(xs * cos + pltpu.roll(xs, HEAD_DIM, 1) * sin_signed)
    return outs[0] if len(outs) == 1 else jnp.concatenate(outs, axis=1)


def _standard_lanes(x):
    quarter = lax.broadcasted_iota(jnp.int32, (1, LANES), 1) // HALF
    outs = []
    for j in range(x.shape[1] // LANES):
        xs = x[:, j * LANES:(j + 1) * LANES]
        outs.append(jnp.where(quarter == 1, pltpu.roll(xs, LANES - HALF, 1),
                              jnp.where(quarter == 2, pltpu.roll(xs, HALF, 1), xs)))
    return outs[0] if len(outs) == 1 else jnp.concatenate(outs, axis=1)


def _store_standard_rows(ref, xt, dst):
    for s in range(xt.shape[0] // LANES):
        for quarter, src in enumerate(PAIRED_QUARTERS):
            row, src_row = s * LANES + quarter * HALF, s * LANES + src * HALF
            ref[0, row:row + HALF, dst] = xt[src_row:src_row + HALF]


def _normalize(x):
    return x * lax.rsqrt(jnp.mean(x * x, axis=-1, keepdims=True) + EPS)


def _dot(a, b):
    return jnp.dot(a, b, preferred_element_type=F32)


def _ada_kernel(c_ref, w_ref, b_ref, o_ref):
    s = jax.nn.silu(c_ref[...]).astype(BF16)
    o_ref[...] = _dot(s, w_ref[...].astype(BF16)) + b_ref[...]


def _ada(c, w, b):
    nl, _, n = w.shape
    m = c.shape[0]
    tn = 1024
    return pl.pallas_call(
        _ada_kernel,
        grid=(nl, n // tn),
        in_specs=[
            pl.BlockSpec((m, D_MODEL), lambda l, j: (0, 0)),
            pl.BlockSpec((None, D_MODEL, tn), lambda l, j: (l, 0, j)),
            pl.BlockSpec((None, 1, tn), lambda l, j: (l, 0, j)),
        ],
        out_specs=pl.BlockSpec((None, m, tn), lambda l, j: (l, 0, j)),
        out_shape=jax.ShapeDtypeStruct((nl, m, n), F32),
        compiler_params=_ARB(2),
        name="ada",
    )(c, w, b.reshape(nl, 1, n))


ROW_CHUNK = 512


def _row_chunks(tm):
    size = min(ROW_CHUNK, tm)
    return [slice(c * size, (c + 1) * size) for c in range(tm // size)]


def _mod_rows(ref, rows):
    return ref[0] if ref.shape[1] == 1 else ref[0, rows, :]


def _inproj_a_body(x_ref, sh_ref, sc_ref, g_ref, w_ref, cos_ref, sin_ref, rows):
    h = _normalize(x_ref[0, rows, :]) * g_ref[...]
    hb = (h * (1.0 + _mod_rows(sc_ref, rows)) + _mod_rows(sh_ref, rows)).astype(BF16)
    cos, sin = cos_ref[rows, :], sin_ref[rows, :]
    q = _rope(_dot(hb, w_ref[:, 0:A_WIDTH]), cos, sin) * Q_SCALE
    k = _rope(_dot(hb, w_ref[:, A_WIDTH:2 * A_WIDTH]), cos, sin)
    v = _dot(hb, w_ref[:, 2 * A_WIDTH:3 * A_WIDTH])
    gz = jax.nn.silu(_dot(hb, w_ref[:, 3 * A_WIDTH:4 * A_WIDTH]))
    return q, k, v, gz


def _inproj_a_prompt_kernel(x_ref, sh_ref, sc_ref, g_ref, w_ref, cos_ref, sin_ref, *rest, tm, nt, job):
    job_in, main_out, job_out, (scr,) = _split_job_refs(rest, job, 13)
    qkv_refs, (gz_ref, c0_ref, c1_ref, c2_ref) = main_out[:9], main_out[9:]
    i = pl.program_id(1)
    if job:
        _run_job(job, job_in, job_out, pl.program_id(0) * nt + i)
    slabs_per_group = GROUP_WIDTH // LANES
    for c, rows in enumerate(_row_chunks(tm)):
        n = rows.stop - rows.start
        q, k, v, gz = _inproj_a_body(x_ref, sh_ref, sc_ref, g_ref, w_ref, cos_ref, sin_ref, rows)
        gz_ref[0, rows, :] = gz.astype(BF16)
        for which, val in enumerate((q, k, v)):
            for j in range(A_WIDTH // LANES):
                scr[c, which, j] = val[:, j * LANES:(j + 1) * LANES]
            for g, (_, dil) in enumerate(A_GROUPS):
                out_ref = qkv_refs[3 * g + which]
                dst = slice(rows.start // dil, rows.stop // dil)
                for r in range(dil):
                    for h in range(slabs_per_group):
                        part = scr[c, which, g * slabs_per_group + h, pl.ds(r, n // dil, stride=dil), :]
                        out_ref[0, r, dst, h * LANES:(h + 1) * LANES] = part.astype(BF16)
        for g, c_ref in enumerate((c0_ref, c1_ref, c2_ref)):
            win = A_GROUPS[g][0]
            width = min(tm, win)
            first_tile = nt - max(win // tm, 1)
            lo = max(rows.start, tm - width)
            if lo >= rows.stop:
                continue

            @pl.when(i >= first_tile)
            def _(g=g, c_ref=c_ref, lo=lo, k=k, v=v, rows=rows, width=width):
                cols = slice(g * GROUP_WIDTH, (g + 1) * GROUP_WIDTH)
                dst = slice(lo - (tm - width), rows.stop - (tm - width))
                _store_standard_rows(c_ref, k[lo - rows.start:, cols].T, dst)
                c_ref[0, GROUP_WIDTH:2 * GROUP_WIDTH, dst] = v[lo - rows.start:, cols].T


def _inproj_a_sample_kernel(x_ref, sh_ref, sc_ref, g_ref, w_ref, cos_ref, sin_ref,
                            q_ref, k_ref, v_ref, gz_ref):
    q, k, v, gz = _inproj_a_body(x_ref, sh_ref, sc_ref, g_ref, w_ref, cos_ref, sin_ref, slice(0, x_ref.shape[1]))
    q_ref[0] = _standard_lanes(q)
    k_ref[0] = _standard_lanes(k)
    v_ref[0] = v
    gz_ref[0] = gz.astype(BF16)


def _mod_spec(mod, tm):
    if mod.shape[1] == 1:
        return pl.BlockSpec((1, 1, D_MODEL), lambda b, i: (b, 0, 0))
    return pl.BlockSpec((1, tm, D_MODEL), lambda b, i: (b, i, 0))


def _inproj_a(x, shift, scale, g, w, cos, sin, *, sample, job=None):
    nb, t, _ = x.shape
    tm = min(TOKEN_TILE, t)
    nt = t // tm
    row = lambda width: pl.BlockSpec((1, tm, width), lambda b, i: (b, i, 0))
    in_specs = [
        row(D_MODEL), _mod_spec(shift, tm), _mod_spec(scale, tm),
        pl.BlockSpec((1, D_MODEL), lambda b, i: (0, 0)),
        pl.BlockSpec((D_MODEL, 4 * A_WIDTH), lambda b, i: (0, 0)),
        pl.BlockSpec((tm, LANES), lambda b, i: (i, 0)),
        pl.BlockSpec((tm, LANES), lambda b, i: (i, 0)),
    ]
    if sample:
        out_shape = [jax.ShapeDtypeStruct((nb, t, A_WIDTH), dt) for dt in (F32, F32, F32, BF16)]
        return pl.pallas_call(
            _inproj_a_sample_kernel, grid=(nb, nt), in_specs=in_specs,
            out_specs=[row(A_WIDTH)] * 4, out_shape=out_shape,
            compiler_params=_ARB(2), name="inproj_a_sample",
        )(x, shift, scale, g, w, cos, sin)
    out_shape, out_specs = [], []
    for _, dil in A_GROUPS:
        out_shape += [jax.ShapeDtypeStruct((nb, dil, t // dil, GROUP_WIDTH), BF16)] * 3
        out_specs += [pl.BlockSpec((1, dil, tm // dil, GROUP_WIDTH), lambda b, i: (b, 0, i, 0))] * 3
    out_shape.append(jax.ShapeDtypeStruct((nb, t, A_WIDTH), BF16))
    out_specs.append(row(A_WIDTH))
    for win, _ in A_GROUPS:
        rows = min(tm, win)
        first_tile = nt - max(win // tm, 1)
        out_shape.append(jax.ShapeDtypeStruct((nb, 2 * GROUP_WIDTH, min(win, t)), F32))
        out_specs.append(pl.BlockSpec(
            (1, 2 * GROUP_WIDTH, rows),
            lambda b, i, first_tile=first_tile: (b, 0, jnp.maximum(i - first_tile, 0))))
    args, aliases = [x, shift, scale, g, w, cos, sin], {}
    if job is not None:
        assert job.q.shape[0] == nb * nt * job.per_step
        j_in, j_args, j_out, j_shape, aliases = _job_operands(job, lambda b, i: b * nt + i, len(args),
                                                              len(out_specs))
        in_specs, args = in_specs + j_in, args + j_args
        out_specs, out_shape = out_specs + j_out, out_shape + j_shape
    return pl.pallas_call(
        functools.partial(_inproj_a_prompt_kernel, tm=tm, nt=nt, job=job.static if job else None),
        grid=(nb, nt), in_specs=in_specs, out_specs=out_specs, out_shape=out_shape,
        scratch_shapes=[pltpu.VMEM((tm // min(ROW_CHUNK, tm), 3, A_WIDTH // LANES, min(ROW_CHUNK, tm), LANES), F32)],
        input_output_aliases=aliases, compiler_params=_ARB(2), name="inproj_a_prompt",
    )(*args)


def _query_minus_key():
    kj = lax.broadcasted_iota(jnp.int32, (2 * BLOCK, BLOCK), 0)
    qi = lax.broadcasted_iota(jnp.int32, (2 * BLOCK, BLOCK), 1)
    return qi - kj


def _head_attend(st, mask, vt_ext, sink):
    st = jnp.where(mask, st, NEG)
    m = jnp.max(st, axis=0, keepdims=True)
    if sink is not None:
        m = jnp.maximum(m, sink)
    ext = _dot(vt_ext, jnp.exp2(st - m).astype(BF16))
    l = ext[HEAD_DIM:HEAD_DIM + 1, :]
    if sink is not None:
        l = l + jnp.exp2(sink - m)
    return ext[0:HEAD_DIM, :] * (1.0 / l), m, l


def _values_ext(vt_prev, vt_cur, head):
    rows = slice(head * HEAD_DIM, (head + 1) * HEAD_DIM)
    ones = jnp.ones((16, 2 * BLOCK), BF16)
    return jnp.concatenate([jnp.concatenate([vt_prev[rows], vt_cur[rows]], axis=1), ones], axis=0)


def _block_diag_queries(qt, heads, kv_of_head, kv_heads):
    zeros = jnp.zeros((HALF, BLOCK), BF16)
    cols = []
    for h in range(heads):
        src = (h // 2) * LANES + (h % 2) * HALF
        lo, hi = qt[src:src + HALF], qt[src + HEAD_DIM:src + HEAD_DIM + HALF]
        kv = kv_of_head(h)
        pieces = []
        for slab in range(kv_heads // 2):
            for quarter in range(4):
                mine = slab == kv // 2 and quarter % 2 == kv % 2
                pieces.append((lo if quarter < 2 else hi) if mine else zeros)
        cols.append(jnp.concatenate(pieces, axis=0))
    return jnp.concatenate(cols, axis=1)


ATTN_BLOCKS_IN_FLIGHT = 16


def _dilated_attn_kernel(q_ref, k_ref, v_ref, *rest, job):
    job_in, (o_ref, lse_ref), job_out, (vt_ref,) = _split_job_refs(rest, job, 2)
    if job:
        _run_job(job, job_in, job_out, pl.program_id(0) * pl.num_programs(1) + pl.program_id(1))
    nres, m_rows = q_ref.shape[1], q_ref.shape[2]
    nblk = m_rows // BLOCK
    base = _query_minus_key()
    heads = A_HEADS_PER_GROUP

    def block(r, i):
        prev = jnp.maximum(i - 1, 0)
        dist = base + (i - prev) * BLOCK
        mask = (dist >= 0) & (dist <= BLOCK)
        qrows = pl.ds(pl.multiple_of(i * BLOCK, BLOCK), BLOCK)
        k = k_ref[0, r, pl.ds(pl.multiple_of(prev * BLOCK, BLOCK), 2 * BLOCK), :]
        qd = _block_diag_queries(q_ref[0, r, qrows, :].T, heads, lambda h: h, heads)
        st_all = _dot(k, qd)
        vt_cur = v_ref[0, r, qrows, :].T
        vt_ref[r, i] = vt_cur
        vt_prev = vt_ref[r, prev]
        outs, lses = [], []
        for h in range(heads):
            o_t, m, l = _head_attend(st_all[:, h * BLOCK:(h + 1) * BLOCK], mask,
                                     _values_ext(vt_prev, vt_cur, h), None)
            outs.append(o_t)
            lses.append(jnp.broadcast_to(m * LN2 + jnp.log(l), (HEAD_DIM, BLOCK)))
        o_ref[0, r, qrows, :] = jnp.concatenate(outs, axis=0).T.astype(BF16)
        lse_ref[0, r, qrows, :] = jnp.concatenate(lses, axis=0).T

    if nres * nblk <= ATTN_BLOCKS_IN_FLIGHT:
        for r in range(nres):
            for i in range(nblk):
                block(r, i)
    else:
        assert nres == 1 and nblk % ATTN_BLOCKS_IN_FLIGHT == 0

        def chunk(c, carry):
            for u in range(ATTN_BLOCKS_IN_FLIGHT):
                block(0, c * ATTN_BLOCKS_IN_FLIGHT + u)
            return carry

        lax.fori_loop(0, nblk // ATTN_BLOCKS_IN_FLIGHT, chunk, 0)


def _dilated_steps(dil):
    nres = max(1, dil // 2)
    return nres, dil // nres


def _dilated_attn(q, k, v, g, job=None):
    nb, dil, m_rows, _ = q.shape
    nres, nsteps = _dilated_steps(dil)
    blk = pl.BlockSpec((1, nres, m_rows, GROUP_WIDTH), lambda b, r: (b, r, 0, 0))
    in_specs, args, aliases = [blk, blk, blk], [q, k, v], {}
    out_specs = [blk, blk]
    out_shape = [jax.ShapeDtypeStruct(q.shape, BF16), jax.ShapeDtypeStruct(q.shape, F32)]
    if job is not None:
        assert job.q.shape[0] == nb * nsteps * job.per_step
        j_in, j_args, j_out, j_shape, aliases = _job_operands(job, lambda b, r: b * nsteps + r, len(args), 2)
        in_specs, args = in_specs + j_in, args + j_args
        out_specs, out_shape = out_specs + j_out, out_shape + j_shape
    return pl.pallas_call(
        functools.partial(_dilated_attn_kernel, job=job.static if job else None),
        grid=(nb, nsteps), in_specs=in_specs, out_specs=out_specs, out_shape=out_shape,
        scratch_shapes=[pltpu.VMEM((nres, m_rows // BLOCK, GROUP_WIDTH, BLOCK), BF16)],
        input_output_aliases=aliases, compiler_params=_ARB(2), name=f"dilated_attn_g{g}",
    )(*args)


SWA_QUERY_TILE = 512


def _swa_attn_kernel(q_ref, k_ref, v_ref, sink_ref, gz_ref, x_ref, gate_ref, g_ref, w_ref, *rest, job):
    job_in, (y_ref,), job_out, (vt_ref, o_scr) = _split_job_refs(rest, job, 1)
    j = pl.program_id(1)
    nq = q_ref.shape[1] // BLOCK
    base = _query_minus_key()
    if job:
        new_cols = _run_job(job, job_in, job_out, pl.program_id(0) * pl.num_programs(1) + j, shift_now=False)

    for i in range(nq):
        cur = j * nq + i
        prev = jnp.maximum(cur - 1, 0)
        dist = base + (cur - prev) * BLOCK
        mask = (dist >= 0) & (dist < B_WINDOW)
        qrows = slice(i * BLOCK, (i + 1) * BLOCK)
        k = k_ref[0, pl.ds(pl.multiple_of(prev * BLOCK, BLOCK), 2 * BLOCK), :]
        qt = q_ref[0, qrows, :].T
        vt_cur = v_ref[0, pl.ds(pl.multiple_of(cur * BLOCK, BLOCK), BLOCK), :].T
        vt_ref[cur] = vt_cur
        vt_prev = vt_ref[prev]
        outs = []
        for kvh in range(B_KV_HEADS):
            qd = _block_diag_queries(qt[kvh * B_GROUP * HEAD_DIM:(kvh + 1) * B_GROUP * HEAD_DIM], B_GROUP,
                                     lambda h, kvh=kvh: kvh, B_KV_HEADS)
            st_all = _dot(k, qd)
            vt_ext = _values_ext(vt_prev, vt_cur, kvh)
            for gq in range(B_GROUP):
                hq = kvh * B_GROUP + gq
                o_t, _, _ = _head_attend(st_all[:, gq * BLOCK:(gq + 1) * BLOCK], mask, vt_ext,
                                         sink_ref[hq:hq + 1, :])
                outs.append(o_t)
        o_scr[qrows, :] = jnp.concatenate(outs, axis=0).T.astype(BF16)
        if job:
            _job_shift_part(job, job_in, job_out, new_cols, i, nq)
    a = (o_scr[...].astype(F32) * gz_ref[0].astype(F32)).astype(BF16)
    _finish([a], w_ref, x_ref, gate_ref, g_ref, y_ref)


def _swa_attn(q, k, v, sinks, gz, x, gate, g, w, job=None):
    nb, t, _ = q.shape
    tq = min(SWA_QUERY_TILE, t)
    nj = t // tq
    qblk = pl.BlockSpec((1, tq, B_WIDTH), lambda b, j: (b, j, 0))
    xblk = pl.BlockSpec((1, tq, D_MODEL), lambda b, j: (b, j, 0))
    kvblk = pl.BlockSpec((1, t, B_KV_WIDTH), lambda b, j: (b, 0, 0))
    in_specs = [qblk, kvblk, kvblk, pl.BlockSpec((B_Q_HEADS, LANES), lambda b, j: (0, 0)),
                qblk, xblk, _mod_spec(gate, tq), pl.BlockSpec((1, D_MODEL), lambda b, j: (0, 0)),
                pl.BlockSpec((B_WIDTH, D_MODEL), lambda b, j: (0, 0))]
    args, aliases = [q, k, v, sinks, gz, x, gate, g, w], {}
    out_specs, out_shape = [xblk], [jax.ShapeDtypeStruct((nb, t, D_MODEL), F32)]
    if job is not None:
        assert job.q.shape[0] == nb * nj * job.per_step
        j_in, j_args, j_out, j_shape, aliases = _job_operands(job, lambda b, j: b * nj + j, len(args), 1)
        in_specs, args = in_specs + j_in, args + j_args
        out_specs, out_shape = out_specs + j_out, out_shape + j_shape
    return pl.pallas_call(
        functools.partial(_swa_attn_kernel, job=job.static if job else None),
        grid=(nb, nj), in_specs=in_specs, out_specs=out_specs, out_shape=out_shape,
        scratch_shapes=[pltpu.VMEM((t // BLOCK, B_KV_WIDTH, BLOCK), BF16), pltpu.VMEM((tq, B_WIDTH), BF16)],
        input_output_aliases=aliases, compiler_params=_ARB(2), name="swa_attn",
    )(*args)


def _finish_value(a_parts, w_ref, x, gate, g):
    acc = None
    row = 0
    for a in a_parts:
        part = _dot(a, w_ref[row:row + a.shape[1], :])
        acc = part if acc is None else acc + part
        row += a.shape[1]
    return x + gate * (_normalize(acc) * g)


def _finish(a_parts, w_ref, x_ref, gate_ref, g_ref, y_ref):
    y_ref[0] = _finish_value(a_parts, w_ref, x_ref[0], gate_ref[0], g_ref[...])


def _in_row_order(ref, scr):
    dil = ref.shape[1]
    if dil == 1:
        return ref[0, 0].astype(F32)
    nslab = ref.shape[3] // LANES
    for r in range(dil):
        for h in range(nslab):
            scr[h, pl.ds(r, ref.shape[2], stride=dil), :] = ref[0, r, :, h * LANES:(h + 1) * LANES].astype(F32)
    return jnp.concatenate([scr[h] for h in range(nslab)], axis=1)


def _mixture_parts(o_refs, l_refs, gz_ref, scratch):
    os = [_in_row_order(r, s) for r, s in zip(o_refs, scratch[0:3])]
    lses = [_in_row_order(r, s) for r, s in zip(l_refs, scratch[3:6])]
    top = jnp.maximum(jnp.maximum(lses[0], lses[1]), lses[2])
    es = [jnp.exp(l - top) for l in lses]
    inv = 1.0 / (es[0] + es[1] + es[2])
    parts = []
    for g in range(len(A_GROUPS)):
        gz = gz_ref[0, :, g * GROUP_WIDTH:(g + 1) * GROUP_WIDTH].astype(F32)
        parts.append((os[g] * (es[g] * inv) * gz).astype(BF16))
    return parts


def _outproj_mix_kernel(o0_ref, o1_ref, o2_ref, l0_ref, l1_ref, l2_ref, gz_ref, x_ref, gate_ref, g_ref, w_ref,
                        y_ref, *scratch):
    parts = _mixture_parts((o0_ref, o1_ref, o2_ref), (l0_ref, l1_ref, l2_ref), gz_ref, scratch)
    _finish(parts, w_ref, x_ref, gate_ref, g_ref, y_ref)


def _outproj_kernel(o_ref, gz_ref, x_ref, gate_ref, g_ref, w_ref, y_ref):
    a = (o_ref[0].astype(F32) * gz_ref[0].astype(F32)).astype(BF16)
    _finish([a], w_ref, x_ref, gate_ref, g_ref, y_ref)


def _outproj(os, lses, gz, x, gate, g, w):
    nb, t, _ = x.shape
    tm = min(TOKEN_TILE, t)
    row = lambda width: pl.BlockSpec((1, tm, width), lambda b, i: (b, i, 0))
    width = w.shape[0]
    tail_specs = [row(width), row(D_MODEL), _mod_spec(gate, tm),
                  pl.BlockSpec((1, D_MODEL), lambda b, i: (0, 0)),
                  pl.BlockSpec((width, D_MODEL), lambda b, i: (0, 0))]
    scratch = []
    if lses is None:
        kern, name = _outproj_kernel, "outproj_b"
        in_specs = [row(width)] + tail_specs
        args = (os[0], gz, x, gate, g, w)
    else:
        kern, name = _outproj_mix_kernel, "outproj_a"
        split = lambda a: pl.BlockSpec((1, a.shape[1], tm // a.shape[1], GROUP_WIDTH), lambda b, i: (b, 0, i, 0))
        in_specs = [split(a) for a in (*os, *lses)] + tail_specs
        args = (*os, *lses, gz, x, gate, g, w)
        scratch = [pltpu.VMEM((GROUP_WIDTH // LANES, tm, LANES), F32)] * 6
    return pl.pallas_call(
        kern, grid=(nb, t // tm), in_specs=in_specs, out_specs=row(D_MODEL),
        out_shape=jax.ShapeDtypeStruct((nb, t, D_MODEL), F32), scratch_shapes=scratch,
        compiler_params=_ARB(2), name=name,
    )(*args)


def _inproj_b_body(x, sh_ref, sc_ref, ksh_ref, ksc_ref, g_ref, gkv_ref, w_ref, wkv_ref, cos_ref, sin_ref, rows):
    xn = _normalize(x)
    hb = ((xn * g_ref[...]) * (1.0 + _mod_rows(sc_ref, rows)) + _mod_rows(sh_ref, rows)).astype(BF16)
    hk = ((xn * gkv_ref[...]) * (1.0 + _mod_rows(ksc_ref, rows)) + _mod_rows(ksh_ref, rows)).astype(BF16)
    cos, sin = cos_ref[rows, :], sin_ref[rows, :]
    q = _rope(_dot(hb, w_ref[:, 0:B_WIDTH]), cos, sin) * Q_SCALE
    gz = jax.nn.silu(_dot(hb, w_ref[:, B_WIDTH:2 * B_WIDTH]))
    kv = _dot(hk, wkv_ref[...])
    k = _rope(kv[:, 0:B_KV_WIDTH], cos, sin)
    v = kv[:, B_KV_WIDTH:2 * B_KV_WIDTH]
    return q, gz, k, v


def _mid_prompt_kernel(o0_ref, o1_ref, o2_ref, l0_ref, l1_ref, l2_ref, gza_ref, x_ref, gate_ref, gpost_ref, wo_ref,
                       sh_ref, sc_ref, ksh_ref, ksc_ref, g_ref, gkv_ref, w_ref, wkv_ref, cos_ref, sin_ref,
                       *rest, tm, nt, job):
    job_in, (x1_ref, q_ref, gz_ref, kd_ref, vd_ref, c_ref), job_out, scratch = _split_job_refs(rest, job, 6)
    if job:
        _run_job(job, job_in, job_out, pl.program_id(0) * nt + pl.program_id(1))
    parts = _mixture_parts((o0_ref, o1_ref, o2_ref), (l0_ref, l1_ref, l2_ref), gza_ref, scratch)
    x1 = _finish_value(parts, wo_ref, x_ref[0], gate_ref[0], gpost_ref[...])
    x1_ref[0] = x1
    q, gz, k, v = _inproj_b_body(x1, sh_ref, sc_ref, ksh_ref, ksc_ref, g_ref, gkv_ref, w_ref, wkv_ref,
                                 cos_ref, sin_ref, slice(0, tm))
    q_ref[0] = q.astype(BF16)
    gz_ref[0] = gz.astype(BF16)
    kd_ref[0] = k.astype(BF16)
    vd_ref[0] = v.astype(BF16)

    @pl.when(pl.program_id(1) == nt - 1)
    def _():
        _store_standard_rows(c_ref, k[tm - B_WINDOW:, :].T, slice(None))
        c_ref[0, B_KV_WIDTH:2 * B_KV_WIDTH, :] = v[tm - B_WINDOW:, :].T


def _inproj_b_sample_kernel(x_ref, sh_ref, sc_ref, ksh_ref, ksc_ref, g_ref, gkv_ref, w_ref, wkv_ref, cos_ref,
                            sin_ref, q_ref, gz_ref, k_ref, v_ref):
    q, gz, k, v = _inproj_b_body(x_ref[0], sh_ref, sc_ref, ksh_ref, ksc_ref, g_ref, gkv_ref, w_ref, wkv_ref,
                                 cos_ref, sin_ref, slice(0, x_ref.shape[1]))
    q_ref[0] = _standard_lanes(q)
    gz_ref[0] = gz.astype(BF16)
    k_ref[0] = _standard_lanes(k)
    v_ref[0] = v


def _inproj_b_sample(x, shift, scale, kshift, kscale, g, gkv, w, wkv, cos, sin):
    nb, t, _ = x.shape
    tm = min(TOKEN_TILE, t)
    row = lambda width: pl.BlockSpec((1, tm, width), lambda b, i: (b, i, 0))
    vec = pl.BlockSpec((1, D_MODEL), lambda b, i: (0, 0))
    tab = pl.BlockSpec((tm, LANES), lambda b, i: (i, 0))
    in_specs = [row(D_MODEL), _mod_spec(shift, tm), _mod_spec(scale, tm), _mod_spec(kshift, tm),
                _mod_spec(kscale, tm), vec, vec,
                pl.BlockSpec((D_MODEL, 2 * B_WIDTH), lambda b, i: (0, 0)),
                pl.BlockSpec((D_MODEL, 2 * B_KV_WIDTH), lambda b, i: (0, 0)), tab, tab]
    return pl.pallas_call(
        _inproj_b_sample_kernel, grid=(nb, t // tm), in_specs=in_specs,
        out_specs=[row(B_WIDTH), row(B_WIDTH), row(B_KV_WIDTH), row(B_KV_WIDTH)],
        out_shape=[jax.ShapeDtypeStruct((nb, t, B_WIDTH), F32), jax.ShapeDtypeStruct((nb, t, B_WIDTH), BF16),
                   jax.ShapeDtypeStruct((nb, t, B_KV_WIDTH), F32), jax.ShapeDtypeStruct((nb, t, B_KV_WIDTH), F32)],
        compiler_params=_ARB(2), name="inproj_b_sample",
    )(x, shift, scale, kshift, kscale, g, gkv, w, wkv, cos, sin)


def _mid_prompt(os, lses, gza, x, gate, gpost, wo, shift, scale, kshift, kscale, g, gkv, w, wkv, cos, sin, job=None):
    nb, t, _ = x.shape
    tm = min(TOKEN_TILE, t)
    nt = t // tm
    row = lambda width: pl.BlockSpec((1, tm, width), lambda b, i: (b, i, 0))
    vec = pl.BlockSpec((1, D_MODEL), lambda b, i: (0, 0))
    tab = pl.BlockSpec((tm, LANES), lambda b, i: (i, 0))
    full = lambda a: pl.BlockSpec(a.shape, lambda b, i: (0, 0))
    split = lambda a: pl.BlockSpec((1, a.shape[1], tm // a.shape[1], GROUP_WIDTH), lambda b, i: (b, 0, i, 0))
    in_specs = ([split(a) for a in (*os, *lses)]
                + [row(A_WIDTH), row(D_MODEL), _mod_spec(gate, tm), vec, full(wo)]
                + [_mod_spec(m, tm) for m in (shift, scale, kshift, kscale)] + [vec, vec, full(w), full(wkv), tab, tab])
    args = [*os, *lses, gza, x, gate, gpost, wo, shift, scale, kshift, kscale, g, gkv, w, wkv, cos, sin]
    out_specs = [row(D_MODEL), row(B_WIDTH), row(B_WIDTH), row(B_KV_WIDTH), row(B_KV_WIDTH),
                 pl.BlockSpec((1, 2 * B_KV_WIDTH, B_WINDOW), lambda b, i: (b, 0, 0))]
    out_shape = [jax.ShapeDtypeStruct((nb, t, D_MODEL), F32),
                 jax.ShapeDtypeStruct((nb, t, B_WIDTH), BF16), jax.ShapeDtypeStruct((nb, t, B_WIDTH), BF16),
                 jax.ShapeDtypeStruct((nb, t, B_KV_WIDTH), BF16), jax.ShapeDtypeStruct((nb, t, B_KV_WIDTH), BF16),
                 jax.ShapeDtypeStruct((nb, 2 * B_KV_WIDTH, B_WINDOW), F32)]
    aliases = {}
    if job is not None:
        assert job.q.shape[0] == nb * nt * job.per_step
        j_in, j_args, j_out, j_shape, aliases = _job_operands(job, lambda b, i: b * nt + i, len(args),
                                                              len(out_specs))
        in_specs, args = in_specs + j_in, args + j_args
        out_specs, out_shape = out_specs + j_out, out_shape + j_shape
    return pl.pallas_call(
        functools.partial(_mid_prompt_kernel, tm=tm, nt=nt, job=job.static if job else None),
        grid=(nb, nt), in_specs=in_specs, out_specs=out_specs, out_shape=out_shape,
        scratch_shapes=[pltpu.VMEM((GROUP_WIDTH // LANES, tm, LANES), F32)] * 6,
        input_output_aliases=aliases, compiler_params=_ARB(2), name="mid_prompt",
    )(*args)


def _eye():
    r = lax.broadcasted_iota(jnp.int32, (LANES, LANES), 0)
    c = lax.broadcasted_iota(jnp.int32, (LANES, LANES), 1)
    return r == c


def _row_to_col(row):
    eye = _eye()
    chunks = [jnp.sum(jnp.where(eye, row[:, j * LANES:(j + 1) * LANES], 0.0), axis=1, keepdims=True)
              for j in range(row.shape[1] // LANES)]
    return chunks[0] if len(chunks) == 1 else jnp.concatenate(chunks, axis=0)


def _shift_rows(cache_ref, out_ref, e, new_col, part=0, nparts=1):
    ntile = cache_ref.shape[2] // LANES
    first, last = part * ntile // nparts, (part + 1) * ntile // nparts
    if first == last:
        return
    lane = lax.broadcasted_iota(jnp.int32, (1, LANES), 1)
    nxt = pltpu.roll(cache_ref[e, :, first * LANES:(first + 1) * LANES], LANES - 1, 1)
    for j in range(first, last):
        cur = nxt
        if j + 1 < ntile:
            nxt = pltpu.roll(cache_ref[e, :, (j + 1) * LANES:(j + 2) * LANES], LANES - 1, 1)
            fill = nxt
        else:
            fill = new_col
        out_ref[e, :, j * LANES:(j + 1) * LANES] = jnp.where(lane < LANES - 1, cur, fill)


_NT = (((1,), (1,)), ((), ()))


def _sample_dilated_step(q_ref, kn_ref, vn_ref, cache_ref, out_cache_ref, o_ref, lse_ref, step, *, dil, bb,
                         shift_now=True):
    width, length = q_ref.shape[1], cache_ref.shape[2]
    new_cols = []
    sel = (lax.broadcasted_iota(jnp.int32, (8, width), 0)
           == lax.broadcasted_iota(jnp.int32, (8, width), 1) // HEAD_DIM)
    valid = lax.broadcasted_iota(jnp.int32, (1, length), 1) % dil == 0
    per_head = lambda x: jnp.sum(jnp.where(sel, x, 0.0), axis=0, keepdims=True)
    for e in range(bb):
        row = pl.ds(step * bb + e, 1)
        q, kn, vn = q_ref[row, :], kn_ref[row, :], vn_ref[row, :]
        qb = jnp.where(sel, q, 0.0)
        kt = cache_ref[e, 0:width, :].astype(BF16)
        vt = cache_ref[e, width:2 * width, :].astype(BF16)
        s = jnp.where(valid, _dot(qb.astype(BF16), kt), NEG)
        s_new = jnp.sum(qb * kn, axis=1, keepdims=True)
        m = jnp.maximum(jnp.max(s, axis=1, keepdims=True), s_new)
        p = jnp.exp2(s - m)
        p_new = jnp.exp2(s_new - m)
        l = per_head(jnp.sum(p, axis=1, keepdims=True) + p_new)
        o = lax.dot_general(p.astype(BF16), vt, _NT, preferred_element_type=F32)
        o_ref[row, :] = (per_head(o) + per_head(p_new) * vn) * (1.0 / l)
        lse_ref[row, :] = per_head(m) * LN2 + jnp.log(l)
        new_cols.append(jnp.concatenate([_row_to_col(kn), _row_to_col(vn)], axis=0))
        if shift_now:
            _shift_rows(cache_ref, out_cache_ref, e, new_cols[-1])
    return new_cols


def _sample_dilated_kernel(q_ref, kn_ref, vn_ref, cache_ref, out_cache_ref, o_ref, lse_ref, *, dil, bb):
    _sample_dilated_step(q_ref, kn_ref, vn_ref, cache_ref, out_cache_ref, o_ref, lse_ref, pl.program_id(0),
                         dil=dil, bb=bb)


class _ShiftJob(NamedTuple):
    q: jax.Array
    k_new: jax.Array
    v_new: jax.Array
    cache: jax.Array
    first: int
    per_step: int
    dil: int
    prev_out: Optional[jax.Array]

    @property
    def static(self):
        return (4 if self.prev_out is None else 5, self.dil, self.per_step)


def _job_operands(job, step_of, inputs_before, outputs_before):
    count, wq = job.q.shape
    _, chans, length = job.cache.shape
    first_blk = job.first // job.per_step
    full = pl.BlockSpec((count, wq), lambda *ids: (0, 0))
    tile = pl.BlockSpec((job.per_step, chans, length), lambda *ids: (first_blk + step_of(*ids), 0, 0))
    in_specs, args, aliases = [full, full, full, tile], [job.q, job.k_new, job.v_new, job.cache], {}
    if job.prev_out is not None:
        in_specs.append(pl.BlockSpec(memory_space=pl.ANY))
        args.append(job.prev_out)
        aliases[inputs_before + 4] = outputs_before
    out_shape = [jax.ShapeDtypeStruct(job.cache.shape, F32)] + [jax.ShapeDtypeStruct((count, wq), F32)] * 2
    return in_specs, args, [tile, full, full], out_shape, aliases


def _split_job_refs(rest, job_static, n_main_out):
    n_in = job_static[0] if job_static else 0
    n_out = 3 if job_static else 0
    job_in, rest = rest[:n_in], rest[n_in:]
    return job_in, rest[:n_main_out], rest[n_main_out:n_main_out + n_out], rest[n_main_out + n_out:]


def _run_job(job_static, in_refs, out_refs, step, shift_now=True):
    _, dil, per_step = job_static
    return _sample_dilated_step(*in_refs[:4], *out_refs, step, dil=dil, bb=per_step, shift_now=shift_now)


def _job_shift_part(job_static, in_refs, out_refs, new_cols, part, nparts):
    for e in range(job_static[2]):
        _shift_rows(in_refs[3], out_refs[0], e, new_cols[e], part, nparts)


def _sample_swa_kernel(q_ref, kn_ref, vn_ref, cache_ref, sink_ref, out_cache_ref, o_ref, qexp, oexp, *, bb):
    step = pl.program_id(0)
    nb, length = q_ref.shape[0], cache_ref.shape[2]
    low = lax.broadcasted_iota(jnp.int32, (1, LANES), 1) < HEAD_DIM

    @pl.when(step == 0)
    def _():
        for hq in range(B_Q_HEADS):
            slab = q_ref[:, (hq // 2) * LANES:(hq // 2 + 1) * LANES]
            src_low, dst_low = hq % 2 == 0, hq // B_GROUP == 0
            x = slab if src_low == dst_low else pltpu.roll(slab, HEAD_DIM, 1)
            qexp[hq * nb:(hq + 1) * nb, :] = jnp.where(low if dst_low else jnp.logical_not(low), x, 0.0)

    own_half = (lax.broadcasted_iota(jnp.int32, (B_Q_HEADS, LANES), 0) // B_GROUP
                == lax.broadcasted_iota(jnp.int32, (B_Q_HEADS, LANES), 1) // HEAD_DIM)
    valid = lax.broadcasted_iota(jnp.int32, (1, length), 1) >= 1
    sink = sink_ref[:, 0:1]
    for e in range(bb):
        b = step * bb + e
        heads = pl.ds(b, B_Q_HEADS, stride=nb)
        kn, vn = kn_ref[pl.ds(b, 1), :], vn_ref[pl.ds(b, 1), :]
        qb = qexp[heads, :]
        kt = cache_ref[e, 0:B_KV_WIDTH, :].astype(BF16)
        vt = cache_ref[e, B_KV_WIDTH:2 * B_KV_WIDTH, :].astype(BF16)
        s = jnp.where(valid, _dot(qb.astype(BF16), kt), NEG)
        s_new = jnp.sum(qb * kn, axis=1, keepdims=True)
        m = jnp.maximum(jnp.maximum(jnp.max(s, axis=1, keepdims=True), s_new), sink)
        p = jnp.exp2(s - m)
        p_new = jnp.exp2(s_new - m)
        l = jnp.sum(p, axis=1, keepdims=True) + p_new + jnp.exp2(sink - m)
        o = lax.dot_general(p.astype(BF16), vt, _NT, preferred_element_type=F32)
        oexp[heads, :] = jnp.where(own_half, o + p_new * vn, 0.0) * (1.0 / l)
        _shift_rows(cache_ref, out_cache_ref, e, jnp.concatenate([_row_to_col(kn), _row_to_col(vn)], axis=0))

    @pl.when(step == pl.num_programs(0) - 1)
    def _():
        for j in range(B_Q_HEADS // 2):
            even, odd = oexp[2 * j * nb:(2 * j + 1) * nb, :], oexp[(2 * j + 1) * nb:(2 * j + 2) * nb, :]
            if 2 * j // B_GROUP == 0:
                odd = pltpu.roll(odd, HEAD_DIM, 1)
            else:
                even = pltpu.roll(even, HEAD_DIM, 1)
            o_ref[:, j * LANES:(j + 1) * LANES] = jnp.where(low, even, odd)


def _sample_attn(q, k_new, v_new, cache_t, *, dil=1, sinks=None):
    nb, chans, length = cache_t.shape
    wq, wkv = q.shape[1], k_new.shape[1]
    bb = max(1, min(8, 2048 // length))
    full = lambda w: pl.BlockSpec((nb, w), lambda s: (0, 0))
    tile = pl.BlockSpec((bb, chans, length), lambda s: (s, 0, 0))
    in_specs = [full(wq), full(wkv), full(wkv), tile]
    args = [q, k_new, v_new, cache_t]
    out_specs = [tile, full(wq)]
    out_shape = [jax.ShapeDtypeStruct(cache_t.shape, F32), jax.ShapeDtypeStruct((nb, wq), F32)]
    if sinks is None:
        kern = functools.partial(_sample_dilated_kernel, dil=dil, bb=bb)
        out_specs.append(full(wq))
        out_shape.append(jax.ShapeDtypeStruct((nb, wq), F32))
        scratch = []
    else:
        kern = functools.partial(_sample_swa_kernel, bb=bb)
        in_specs.append(pl.BlockSpec(sinks.shape, lambda s: (0, 0)))
        args.append(sinks)
        scratch = [pltpu.VMEM((B_Q_HEADS * nb, LANES), F32)] * 2
    return pl.pallas_call(
        kern, grid=(nb // bb,), in_specs=in_specs, out_specs=out_specs, out_shape=out_shape,
        scratch_shapes=scratch, compiler_params=_ARB(1),
        name=f"sample_attn_d{dil}_l{length}" if sinks is None else "sample_swa",
    )(*args)


def _rope_tables(pos):
    half = HEAD_DIM // 2
    inv = ROPE_THETA ** (-jnp.arange(half, dtype=F32) / half)
    ang = pos.astype(F32)[:, None] * jnp.tile(inv, 4)[None, :]
    sign = jnp.where(jnp.arange(LANES) < HEAD_DIM, -1.0, 1.0).astype(F32)
    return jnp.cos(ang), jnp.sin(ang) * sign[None, :]


def _to_tiles(cache):
    nb, length = cache.shape[0], cache.shape[1]
    return jnp.transpose(cache, (0, 2, 3, 4, 1)).reshape(nb, -1, length)


def _from_tiles(tiles, heads):
    nb, _, length = tiles.shape
    return jnp.transpose(tiles.reshape(nb, 2, heads, HEAD_DIM, length), (0, 4, 1, 2, 3))


def kernel(x_prompt, x_sample, c_prompt, c_sample, cache_a_kv_g0, cache_a_kv_g1, cache_a_kv_g2, cache_b_kv, ada_w,
           ada_b, g_pre, g_post, w_in_a, w_o_a, w_in_b, w_o_b, sinks_b, ada_kv_w, ada_kv_b, g_kv, w_kv):
    nbp, t, _ = x_prompt.shape
    nbs = x_sample.shape[0]
    assert x_sample.shape[1] == 1

    c_all = jnp.concatenate([c_prompt, c_sample], axis=0)
    mod = _ada(c_all, ada_w, ada_b)
    mod_kv = _ada(c_all, ada_kv_w[None], ada_kv_b[None])[0]

    def split(a, parts, sample):
        rows = a[nbp:][None] if sample else a[:nbp][:, None]
        return [rows[..., p * D_MODEL:(p + 1) * D_MODEL] for p in range(parts)]

    def paired(w, ncols):
        return jnp.concatenate([_paired_columns(w[:, :ncols]), w[:, ncols:]], axis=1).astype(BF16)

    w_in_a_b, w_o_a_b = paired(w_in_a[0], 2 * A_WIDTH), w_o_a[0].astype(BF16)
    w_in_b_b, w_o_b_b, w_kv_b = paired(w_in_b[0], B_WIDTH), w_o_b[0].astype(BF16), paired(w_kv, B_KV_WIDTH)
    g_pre0, g_pre1 = g_pre[0:1], g_pre[1:2]
    g_post0, g_post1 = g_post[0:1], g_post[1:2]
    g_kv_r = g_kv[None]
    sinks = jnp.broadcast_to(sinks_b[0][:, None] * LOG2E, (B_Q_HEADS, LANES))

    xs = x_sample.reshape(1, nbs, D_MODEL)
    cos_s, sin_s = _rope_tables(jnp.full((nbs,), PAST_LEN, jnp.int32))
    sh0s, sc0s, gt0s = split(mod[0], 3, True)
    sh1s, sc1s, gt1s = split(mod[1], 3, True)
    kshs, kscs = split(mod_kv, 2, True)
    qs, ks, vs, gzs = _inproj_a(xs, sh0s, sc0s, g_pre0, w_in_a_b, cos_s, sin_s, sample=True)
    tiles = [_to_tiles(c[0]) for c in (cache_a_kv_g0, cache_a_kv_g1, cache_a_kv_g2)]

    def job(g, first, count, per_step, prev_out=None):
        rows, cols = slice(first, first + count), slice(g * GROUP_WIDTH, (g + 1) * GROUP_WIDTH)
        return _ShiftJob(qs[0, rows, cols], ks[0, rows, cols], vs[0, rows, cols], tiles[g], first, per_step,
                         A_GROUPS[g][1], prev_out)

    proj_steps = nbp * (t // min(TOKEN_TILE, t))

    cos_p, sin_p = _rope_tables(jnp.arange(t))
    sh0, sc0, gt0 = split(mod[0], 3, False)
    sh1, sc1, gt1 = split(mod[1], 3, False)
    ksh, ksc = split(mod_kv, 2, False)
    *qkv, gz, ca0, ca1, ca2, shifted2, o2a, lse2a = _inproj_a(
        x_prompt, sh0, sc0, g_pre0, w_in_a_b, cos_p, sin_p, sample=False, job=job(2, 0, proj_steps, 1))
    g1_steps = nbp * _dilated_steps(A_GROUPS[1][1])[1]
    attn = [_dilated_attn(*qkv[0:3], 0),
            _dilated_attn(*qkv[3:6], 1, job=job(0, 0, nbs, nbs // g1_steps)),
            _dilated_attn(*qkv[6:9], 2)]
    os, lses = [a[0] for a in attn], [a[1] for a in attn]
    shifted0, o0, lse0 = attn[1][2:]
    x1, qb, gzb, kd, vd, cb, shifted1, o1, lse1 = _mid_prompt(
        os, lses, gz, x_prompt, gt0, g_post0, w_o_a_b, sh1, sc1, ksh, ksc, g_pre1, g_kv_r, w_in_b_b, w_kv_b,
        cos_p, sin_p, job=job(1, 0, nbs, nbs // proj_steps))
    y_prompt, shifted2, o2b, lse2b = _swa_attn(qb, kd, vd, sinks, gzb, x1, gt1, g_post1, w_o_b_b,
                                               job=job(2, proj_steps, nbs - proj_steps, 1, prev_out=shifted2))
    new_a_prompt = [_from_tiles(c, A_HEADS_PER_GROUP)[None] for c in (ca0, ca1, ca2)]
    new_b_prompt = _from_tiles(cb, B_KV_HEADS)

    os = [o0, o1, jnp.concatenate([o2a, o2b], axis=0)]
    lses = [lse0, lse1, jnp.concatenate([lse2a, lse2b], axis=0)]
    os = [o.astype(BF16)[None, None] for o in os]
    lses = [l[None, None] for l in lses]
    new_a_sample = [_from_tiles(s, A_HEADS_PER_GROUP)[None] for s in (shifted0, shifted1, shifted2)]
    sh1, sc1, gt1, ksh, ksc, gz, gt0 = sh1s, sc1s, gt1s, kshs, kscs, gzs, gt0s
    xs1 = _outproj(os, lses, gz, xs, gt0, g_post0, w_o_a_b)
    qb, gzb, kb, vb = _inproj_b_sample(xs1, sh1, sc1, ksh, ksc, g_pre1, g_kv_r, w_in_b_b, w_kv_b, cos_s, sin_s)
    shifted_b, ob = _sample_attn(qb[0], kb[0], vb[0], _to_tiles(cache_b_kv), sinks=sinks)
    y_sample = _outproj([ob.astype(BF16)[None]], None, gzb, xs1, gt1, g_post1, w_o_b_b).reshape(nbs, 1, D_MODEL)
    new_b_sample = _from_tiles(shifted_b, B_KV_HEADS)

    return (y_prompt, y_sample, *new_a_prompt, new_b_prompt, *new_a_sample, new_b_sample)
```

```python
import functools
from typing import NamedTuple, Optional

import jax
import jax.numpy as jnp
from jax import lax
from jax.experimental import pallas as pl
from jax.experimental.pallas import tpu as pltpu

D_MODEL = 1024
HEAD_DIM = 64
A_GROUPS = ((128, 1), (512, 4), (2048, 16))
A_HEADS_PER_GROUP = 4
GROUP_WIDTH = A_HEADS_PER_GROUP * HEAD_DIM
A_WIDTH = GROUP_WIDTH * len(A_GROUPS)
B_Q_HEADS = 16
B_KV_HEADS = 2
B_GROUP = B_Q_HEADS // B_KV_HEADS
B_WIDTH = B_Q_HEADS * HEAD_DIM
B_KV_WIDTH = B_KV_HEADS * HEAD_DIM
B_WINDOW = 128
BLOCK = 128
PAST_LEN = 16384
ROPE_THETA = 10000.0
EPS = 1e-6
NEG = -1e30
LANES = 128
LOG2E = 1.4426950408889634
LN2 = 0.6931471805599453
Q_SCALE = HEAD_DIM ** -0.5 * LOG2E

BF16 = jnp.bfloat16
F32 = jnp.float32
TOKEN_TILE = 512

_ARB = lambda n: pltpu.CompilerParams(dimension_semantics=("arbitrary",) * n)


HALF = HEAD_DIM // 2
PAIRED_QUARTERS = (0, 2, 1, 3)


def _paired_columns(w):
    d, n = w.shape
    return w.reshape(d, n // LANES, 4, HALF)[:, :, PAIRED_QUARTERS, :].reshape(d, n)


def _rope(x, cos, sin_signed):
    outs = []
    for j in range(x.shape[1] // LANES):
        xs = x[:, j * LANES:(j + 1) * LANES]
        outs.append(xs * cos + pltpu.roll(xs, HEAD_DIM, 1) * sin_signed)
    return outs[0] if len(outs) == 1 else jnp.concatenate(outs, axis=1)


def _standard_lanes(x):
    quarter = lax.broadcasted_iota(jnp.int32, (1, LANES), 1) // HALF
    outs = []
    for j in range(x.shape[1] // LANES):
        xs = x[:, j * LANES:(j + 1) * LANES]
        outs.append(jnp.where(quarter == 1, pltpu.roll(xs, LANES - HALF, 1),
                              jnp.where(quarter == 2, pltpu.roll(xs, HALF, 1), xs)))
    return outs[0] if len(outs) == 1 else jnp.concatenate(outs, axis=1)


def _store_standard_rows(ref, xt, dst):
    for s in range(xt.shape[0] // LANES):
        for quarter, src in enumerate(PAIRED_QUARTERS):
            row, src_row = s * LANES + quarter * HALF, s * LANES + src * HALF
            ref[0, row:row + HALF, dst] = xt[src_row:src_row + HALF]


def _normalize(x):
    return x * lax.rsqrt(jnp.mean(x * x, axis=-1, keepdims=True) + EPS)


def _dot(a, b):
    return jnp.dot(a, b, preferred_element_type=F32)


def _ada_kernel(c_ref, w_ref, b_ref, o_ref):
    s = jax.nn.silu(c_ref[...]).astype(BF16)
    o_ref[...] = _dot(s, w_ref[...].astype(BF16)) + b_ref[...]


def _ada(c, w, b):
    nl, _, n = w.shape
    m = c.shape[0]
    tn = 1024
    return pl.pallas_call(
        _ada_kernel,
        grid=(nl, n // tn),
        in_specs=[
            pl.BlockSpec((m, D_MODEL), lambda l, j: (0, 0)),
            pl.BlockSpec((None, D_MODEL, tn), lambda l, j: (l, 0, j)),
            pl.BlockSpec((None, 1, tn), lambda l, j: (l, 0, j)),
        ],
        out_specs=pl.BlockSpec((None, m, tn), lambda l, j: (l, 0, j)),
        out_shape=jax.ShapeDtypeStruct((nl, m, n), F32),
        compiler_params=_ARB(2),
        name="ada",
    )(c, w, b.reshape(nl, 1, n))


ROW_CHUNK = 512


def _row_chunks(tm):
    size = min(ROW_CHUNK, tm)
    return [slice(c * size, (c + 1) * size) for c in range(tm // size)]


def _mod_rows(ref, rows):
    return ref[0] if ref.shape[1] == 1 else ref[0, rows, :]


def _inproj_a_body(x_ref, sh_ref, sc_ref, g_ref, w_ref, cos_ref, sin_ref, rows):
    h = _normalize(x_ref[0, rows, :]) * g_ref[...]
    hb = (h * (1.0 + _mod_rows(sc_ref, rows)) + _mod_rows(sh_ref, rows)).astype(BF16)
    cos, sin = cos_ref[rows, :], sin_ref[rows, :]
    q = _rope(_dot(hb, w_ref[:, 0:A_WIDTH]), cos, sin) * Q_SCALE
    k = _rope(_dot(hb, w_ref[:, A_WIDTH:2 * A_WIDTH]), cos, sin)
    v = _dot(hb, w_ref[:, 2 * A_WIDTH:3 * A_WIDTH])
    gz = jax.nn.silu(_dot(hb, w_ref[:, 3 * A_WIDTH:4 * A_WIDTH]))
    return q, k, v, gz


def _inproj_a_prompt_kernel(x_ref, sh_ref, sc_ref, g_ref, w_ref, cos_ref, sin_ref, *rest, tm, nt, job):
    job_in, main_out, job_out, (scr,) = _split_job_refs(rest, job, 13)
    qkv_refs, (gz_ref, c0_ref, c1_ref, c2_ref) = main_out[:9], main_out[9:]
    i = pl.program_id(1)
    if job:
        _run_job(job, job_in, job_out, pl.program_id(0) * nt + i)
    slabs_per_group = GROUP_WIDTH // LANES
    for c, rows in enumerate(_row_chunks(tm)):
        n = rows.stop - rows.start
        q, k, v, gz = _inproj_a_body(x_ref, sh_ref, sc_ref, g_ref, w_ref, cos_ref, sin_ref, rows)
        gz_ref[0, rows, :] = gz.astype(BF16)
        for which, val in enumerate((q, k, v)):
            for j in range(A_WIDTH // LANES):
                scr[c, which, j] = val[:, j * LANES:(j + 1) * LANES]
            for g, (_, dil) in enumerate(A_GROUPS):
                out_ref = qkv_refs[3 * g + which]
                dst = slice(rows.start // dil, rows.stop // dil)
                for r in range(dil):
                    for h in range(slabs_per_group):
                        part = scr[c, which, g * slabs_per_group + h, pl.ds(r, n // dil, stride=dil), :]
                        out_ref[0, r, dst, h * LANES:(h + 1) * LANES] = part.astype(BF16)
        for g, c_ref in enumerate((c0_ref, c1_ref, c2_ref)):
            win = A_GROUPS[g][0]
            width = min(tm, win)
            first_tile = nt - max(win // tm, 1)
            lo = max(rows.start, tm - width)
            if lo >= rows.stop:
                continue

            @pl.when(i >= first_tile)
            def _(g=g, c_ref=c_ref, lo=lo, k=k, v=v, rows=rows, width=width):
                cols = slice(g * GROUP_WIDTH, (g + 1) * GROUP_WIDTH)
                dst = slice(lo - (tm - width), rows.stop - (tm - width))
                _store_standard_rows(c_ref, k[lo - rows.start:, cols].T, dst)
                c_ref[0, GROUP_WIDTH:2 * GROUP_WIDTH, dst] = v[lo - rows.start:, cols].T


def _inproj_a_sample_kernel(x_ref, sh_ref, sc_ref, g_ref, w_ref, cos_ref, sin_ref,
                            q_ref, k_ref, v_ref, gz_ref):
    q, k, v, gz = _inproj_a_body(x_ref, sh_ref, sc_ref, g_ref, w_ref, cos_ref, sin_ref, slice(0, x_ref.shape[1]))
    q_ref[0] = _standard_lanes(q)
    k_ref[0] = _standard_lanes(k)
    v_ref[0] = v
    gz_ref[0] = gz.astype(BF16)


class _Mod(NamedTuple):
    array: jax.Array
    layer: int
    part: int
    row0: int
    per_row: bool


def _mod_spec(m, tm):
    if m.per_row:
        return pl.BlockSpec((None, 1, tm, D_MODEL), lambda b, i: (m.layer, 0, m.row0 // tm + i, m.part))
    return pl.BlockSpec((None, 1, 1, D_MODEL), lambda b, i: (m.layer, m.row0 + b, 0, m.part))


def _arrays(args):
    return [a.array if isinstance(a, _Mod) else a for a in args]


def _inproj_a(x, shift, scale, g, w, cos, sin, *, sample, job=None):
    nb, t, _ = x.shape
    tm = min(TOKEN_TILE, t)
    nt = t // tm
    row = lambda width: pl.BlockSpec((1, tm, width), lambda b, i: (b, i, 0))
    in_specs = [
        row(D_MODEL), _mod_spec(shift, tm), _mod_spec(scale, tm),
        pl.BlockSpec((1, D_MODEL), lambda b, i: (0, 0)),
        pl.BlockSpec((D_MODEL, 4 * A_WIDTH), lambda b, i: (0, 0)),
        pl.BlockSpec((tm, LANES), lambda b, i: (i, 0)),
        pl.BlockSpec((tm, LANES), lambda b, i: (i, 0)),
    ]
    if sample:
        out_shape = [jax.ShapeDtypeStruct((nb, t, A_WIDTH), dt) for dt in (F32, F32, F32, BF16)]
        return pl.pallas_call(
            _inproj_a_sample_kernel, grid=(nb, nt), in_specs=in_specs,
            out_specs=[row(A_WIDTH)] * 4, out_shape=out_shape,
            compiler_params=_ARB(2), name="inproj_a_sample",
        )(*_arrays([x, shift, scale, g, w, cos, sin]))
    out_shape, out_specs = [], []
    for _, dil in A_GROUPS:
        out_shape += [jax.ShapeDtypeStruct((nb, dil, t // dil, GROUP_WIDTH), BF16)] * 3
        out_specs += [pl.BlockSpec((1, dil, tm // dil, GROUP_WIDTH), lambda b, i: (b, 0, i, 0))] * 3
    out_shape.append(jax.ShapeDtypeStruct((nb, t, A_WIDTH), BF16))
    out_specs.append(row(A_WIDTH))
    for win, _ in A_GROUPS:
        rows = min(tm, win)
        first_tile = nt - max(win // tm, 1)
        out_shape.append(jax.ShapeDtypeStruct((nb, 2 * GROUP_WIDTH, min(win, t)), F32))
        out_specs.append(pl.BlockSpec(
            (1, 2 * GROUP_WIDTH, rows),
            lambda b, i, first_tile=first_tile: (b, 0, jnp.maximum(i - first_tile, 0))))
    args, aliases = [x, shift, scale, g, w, cos, sin], {}
    if job is not None:
        assert job.count == nb * nt * job.per_step
        j_in, j_args, j_out, j_shape, aliases = _job_operands(job, lambda b, i: b * nt + i, len(args),
                                                              len(out_specs))
        in_specs, args = in_specs + j_in, args + j_args
        out_specs, out_shape = out_specs + j_out, out_shape + j_shape
    return pl.pallas_call(
        functools.partial(_inproj_a_prompt_kernel, tm=tm, nt=nt, job=job.static if job else None),
        grid=(nb, nt), in_specs=in_specs, out_specs=out_specs, out_shape=out_shape,
        scratch_shapes=[pltpu.VMEM((tm // min(ROW_CHUNK, tm), 3, A_WIDTH // LANES, min(ROW_CHUNK, tm), LANES), F32)],
        input_output_aliases=aliases, compiler_params=_ARB(2), name="inproj_a_prompt",
    )(*_arrays(args))


def _query_minus_key():
    kj = lax.broadcasted_iota(jnp.int32, (2 * BLOCK, BLOCK), 0)
    qi = lax.broadcasted_iota(jnp.int32, (2 * BLOCK, BLOCK), 1)
    return qi - kj


def _head_attend(st, mask, vt_ext, sink):
    st = jnp.where(mask, st, NEG)
    m = jnp.max(st, axis=0, keepdims=True)
    if sink is not None:
        m = jnp.maximum(m, sink)
    ext = _dot(vt_ext, jnp.exp2(st - m).astype(BF16))
    l = ext[HEAD_DIM:HEAD_DIM + 1, :]
    if sink is not None:
        l = l + jnp.exp2(sink - m)
    return ext[0:HEAD_DIM, :] * (1.0 / l), m, l


def _values_ext(vt_prev, vt_cur, head):
    rows = slice(head * HEAD_DIM, (head + 1) * HEAD_DIM)
    ones = jnp.ones((16, 2 * BLOCK), BF16)
    return jnp.concatenate([jnp.concatenate([vt_prev[rows], vt_cur[rows]], axis=1), ones], axis=0)


def _block_diag_queries(qt, heads, kv_of_head, kv_heads):
    zeros = jnp.zeros((HALF, BLOCK), BF16)
    cols = []
    for h in range(heads):
        src = (h // 2) * LANES + (h % 2) * HALF
        lo, hi = qt[src:src + HALF], qt[src + HEAD_DIM:src + HEAD_DIM + HALF]
        kv = kv_of_head(h)
        pieces = []
        for slab in range(kv_heads // 2):
            for quarter in range(4):
                mine = slab == kv // 2 and quarter % 2 == kv % 2
                pieces.append((lo if quarter < 2 else hi) if mine else zeros)
        cols.append(jnp.concatenate(pieces, axis=0))
    return jnp.concatenate(cols, axis=1)


ATTN_BLOCKS_IN_FLIGHT = 16


def _dilated_attn_kernel(q_ref, k_ref, v_ref, *rest, job):
    job_in, (o_ref, lse_ref), job_out, (vt_ref,) = _split_job_refs(rest, job, 2)
    if job:
        _run_job(job, job_in, job_out, pl.program_id(0) * pl.num_programs(1) + pl.program_id(1))
    nres, m_rows = q_ref.shape[1], q_ref.shape[2]
    nblk = m_rows // BLOCK
    base = _query_minus_key()
    heads = A_HEADS_PER_GROUP

    def block(r, i):
        prev = jnp.maximum(i - 1, 0)
        dist = base + (i - prev) * BLOCK
        mask = (dist >= 0) & (dist <= BLOCK)
        qrows = pl.ds(pl.multiple_of(i * BLOCK, BLOCK), BLOCK)
        k = k_ref[0, r, pl.ds(pl.multiple_of(prev * BLOCK, BLOCK), 2 * BLOCK), :]
        qd = _block_diag_queries(q_ref[0, r, qrows, :].T, heads, lambda h: h, heads)
        st_all = _dot(k, qd)
        vt_cur = v_ref[0, r, qrows, :].T
        vt_ref[r, i] = vt_cur
        vt_prev = vt_ref[r, prev]
        outs, lses = [], []
        for h in range(heads):
            o_t, m, l = _head_attend(st_all[:, h * BLOCK:(h + 1) * BLOCK], mask,
                                     _values_ext(vt_prev, vt_cur, h), None)
            outs.append(o_t)
            lses.append(jnp.broadcast_to(m * LN2 + jnp.log(l), (HEAD_DIM, BLOCK)))
        o_ref[0, r, qrows, :] = jnp.concatenate(outs, axis=0).T.astype(BF16)
        lse_ref[0, r, qrows, :] = jnp.concatenate(lses, axis=0).T

    if nres * nblk <= ATTN_BLOCKS_IN_FLIGHT:
        for r in range(nres):
            for i in range(nblk):
                block(r, i)
    else:
        assert nres == 1 and nblk % ATTN_BLOCKS_IN_FLIGHT == 0

        def chunk(c, carry):
            for u in range(ATTN_BLOCKS_IN_FLIGHT):
                block(0, c * ATTN_BLOCKS_IN_FLIGHT + u)
            return carry

        lax.fori_loop(0, nblk // ATTN_BLOCKS_IN_FLIGHT, chunk, 0)


def _dilated_steps(dil):
    nres = max(1, dil // 2)
    return nres, dil // nres


def _dilated_attn(q, k, v, g, job=None):
    nb, dil, m_rows, _ = q.shape
    nres, nsteps = _dilated_steps(dil)
    blk = pl.BlockSpec((1, nres, m_rows, GROUP_WIDTH), lambda b, r: (b, r, 0, 0))
    in_specs, args, aliases = [blk, blk, blk], [q, k, v], {}
    out_specs = [blk, blk]
    out_shape = [jax.ShapeDtypeStruct(q.shape, BF16), jax.ShapeDtypeStruct(q.shape, F32)]
    if job is not None:
        assert job.count == nb * nsteps * job.per_step
        j_in, j_args, j_out, j_shape, aliases = _job_operands(job, lambda b, r: b * nsteps + r, len(args), 2)
        in_specs, args = in_specs + j_in, args + j_args
        out_specs, out_shape = out_specs + j_out, out_shape + j_shape
    return pl.pallas_call(
        functools.partial(_dilated_attn_kernel, job=job.static if job else None),
        grid=(nb, nsteps), in_specs=in_specs, out_specs=out_specs, out_shape=out_shape,
        scratch_shapes=[pltpu.VMEM((nres, m_rows // BLOCK, GROUP_WIDTH, BLOCK), BF16)],
        input_output_aliases=aliases, compiler_params=_ARB(2), name=f"dilated_attn_g{g}",
    )(*_arrays(args))


SWA_QUERY_TILE = 512


def _swa_attn_kernel(q_ref, k_ref, v_ref, sink_ref, gz_ref, x_ref, gate_ref, g_ref, w_ref, *rest, job):
    job_in, (y_ref,), job_out, (vt_ref, o_scr) = _split_job_refs(rest, job, 1)
    j = pl.program_id(1)
    nq = q_ref.shape[1] // BLOCK
    base = _query_minus_key()
    if job:
        new_cols = _run_job(job, job_in, job_out, pl.program_id(0) * pl.num_programs(1) + j, shift_now=False)

    for i in range(nq):
        cur = j * nq + i
        prev = jnp.maximum(cur - 1, 0)
        dist = base + (cur - prev) * BLOCK
        mask = (dist >= 0) & (dist < B_WINDOW)
        qrows = slice(i * BLOCK, (i + 1) * BLOCK)
        k = k_ref[0, pl.ds(pl.multiple_of(prev * BLOCK, BLOCK), 2 * BLOCK), :]
        qt = q_ref[0, qrows, :].T
        vt_cur = v_ref[0, pl.ds(pl.multiple_of(cur * BLOCK, BLOCK), BLOCK), :].T
        vt_ref[cur] = vt_cur
        vt_prev = vt_ref[prev]
        outs = []
        for kvh in range(B_KV_HEADS):
            qd = _block_diag_queries(qt[kvh * B_GROUP * HEAD_DIM:(kvh + 1) * B_GROUP * HEAD_DIM], B_GROUP,
                                     lambda h, kvh=kvh: kvh, B_KV_HEADS)
            st_all = _dot(k, qd)
            vt_ext = _values_ext(vt_prev, vt_cur, kvh)
            for gq in range(B_GROUP):
                hq = kvh * B_GROUP + gq
                o_t, _, _ = _head_attend(st_all[:, gq * BLOCK:(gq + 1) * BLOCK], mask, vt_ext,
                                         sink_ref[hq:hq + 1, :])
                outs.append(o_t)
        o_scr[qrows, :] = jnp.concatenate(outs, axis=0).T.astype(BF16)
        if job:
            _job_shift_part(job, job_in, job_out, new_cols, i, nq)
    a = (o_scr[...].astype(F32) * gz_ref[0].astype(F32)).astype(BF16)
    _finish([a], w_ref, x_ref, gate_ref, g_ref, y_ref)


def _swa_attn(q, k, v, sinks, gz, x, gate, g, w, job=None):
    nb, t, _ = q.shape
    tq = min(SWA_QUERY_TILE, t)
    nj = t // tq
    qblk = pl.BlockSpec((1, tq, B_WIDTH), lambda b, j: (b, j, 0))
    xblk = pl.BlockSpec((1, tq, D_MODEL), lambda b, j: (b, j, 0))
    kvblk = pl.BlockSpec((1, t, B_KV_WIDTH), lambda b, j: (b, 0, 0))
    in_specs = [qblk, kvblk, kvblk, pl.BlockSpec((B_Q_HEADS, LANES), lambda b, j: (0, 0)),
                qblk, xblk, _mod_spec(gate, tq), pl.BlockSpec((1, D_MODEL), lambda b, j: (0, 0)),
                pl.BlockSpec((B_WIDTH, D_MODEL), lambda b, j: (0, 0))]
    args, aliases = [q, k, v, sinks, gz, x, gate, g, w], {}
    out_specs, out_shape = [xblk], [jax.ShapeDtypeStruct((nb, t, D_MODEL), F32)]
    if job is not None:
        assert job.count == nb * nj * job.per_step
        j_in, j_args, j_out, j_shape, aliases = _job_operands(job, lambda b, j: b * nj + j, len(args), 1)
        in_specs, args = in_specs + j_in, args + j_args
        out_specs, out_shape = out_specs + j_out, out_shape + j_shape
    return pl.pallas_call(
        functools.partial(_swa_attn_kernel, job=job.static if job else None),
        grid=(nb, nj), in_specs=in_specs, out_specs=out_specs, out_shape=out_shape,
        scratch_shapes=[pltpu.VMEM((t // BLOCK, B_KV_WIDTH, BLOCK), BF16), pltpu.VMEM((tq, B_WIDTH), BF16)],
        input_output_aliases=aliases, compiler_params=_ARB(2), name="swa_attn",
    )(*_arrays(args))


def _finish_value(a_parts, w_ref, x, gate, g):
    acc = None
    row = 0
    for a in a_parts:
        part = _dot(a, w_ref[row:row + a.shape[1], :])
        acc = part if acc is None else acc + part
        row += a.shape[1]
    return x + gate * (_normalize(acc) * g)


def _finish(a_parts, w_ref, x_ref, gate_ref, g_ref, y_ref):
    y_ref[0] = _finish_value(a_parts, w_ref, x_ref[0], gate_ref[0], g_ref[...])


def _in_row_order(ref, scr):
    dil = ref.shape[1]
    if dil == 1:
        return ref[0, 0].astype(F32)
    nslab = ref.shape[3] // LANES
    for r in range(dil):
        for h in range(nslab):
            scr[h, pl.ds(r, ref.shape[2], stride=dil), :] = ref[0, r, :, h * LANES:(h + 1) * LANES].astype(F32)
    return jnp.concatenate([scr[h] for h in range(nslab)], axis=1)


def _mixture_parts(o_refs, l_refs, gz_ref, scratch):
    os = [_in_row_order(r, s) for r, s in zip(o_refs, scratch[0:3])]
    lses = [_in_row_order(r, s) for r, s in zip(l_refs, scratch[3:6])]
    top = jnp.maximum(jnp.maximum(lses[0], lses[1]), lses[2])
    es = [jnp.exp(l - top) for l in lses]
    inv = 1.0 / (es[0] + es[1] + es[2])
    parts = []
    for g in range(len(A_GROUPS)):
        gz = gz_ref[0, :, g * GROUP_WIDTH:(g + 1) * GROUP_WIDTH].astype(F32)
        parts.append((os[g] * (es[g] * inv) * gz).astype(BF16))
    return parts


def _outproj_mix_kernel(o0_ref, o1_ref, o2_ref, l0_ref, l1_ref, l2_ref, gz_ref, x_ref, gate_ref, g_ref, w_ref,
                        y_ref, *scratch):
    parts = _mixture_parts((o0_ref, o1_ref, o2_ref), (l0_ref, l1_ref, l2_ref), gz_ref, scratch)
    _finish(parts, w_ref, x_ref, gate_ref, g_ref, y_ref)


def _outproj_kernel(o_ref, gz_ref, x_ref, gate_ref, g_ref, w_ref, y_ref):
    a = (o_ref[0].astype(F32) * gz_ref[0].astype(F32)).astype(BF16)
    _finish([a], w_ref, x_ref, gate_ref, g_ref, y_ref)


def _outproj(os, lses, gz, x, gate, g, w):
    nb, t, _ = x.shape
    tm = min(TOKEN_TILE, t)
    row = lambda width: pl.BlockSpec((1, tm, width), lambda b, i: (b, i, 0))
    width = w.shape[0]
    tail_specs = [row(width), row(D_MODEL), _mod_spec(gate, tm),
                  pl.BlockSpec((1, D_MODEL), lambda b, i: (0, 0)),
                  pl.BlockSpec((width, D_MODEL), lambda b, i: (0, 0))]
    scratch = []
    if lses is None:
        kern, name = _outproj_kernel, "outproj_b"
        in_specs = [row(width)] + tail_specs
        args = (os[0], gz, x, gate, g, w)
    else:
        kern, name = _outproj_mix_kernel, "outproj_a"
        split = lambda a: pl.BlockSpec((1, a.shape[1], tm // a.shape[1], GROUP_WIDTH), lambda b, i: (b, 0, i, 0))
        in_specs = [split(a) for a in (*os, *lses)] + tail_specs
        args = (*os, *lses, gz, x, gate, g, w)
        scratch = [pltpu.VMEM((GROUP_WIDTH // LANES, tm, LANES), F32)] * 6
    return pl.pallas_call(
        kern, grid=(nb, t // tm), in_specs=in_specs, out_specs=row(D_MODEL),
        out_shape=jax.ShapeDtypeStruct((nb, t, D_MODEL), F32), scratch_shapes=scratch,
        compiler_params=_ARB(2), name=name,
    )(*_arrays(args))


def _inproj_b_body(x, sh_ref, sc_ref, ksh_ref, ksc_ref, g_ref, gkv_ref, w_ref, wkv_ref, cos_ref, sin_ref, rows):
    xn = _normalize(x)
    hb = ((xn * g_ref[...]) * (1.0 + _mod_rows(sc_ref, rows)) + _mod_rows(sh_ref, rows)).astype(BF16)
    hk = ((xn * gkv_ref[...]) * (1.0 + _mod_rows(ksc_ref, rows)) + _mod_rows(ksh_ref, rows)).astype(BF16)
    cos, sin = cos_ref[rows, :], sin_ref[rows, :]
    q = _rope(_dot(hb, w_ref[:, 0:B_WIDTH]), cos, sin) * Q_SCALE
    gz = jax.nn.silu(_dot(hb, w_ref[:, B_WIDTH:2 * B_WIDTH]))
    kv = _dot(hk, wkv_ref[...])
    k = _rope(kv[:, 0:B_KV_WIDTH], cos, sin)
    v = kv[:, B_KV_WIDTH:2 * B_KV_WIDTH]
    return q, gz, k, v


def _mid_prompt_kernel(o0_ref, o1_ref, o2_ref, l0_ref, l1_ref, l2_ref, gza_ref, x_ref, gate_ref, gpost_ref, wo_ref,
                       sh_ref, sc_ref, ksh_ref, ksc_ref, g_ref, gkv_ref, w_ref, wkv_ref, cos_ref, sin_ref,
                       *rest, tm, nt, job):
    job_in, (x1_ref, q_ref, gz_ref, kd_ref, vd_ref, c_ref), job_out, scratch = _split_job_refs(rest, job, 6)
    if job:
        _run_job(job, job_in, job_out, pl.program_id(0) * nt + pl.program_id(1))
    parts = _mixture_parts((o0_ref, o1_ref, o2_ref), (l0_ref, l1_ref, l2_ref), gza_ref, scratch)
    x1 = _finish_value(parts, wo_ref, x_ref[0], gate_ref[0], gpost_ref[...])
    x1_ref[0] = x1
    q, gz, k, v = _inproj_b_body(x1, sh_ref, sc_ref, ksh_ref, ksc_ref, g_ref, gkv_ref, w_ref, wkv_ref,
                                 cos_ref, sin_ref, slice(0, tm))
    q_ref[0] = q.astype(BF16)
    gz_ref[0] = gz.astype(BF16)
    kd_ref[0] = k.astype(BF16)
    vd_ref[0] = v.astype(BF16)

    @pl.when(pl.program_id(1) == nt - 1)
    def _():
        _store_standard_rows(c_ref, k[tm - B_WINDOW:, :].T, slice(None))
        c_ref[0, B_KV_WIDTH:2 * B_KV_WIDTH, :] = v[tm - B_WINDOW:, :].T


def _inproj_b_sample_kernel(x_ref, sh_ref, sc_ref, ksh_ref, ksc_ref, g_ref, gkv_ref, w_ref, wkv_ref, cos_ref,
                            sin_ref, q_ref, gz_ref, k_ref, v_ref):
    q, gz, k, v = _inproj_b_body(x_ref[0], sh_ref, sc_ref, ksh_ref, ksc_ref, g_ref, gkv_ref, w_ref, wkv_ref,
                                 cos_ref, sin_ref, slice(0, x_ref.shape[1]))
    q_ref[0] = _standard_lanes(q)
    gz_ref[0] = gz.astype(BF16)
    k_ref[0] = _standard_lanes(k)
    v_ref[0] = v


def _inproj_b_sample(x, shift, scale, kshift, kscale, g, gkv, w, wkv, cos, sin):
    nb, t, _ = x.shape
    tm = min(TOKEN_TILE, t)
    row = lambda width: pl.BlockSpec((1, tm, width), lambda b, i: (b, i, 0))
    vec = pl.BlockSpec((1, D_MODEL), lambda b, i: (0, 0))
    tab = pl.BlockSpec((tm, LANES), lambda b, i: (i, 0))
    in_specs = [row(D_MODEL), _mod_spec(shift, tm), _mod_spec(scale, tm), _mod_spec(kshift, tm),
                _mod_spec(kscale, tm), vec, vec,
                pl.BlockSpec((D_MODEL, 2 * B_WIDTH), lambda b, i: (0, 0)),
                pl.BlockSpec((D_MODEL, 2 * B_KV_WIDTH), lambda b, i: (0, 0)), tab, tab]
    return pl.pallas_call(
        _inproj_b_sample_kernel, grid=(nb, t // tm), in_specs=in_specs,
        out_specs=[row(B_WIDTH), row(B_WIDTH), row(B_KV_WIDTH), row(B_KV_WIDTH)],
        out_shape=[jax.ShapeDtypeStruct((nb, t, B_WIDTH), F32), jax.ShapeDtypeStruct((nb, t, B_WIDTH), BF16),
                   jax.ShapeDtypeStruct((nb, t, B_KV_WIDTH), F32), jax.ShapeDtypeStruct((nb, t, B_KV_WIDTH), F32)],
        compiler_params=_ARB(2), name="inproj_b_sample",
    )(*_arrays([x, shift, scale, kshift, kscale, g, gkv, w, wkv, cos, sin]))


def _mid_prompt(os, lses, gza, x, gate, gpost, wo, shift, scale, kshift, kscale, g, gkv, w, wkv, cos, sin, job=None):
    nb, t, _ = x.shape
    tm = min(TOKEN_TILE, t)
    nt = t // tm
    row = lambda width: pl.BlockSpec((1, tm, width), lambda b, i: (b, i, 0))
    vec = pl.BlockSpec((1, D_MODEL), lambda b, i: (0, 0))
    tab = pl.BlockSpec((tm, LANES), lambda b, i: (i, 0))
    full = lambda a: pl.BlockSpec(a.shape, lambda b, i: (0, 0))
    split = lambda a: pl.BlockSpec((1, a.shape[1], tm // a.shape[1], GROUP_WIDTH), lambda b, i: (b, 0, i, 0))
    in_specs = ([split(a) for a in (*os, *lses)]
                + [row(A_WIDTH), row(D_MODEL), _mod_spec(gate, tm), vec, full(wo)]
                + [_mod_spec(m, tm) for m in (shift, scale, kshift, kscale)] + [vec, vec, full(w), full(wkv), tab, tab])
    args = [*os, *lses, gza, x, gate, gpost, wo, shift, scale, kshift, kscale, g, gkv, w, wkv, cos, sin]
    out_specs = [row(D_MODEL), row(B_WIDTH), row(B_WIDTH), row(B_KV_WIDTH), row(B_KV_WIDTH),
                 pl.BlockSpec((1, 2 * B_KV_WIDTH, B_WINDOW), lambda b, i: (b, 0, 0))]
    out_shape = [jax.ShapeDtypeStruct((nb, t, D_MODEL), F32),
                 jax.ShapeDtypeStruct((nb, t, B_WIDTH), BF16), jax.ShapeDtypeStruct((nb, t, B_WIDTH), BF16),
                 jax.ShapeDtypeStruct((nb, t, B_KV_WIDTH), BF16), jax.ShapeDtypeStruct((nb, t, B_KV_WIDTH), BF16),
                 jax.ShapeDtypeStruct((nb, 2 * B_KV_WIDTH, B_WINDOW), F32)]
    aliases = {}
    if job is not None:
        assert job.count == nb * nt * job.per_step
        j_in, j_args, j_out, j_shape, aliases = _job_operands(job, lambda b, i: b * nt + i, len(args),
                                                              len(out_specs))
        in_specs, args = in_specs + j_in, args + j_args
        out_specs, out_shape = out_specs + j_out, out_shape + j_shape
    return pl.pallas_call(
        functools.partial(_mid_prompt_kernel, tm=tm, nt=nt, job=job.static if job else None),
        grid=(nb, nt), in_specs=in_specs, out_specs=out_specs, out_shape=out_shape,
        scratch_shapes=[pltpu.VMEM((GROUP_WIDTH // LANES, tm, LANES), F32)] * 6,
        input_output_aliases=aliases, compiler_params=_ARB(2), name="mid_prompt",
    )(*_arrays(args))


def _eye():
    r = lax.broadcasted_iota(jnp.int32, (LANES, LANES), 0)
    c = lax.broadcasted_iota(jnp.int32, (LANES, LANES), 1)
    return r == c


def _row_to_col(row):
    eye = _eye()
    chunks = [jnp.sum(jnp.where(eye, row[:, j * LANES:(j + 1) * LANES], 0.0), axis=1, keepdims=True)
              for j in range(row.shape[1] // LANES)]
    return chunks[0] if len(chunks) == 1 else jnp.concatenate(chunks, axis=0)


def _shift_rows(cache_ref, out_ref, e, new_col, part=0, nparts=1):
    ntile = cache_ref.shape[2] // LANES
    first, last = part * ntile // nparts, (part + 1) * ntile // nparts
    if first == last:
        return
    lane = lax.broadcasted_iota(jnp.int32, (1, LANES), 1)
    nxt = pltpu.roll(cache_ref[e, :, first * LANES:(first + 1) * LANES], LANES - 1, 1)
    for j in range(first, last):
        cur = nxt
        if j + 1 < ntile:
            nxt = pltpu.roll(cache_ref[e, :, (j + 1) * LANES:(j + 2) * LANES], LANES - 1, 1)
            fill = nxt
        else:
            fill = new_col
        out_ref[e, :, j * LANES:(j + 1) * LANES] = jnp.where(lane < LANES - 1, cur, fill)


_NT = (((1,), (1,)), ((), ()))


def _sample_dilated_step(q_ref, kn_ref, vn_ref, cache_ref, out_cache_ref, o_ref, lse_ref, step, *, dil, bb,
                         shift_now=True):
    width, length = q_ref.shape[1], cache_ref.shape[2]
    new_cols = []
    sel = (lax.broadcasted_iota(jnp.int32, (8, width), 0)
           == lax.broadcasted_iota(jnp.int32, (8, width), 1) // HEAD_DIM)
    valid = lax.broadcasted_iota(jnp.int32, (1, length), 1) % dil == 0
    per_head = lambda x: jnp.sum(jnp.where(sel, x, 0.0), axis=0, keepdims=True)
    for e in range(bb):
        row = pl.ds(step * bb + e, 1)
        q, kn, vn = q_ref[row, :], kn_ref[row, :], vn_ref[row, :]
        qb = jnp.where(sel, q, 0.0)
        kt = cache_ref[e, 0:width, :].astype(BF16)
        vt = cache_ref[e, width:2 * width, :].astype(BF16)
        s = jnp.where(valid, _dot(qb.astype(BF16), kt), NEG)
        s_new = jnp.sum(qb * kn, axis=1, keepdims=True)
        m = jnp.maximum(jnp.max(s, axis=1, keepdims=True), s_new)
        p = jnp.exp2(s - m)
        p_new = jnp.exp2(s_new - m)
        l = per_head(jnp.sum(p, axis=1, keepdims=True) + p_new)
        o = lax.dot_general(p.astype(BF16), vt, _NT, preferred_element_type=F32)
        o_ref[row, :] = (per_head(o) + per_head(p_new) * vn) * (1.0 / l)
        lse_ref[row, :] = per_head(m) * LN2 + jnp.log(l)
        new_cols.append(jnp.concatenate([_row_to_col(kn), _row_to_col(vn)], axis=0))
        if shift_now:
            _shift_rows(cache_ref, out_cache_ref, e, new_cols[-1])
    return new_cols


def _sample_dilated_kernel(q_ref, kn_ref, vn_ref, cache_ref, out_cache_ref, o_ref, lse_ref, *, dil, bb):
    _sample_dilated_step(q_ref, kn_ref, vn_ref, cache_ref, out_cache_ref, o_ref, lse_ref, pl.program_id(0),
                         dil=dil, bb=bb)


class _ShiftJob(NamedTuple):
    q: jax.Array
    k_new: jax.Array
    v_new: jax.Array
    cache: jax.Array
    group: int
    first: int
    count: int
    per_step: int
    dil: int
    prev_out: Optional[jax.Array]

    @property
    def static(self):
        return (4 if self.prev_out is None else 5, self.dil, self.per_step)


def _job_operands(job, step_of, inputs_before, outputs_before):
    count, wq = job.count, GROUP_WIDTH
    _, chans, length = job.cache.shape
    first_blk = job.first // job.per_step
    rows = pl.BlockSpec((None, count, wq), lambda *ids: (0, job.first // count, job.group))
    full = pl.BlockSpec((count, wq), lambda *ids: (0, 0))
    tile = pl.BlockSpec((job.per_step, chans, length), lambda *ids: (first_blk + step_of(*ids), 0, 0))
    in_specs, args, aliases = [rows, rows, rows, tile], [job.q, job.k_new, job.v_new, job.cache], {}
    if job.prev_out is not None:
        in_specs.append(pl.BlockSpec(memory_space=pl.ANY))
        args.append(job.prev_out)
        aliases[inputs_before + 4] = outputs_before
    out_shape = [jax.ShapeDtypeStruct(job.cache.shape, F32)] + [jax.ShapeDtypeStruct((count, wq), F32)] * 2
    return in_specs, args, [tile, full, full], out_shape, aliases


def _split_job_refs(rest, job_static, n_main_out):
    n_in = job_static[0] if job_static else 0
    n_out = 3 if job_static else 0
    job_in, rest = rest[:n_in], rest[n_in:]
    return job_in, rest[:n_main_out], rest[n_main_out:n_main_out + n_out], rest[n_main_out + n_out:]


def _run_job(job_static, in_refs, out_refs, step, shift_now=True):
    _, dil, per_step = job_static
    return _sample_dilated_step(*in_refs[:4], *out_refs, step, dil=dil, bb=per_step, shift_now=shift_now)


def _job_shift_part(job_static, in_refs, out_refs, new_cols, part, nparts):
    for e in range(job_static[2]):
        _shift_rows(in_refs[3], out_refs[0], e, new_cols[e], part, nparts)


def _sample_swa_kernel(q_ref, kn_ref, vn_ref, cache_ref, sink_ref, out_cache_ref, o_ref, qexp, oexp, *, bb):
    step = pl.program_id(0)
    nb, length = q_ref.shape[0], cache_ref.shape[2]
    low = lax.broadcasted_iota(jnp.int32, (1, LANES), 1) < HEAD_DIM

    @pl.when(step == 0)
    def _():
        for hq in range(B_Q_HEADS):
            slab = q_ref[:, (hq // 2) * LANES:(hq // 2 + 1) * LANES]
            src_low, dst_low = hq % 2 == 0, hq // B_GROUP == 0
            x = slab if src_low == dst_low else pltpu.roll(slab, HEAD_DIM, 1)
            qexp[hq * nb:(hq + 1) * nb, :] = jnp.where(low if dst_low else jnp.logical_not(low), x, 0.0)

    own_half = (lax.broadcasted_iota(jnp.int32, (B_Q_HEADS, LANES), 0) // B_GROUP
                == lax.broadcasted_iota(jnp.int32, (B_Q_HEADS, LANES), 1) // HEAD_DIM)
    valid = lax.broadcasted_iota(jnp.int32, (1, length), 1) >= 1
    sink = sink_ref[:, 0:1]
    for e in range(bb):
        b = step * bb + e
        heads = pl.ds(b, B_Q_HEADS, stride=nb)
        kn, vn = kn_ref[pl.ds(b, 1), :], vn_ref[pl.ds(b, 1), :]
        qb = qexp[heads, :]
        kt = cache_ref[e, 0:B_KV_WIDTH, :].astype(BF16)
        vt = cache_ref[e, B_KV_WIDTH:2 * B_KV_WIDTH, :].astype(BF16)
        s = jnp.where(valid, _dot(qb.astype(BF16), kt), NEG)
        s_new = jnp.sum(qb * kn, axis=1, keepdims=True)
        m = jnp.maximum(jnp.maximum(jnp.max(s, axis=1, keepdims=True), s_new), sink)
        p = jnp.exp2(s - m)
        p_new = jnp.exp2(s_new - m)
        l = jnp.sum(p, axis=1, keepdims=True) + p_new + jnp.exp2(sink - m)
        o = lax.dot_general(p.astype(BF16), vt, _NT, preferred_element_type=F32)
        oexp[heads, :] = jnp.where(own_half, o + p_new * vn, 0.0) * (1.0 / l)
        _shift_rows(cache_ref, out_cache_ref, e, jnp.concatenate([_row_to_col(kn), _row_to_col(vn)], axis=0))

    @pl.when(step == pl.num_programs(0) - 1)
    def _():
        for j in range(B_Q_HEADS // 2):
            even, odd = oexp[2 * j * nb:(2 * j + 1) * nb, :], oexp[(2 * j + 1) * nb:(2 * j + 2) * nb, :]
            if 2 * j // B_GROUP == 0:
                odd = pltpu.roll(odd, HEAD_DIM, 1)
            else:
                even = pltpu.roll(even, HEAD_DIM, 1)
            o_ref[:, j * LANES:(j + 1) * LANES] = jnp.where(low, even, odd)


def _sample_attn(q, k_new, v_new, cache_t, *, dil=1, sinks=None):
    nb, chans, length = cache_t.shape
    wq, wkv = q.shape[1], k_new.shape[1]
    bb = max(1, min(8, 2048 // length))
    full = lambda w: pl.BlockSpec((nb, w), lambda s: (0, 0))
    tile = pl.BlockSpec((bb, chans, length), lambda s: (s, 0, 0))
    in_specs = [full(wq), full(wkv), full(wkv), tile]
    args = [q, k_new, v_new, cache_t]
    out_specs = [tile, full(wq)]
    out_shape = [jax.ShapeDtypeStruct(cache_t.shape, F32), jax.ShapeDtypeStruct((nb, wq), F32)]
    if sinks is None:
        kern = functools.partial(_sample_dilated_kernel, dil=dil, bb=bb)
        out_specs.append(full(wq))
        out_shape.append(jax.ShapeDtypeStruct((nb, wq), F32))
        scratch = []
    else:
        kern = functools.partial(_sample_swa_kernel, bb=bb)
        in_specs.append(pl.BlockSpec(sinks.shape, lambda s: (0, 0)))
        args.append(sinks)
        scratch = [pltpu.VMEM((B_Q_HEADS * nb, LANES), F32)] * 2
    return pl.pallas_call(
        kern, grid=(nb // bb,), in_specs=in_specs, out_specs=out_specs, out_shape=out_shape,
        scratch_shapes=scratch, compiler_params=_ARB(1),
        name=f"sample_attn_d{dil}_l{length}" if sinks is None else "sample_swa",
    )(*_arrays(args))


def _rope_tables(pos):
    half = HEAD_DIM // 2
    inv = ROPE_THETA ** (-jnp.arange(half, dtype=F32) / half)
    ang = pos.astype(F32)[:, None] * jnp.tile(inv, 4)[None, :]
    sign = jnp.where(jnp.arange(LANES) < HEAD_DIM, -1.0, 1.0).astype(F32)
    return jnp.cos(ang), jnp.sin(ang) * sign[None, :]


def _to_tiles(cache):
    nb, length = cache.shape[0], cache.shape[1]
    return jnp.transpose(cache, (0, 2, 3, 4, 1)).reshape(nb, -1, length)


def _from_tiles(tiles, heads):
    nb, _, length = tiles.shape
    return jnp.transpose(tiles.reshape(nb, 2, heads, HEAD_DIM, length), (0, 4, 1, 2, 3))


def kernel(x_prompt, x_sample, c_prompt, c_sample, cache_a_kv_g0, cache_a_kv_g1, cache_a_kv_g2, cache_b_kv, ada_w,
           ada_b, g_pre, g_post, w_in_a, w_o_a, w_in_b, w_o_b, sinks_b, ada_kv_w, ada_kv_b, g_kv, w_kv):
    nbp, t, _ = x_prompt.shape
    nbs = x_sample.shape[0]
    assert x_sample.shape[1] == 1

    c_all = jnp.concatenate([c_sample, c_prompt], axis=0)
    mod = _ada(c_all, ada_w, ada_b)
    mod_kv = _ada(c_all, ada_kv_w[None], ada_kv_b[None])

    def split(a, layer, parts, sample):
        nl, m, n = a.shape
        view = a.reshape(nl, 1, m, n) if sample else a.reshape(nl, m, 1, n)
        return [_Mod(view, layer, p, 0 if sample else nbs, sample) for p in range(parts)]

    def paired(w, ncols):
        return jnp.concatenate([_paired_columns(w[:, :ncols]), w[:, ncols:]], axis=1).astype(BF16)

    w_in_a_b, w_o_a_b = paired(w_in_a[0], 2 * A_WIDTH), w_o_a[0].astype(BF16)
    w_in_b_b, w_o_b_b, w_kv_b = paired(w_in_b[0], B_WIDTH), w_o_b[0].astype(BF16), paired(w_kv, B_KV_WIDTH)
    g_pre0, g_pre1 = g_pre[0:1], g_pre[1:2]
    g_post0, g_post1 = g_post[0:1], g_post[1:2]
    g_kv_r = g_kv[None]
    sinks = jnp.broadcast_to(sinks_b[0][:, None] * LOG2E, (B_Q_HEADS, LANES))

    xs = x_sample.reshape(1, nbs, D_MODEL)
    cos_s, sin_s = _rope_tables(jnp.full((nbs,), PAST_LEN, jnp.int32))
    sh0s, sc0s, gt0s = split(mod, 0, 3, True)
    sh1s, sc1s, gt1s = split(mod, 1, 3, True)
    kshs, kscs = split(mod_kv, 0, 2, True)
    qs, ks, vs, gzs = _inproj_a(xs, sh0s, sc0s, g_pre0, w_in_a_b, cos_s, sin_s, sample=True)
    tiles = [_to_tiles(c[0]) for c in (cache_a_kv_g0, cache_a_kv_g1, cache_a_kv_g2)]

    def job(g, first, count, per_step, prev_out=None):
        assert first % count == 0
        return _ShiftJob(qs, ks, vs, tiles[g], g, first, count, per_step, A_GROUPS[g][1], prev_out)

    proj_steps = nbp * (t // min(TOKEN_TILE, t))

    cos_p, sin_p = _rope_tables(jnp.arange(t))
    sh0, sc0, gt0 = split(mod, 0, 3, False)
    sh1, sc1, gt1 = split(mod, 1, 3, False)
    ksh, ksc = split(mod_kv, 0, 2, False)
    *qkv, gz, ca0, ca1, ca2, shifted2, o2a, lse2a = _inproj_a(
        x_prompt, sh0, sc0, g_pre0, w_in_a_b, cos_p, sin_p, sample=False, job=job(2, 0, proj_steps, 1))
    g1_steps = nbp * _dilated_steps(A_GROUPS[1][1])[1]
    attn = [_dilated_attn(*qkv[0:3], 0),
            _dilated_attn(*qkv[3:6], 1, job=job(0, 0, nbs, nbs // g1_steps)),
            _dilated_attn(*qkv[6:9], 2)]
    os, lses = [a[0] for a in attn], [a[1] for a in attn]
    shifted0, o0, lse0 = attn[1][2:]
    x1, qb, gzb, kd, vd, cb, shifted1, o1, lse1 = _mid_prompt(
        os, lses, gz, x_prompt, gt0, g_post0, w_o_a_b, sh1, sc1, ksh, ksc, g_pre1, g_kv_r, w_in_b_b, w_kv_b,
        cos_p, sin_p, job=job(1, 0, nbs, nbs // proj_steps))
    y_prompt, shifted2, o2b, lse2b = _swa_attn(qb, kd, vd, sinks, gzb, x1, gt1, g_post1, w_o_b_b,
                                               job=job(2, proj_steps, nbs - proj_steps, 1, prev_out=shifted2))
    new_a_prompt = [_from_tiles(c, A_HEADS_PER_GROUP)[None] for c in (ca0, ca1, ca2)]
    new_b_prompt = _from_tiles(cb, B_KV_HEADS)

    os = [o0, o1, jnp.concatenate([o2a, o2b], axis=0)]
    lses = [lse0, lse1, jnp.concatenate([lse2a, lse2b], axis=0)]
    os = [o.astype(BF16)[None, None] for o in os]
    lses = [l[None, None] for l in lses]
    new_a_sample = [_from_tiles(s, A_HEADS_PER_GROUP)[None] for s in (shifted0, shifted1, shifted2)]
    sh1, sc1, gt1, ksh, ksc, gz, gt0 = sh1s, sc1s, gt1s, kshs, kscs, gzs, gt0s
    xs1 = _outproj(os, lses, gz, xs, gt0, g_post0, w_o_a_b)
    qb, gzb, kb, vb = _inproj_b_sample(xs1, sh1, sc1, ksh, ksc, g_pre1, g_kv_r, w_in_b_b, w_kv_b, cos_s, sin_s)
    shifted_b, ob = _sample_attn(qb[0], kb[0], vb[0], _to_tiles(cache_b_kv), sinks=sinks)
    y_sample = _outproj([ob.astype(BF16)[None]], None, gzb, xs1, gt1, g_post1, w_o_b_b).reshape(nbs, 1, D_MODEL)
    new_b_sample = _from_tiles(shifted_b, B_KV_HEADS)

    return (y_prompt, y_sample, *new_a_prompt, new_b_prompt, *new_a_sample, new_b_sample)
```

```python
import functools
from typing import NamedTuple, Optional

import jax
import jax.numpy as jnp
from jax import lax
from jax.experimental import pallas as pl
from jax.experimental.pallas import tpu as pltpu

D_MODEL = 1024
HEAD_DIM = 64
A_GROUPS = ((128, 1), (512, 4), (2048, 16))
A_HEADS_PER_GROUP = 4
GROUP_WIDTH = A_HEADS_PER_GROUP * HEAD_DIM
A_WIDTH = GROUP_WIDTH * len(A_GROUPS)
B_Q_HEADS = 16
B_KV_HEADS = 2
B_GROUP = B_Q_HEADS // B_KV_HEADS
B_WIDTH = B_Q_HEADS * HEAD_DIM
B_KV_WIDTH = B_KV_HEADS * HEAD_DIM
B_WINDOW = 128
BLOCK = 128
PAST_LEN = 16384
ROPE_THETA = 10000.0
EPS = 1e-6
NEG = -1e30
LANES = 128
LOG2E = 1.4426950408889634
LN2 = 0.6931471805599453
Q_SCALE = HEAD_DIM ** -0.5 * LOG2E

BF16 = jnp.bfloat16
F32 = jnp.float32
TOKEN_TILE = 512

_ARB = lambda n: pltpu.CompilerParams(dimension_semantics=("arbitrary",) * n)


HALF = HEAD_DIM // 2
PAIRED_QUARTERS = (0, 2, 1, 3)


def _paired_columns(w):
    d, n = w.shape
    return w.reshape(d, n // LANES, 4, HALF)[:, :, PAIRED_QUARTERS, :].reshape(d, n)


def _rope(x, cos, sin_signed):
    outs = []
    for j in range(x.shape[1] // LANES):
        xs = x[:, j * LANES:(j + 1) * LANES]
        outs.append(xs * cos + pltpu.roll(xs, HEAD_DIM, 1) * sin_signed)
    return outs[0] if len(outs) == 1 else jnp.concatenate(outs, axis=1)


def _standard_lanes(x):
    quarter = lax.broadcasted_iota(jnp.int32, (1, LANES), 1) // HALF
    outs = []
    for j in range(x.shape[1] // LANES):
        xs = x[:, j * LANES:(j + 1) * LANES]
        outs.append(jnp.where(quarter == 1, pltpu.roll(xs, LANES - HALF, 1),
                              jnp.where(quarter == 2, pltpu.roll(xs, HALF, 1), xs)))
    return outs[0] if len(outs) == 1 else jnp.concatenate(outs, axis=1)


def _store_standard_rows(ref, xt, dst):
    for s in range(xt.shape[0] // LANES):
        for quarter, src in enumerate(PAIRED_QUARTERS):
            row, src_row = s * LANES + quarter * HALF, s * LANES + src * HALF
            ref[0, row:row + HALF, dst] = xt[src_row:src_row + HALF]


def _normalize(x):
    return x * lax.rsqrt(jnp.mean(x * x, axis=-1, keepdims=True) + EPS)


def _dot(a, b):
    return jnp.dot(a, b, preferred_element_type=F32)


def _ada_kernel(c_ref, w_ref, b_ref, o_ref):
    s = jax.nn.silu(c_ref[...]).astype(BF16)
    o_ref[...] = _dot(s, w_ref[...].astype(BF16)) + b_ref[...]


def _ada(c, w, b):
    nl, _, n = w.shape
    m = c.shape[0]
    tn = 1024
    return pl.pallas_call(
        _ada_kernel,
        grid=(nl, n // tn),
        in_specs=[
            pl.BlockSpec((m, D_MODEL), lambda l, j: (0, 0)),
            pl.BlockSpec((None, D_MODEL, tn), lambda l, j: (l, 0, j)),
            pl.BlockSpec((None, 1, tn), lambda l, j: (l, 0, j)),
        ],
        out_specs=pl.BlockSpec((None, m, tn), lambda l, j: (l, 0, j)),
        out_shape=jax.ShapeDtypeStruct((nl, m, n), F32),
        compiler_params=_ARB(2),
        name="ada",
    )(c, w, b.reshape(nl, 1, n))


ROW_CHUNK = 512


def _row_chunks(tm):
    size = min(ROW_CHUNK, tm)
    return [slice(c * size, (c + 1) * size) for c in range(tm // size)]


def _mod_rows(ref, rows):
    return ref[0] if ref.shape[1] == 1 else ref[0, rows, :]


def _inproj_a_body(x_ref, sh_ref, sc_ref, g_ref, w_ref, cos_ref, sin_ref, rows):
    h = _normalize(x_ref[0, rows, :]) * g_ref[...]
    hb = (h * (1.0 + _mod_rows(sc_ref, rows)) + _mod_rows(sh_ref, rows)).astype(BF16)
    cos, sin = cos_ref[rows, :], sin_ref[rows, :]
    q = _rope(_dot(hb, w_ref[:, 0:A_WIDTH]), cos, sin) * Q_SCALE
    k = _rope(_dot(hb, w_ref[:, A_WIDTH:2 * A_WIDTH]), cos, sin)
    v = _dot(hb, w_ref[:, 2 * A_WIDTH:3 * A_WIDTH])
    gz = jax.nn.silu(_dot(hb, w_ref[:, 3 * A_WIDTH:4 * A_WIDTH]))
    return q, k, v, gz


def _inproj_a_prompt_kernel(x_ref, sh_ref, sc_ref, g_ref, w_ref, cos_ref, sin_ref, *rest, tm, nt, job):
    job_in, main_out, job_out, (scr,) = _split_job_refs(rest, job, 13)
    qkv_refs, (gz_ref, c0_ref, c1_ref, c2_ref) = main_out[:9], main_out[9:]
    i = pl.program_id(1)
    if job:
        _run_job(job, job_in, job_out, pl.program_id(0) * nt + i)
    slabs_per_group = GROUP_WIDTH // LANES
    for c, rows in enumerate(_row_chunks(tm)):
        n = rows.stop - rows.start
        q, k, v, gz = _inproj_a_body(x_ref, sh_ref, sc_ref, g_ref, w_ref, cos_ref, sin_ref, rows)
        gz_ref[0, rows, :] = gz.astype(BF16)
        for which, val in enumerate((q, k, v)):
            for j in range(A_WIDTH // LANES):
                scr[c, which, j] = val[:, j * LANES:(j + 1) * LANES]
            for g, (_, dil) in enumerate(A_GROUPS):
                out_ref = qkv_refs[3 * g + which]
                dst = slice(rows.start // dil, rows.stop // dil)
                for r in range(dil):
                    for h in range(slabs_per_group):
                        part = scr[c, which, g * slabs_per_group + h, pl.ds(r, n // dil, stride=dil), :]
                        out_ref[0, r, dst, h * LANES:(h + 1) * LANES] = part.astype(BF16)
        for g, c_ref in enumerate((c0_ref, c1_ref, c2_ref)):
            win = A_GROUPS[g][0]
            width = min(tm, win)
            first_tile = nt - max(win // tm, 1)
            lo = max(rows.start, tm - width)
            if lo >= rows.stop:
                continue

            @pl.when(i >= first_tile)
            def _(g=g, c_ref=c_ref, lo=lo, k=k, v=v, rows=rows, width=width):
                cols = slice(g * GROUP_WIDTH, (g + 1) * GROUP_WIDTH)
                dst = slice(lo - (tm - width), rows.stop - (tm - width))
                _store_standard_rows(c_ref, k[lo - rows.start:, cols].T, dst)
                c_ref[0, GROUP_WIDTH:2 * GROUP_WIDTH, dst] = v[lo - rows.start:, cols].T


def _inproj_a_sample_kernel(x_ref, sh_ref, sc_ref, g_ref, w_ref, cos_ref, sin_ref,
                            q_ref, k_ref, v_ref, gz_ref):
    q, k, v, gz = _inproj_a_body(x_ref, sh_ref, sc_ref, g_ref, w_ref, cos_ref, sin_ref, slice(0, x_ref.shape[1]))
    q_ref[0] = _standard_lanes(q)
    k_ref[0] = _standard_lanes(k)
    v_ref[0] = v
    gz_ref[0] = gz.astype(BF16)


class _Mod(NamedTuple):
    array: jax.Array
    layer: int
    part: int
    row0: int
    per_row: bool


def _mod_spec(m, tm):
    if m.per_row:
        return pl.BlockSpec((None, 1, tm, D_MODEL), lambda b, i: (m.layer, 0, m.row0 // tm + i, m.part))
    return pl.BlockSpec((None, 1, 1, D_MODEL), lambda b, i: (m.layer, m.row0 + b, 0, m.part))


def _arrays(args):
    return [a.array if isinstance(a, _Mod) else a for a in args]


def _inproj_a(x, shift, scale, g, w, cos, sin, *, sample, job=None):
    nb, t, _ = x.shape
    tm = min(TOKEN_TILE, t)
    nt = t // tm
    row = lambda width: pl.BlockSpec((1, tm, width), lambda b, i: (b, i, 0))
    in_specs = [
        row(D_MODEL), _mod_spec(shift, tm), _mod_spec(scale, tm),
        pl.BlockSpec((1, D_MODEL), lambda b, i: (0, 0)),
        pl.BlockSpec((D_MODEL, 4 * A_WIDTH), lambda b, i: (0, 0)),
        pl.BlockSpec((tm, LANES), lambda b, i: (i, 0)),
        pl.BlockSpec((tm, LANES), lambda b, i: (i, 0)),
    ]
    if sample:
        out_shape = [jax.ShapeDtypeStruct((nb, t, A_WIDTH), dt) for dt in (F32, F32, F32, BF16)]
        return pl.pallas_call(
            _inproj_a_sample_kernel, grid=(nb, nt), in_specs=in_specs,
            out_specs=[row(A_WIDTH)] * 4, out_shape=out_shape,
            compiler_params=_ARB(2), name="inproj_a_sample",
        )(*_arrays([x, shift, scale, g, w, cos, sin]))
    out_shape, out_specs = [], []
    for _, dil in A_GROUPS:
        out_shape += [jax.ShapeDtypeStruct((nb, dil, t // dil, GROUP_WIDTH), BF16)] * 3
        out_specs += [pl.BlockSpec((1, dil, tm // dil, GROUP_WIDTH), lambda b, i: (b, 0, i, 0))] * 3
    out_shape.append(jax.ShapeDtypeStruct((nb, t, A_WIDTH), BF16))
    out_specs.append(row(A_WIDTH))
    for win, _ in A_GROUPS:
        rows = min(tm, win)
        first_tile = nt - max(win // tm, 1)
        out_shape.append(jax.ShapeDtypeStruct((nb, 2 * GROUP_WIDTH, min(win, t)), F32))
        out_specs.append(pl.BlockSpec(
            (1, 2 * GROUP_WIDTH, rows),
            lambda b, i, first_tile=first_tile: (b, 0, jnp.maximum(i - first_tile, 0))))
    args, aliases = [x, shift, scale, g, w, cos, sin], {}
    if job is not None:
        assert job.count == nb * nt * job.per_step
        j_in, j_args, j_out, j_shape, aliases = _job_operands(job, lambda b, i: b * nt + i, len(args),
                                                              len(out_specs))
        in_specs, args = in_specs + j_in, args + j_args
        out_specs, out_shape = out_specs + j_out, out_shape + j_shape
    return pl.pallas_call(
        functools.partial(_inproj_a_prompt_kernel, tm=tm, nt=nt, job=job.static if job else None),
        grid=(nb, nt), in_specs=in_specs, out_specs=out_specs, out_shape=out_shape,
        scratch_shapes=[pltpu.VMEM((tm // min(ROW_CHUNK, tm), 3, A_WIDTH // LANES, min(ROW_CHUNK, tm), LANES), F32)],
        input_output_aliases=aliases, compiler_params=_ARB(2), name="inproj_a_prompt",
    )(*_arrays(args))


def _query_minus_key():
    kj = lax.broadcasted_iota(jnp.int32, (2 * BLOCK, BLOCK), 0)
    qi = lax.broadcasted_iota(jnp.int32, (2 * BLOCK, BLOCK), 1)
    return qi - kj


def _head_attend(st, mask, vt_ext, sink):
    p, m = _head_probs(st, mask, sink)
    o_t, l = _head_finish(_dot(vt_ext, p), m, sink)
    return o_t, m, l


def _head_probs(st, mask, sink):
    st = jnp.where(mask, st, NEG)
    m = jnp.max(st, axis=0, keepdims=True)
    if sink is not None:
        m = jnp.maximum(m, sink)
    return jnp.exp2(st - m).astype(BF16), m


def _head_finish(ext, m, sink):
    l = ext[HEAD_DIM:HEAD_DIM + 1, :]
    if sink is not None:
        l = l + jnp.exp2(sink - m)
    return ext[0:HEAD_DIM, :] * (1.0 / l), l


def _values_ext(vt_prev, vt_cur, head):
    rows = slice(head * HEAD_DIM, (head + 1) * HEAD_DIM)
    ones = jnp.ones((16, 2 * BLOCK), BF16)
    return jnp.concatenate([jnp.concatenate([vt_prev[rows], vt_cur[rows]], axis=1), ones], axis=0)


def _block_diag_queries(qt, heads, kv_of_head, kv_heads):
    zeros = jnp.zeros((HALF, BLOCK), BF16)
    cols = []
    for h in range(heads):
        src = (h // 2) * LANES + (h % 2) * HALF
        lo, hi = qt[src:src + HALF], qt[src + HEAD_DIM:src + HEAD_DIM + HALF]
        kv = kv_of_head(h)
        pieces = []
        for slab in range(kv_heads // 2):
            for quarter in range(4):
                mine = slab == kv // 2 and quarter % 2 == kv % 2
                pieces.append((lo if quarter < 2 else hi) if mine else zeros)
        cols.append(jnp.concatenate(pieces, axis=0))
    return jnp.concatenate(cols, axis=1)


ATTN_BLOCKS_IN_FLIGHT = 16


def _dilated_attn_kernel(q_ref, k_ref, v_ref, *rest, job):
    job_in, (o_ref, lse_ref), job_out, (vt_ref,) = _split_job_refs(rest, job, 2)
    if job:
        _run_job(job, job_in, job_out, pl.program_id(0) * pl.num_programs(1) + pl.program_id(1))
    nres, m_rows = q_ref.shape[1], q_ref.shape[2]
    nblk = m_rows // BLOCK
    base = _query_minus_key()
    heads = A_HEADS_PER_GROUP

    def block(r, i):
        prev = jnp.maximum(i - 1, 0)
        dist = base + (i - prev) * BLOCK
        mask = (dist >= 0) & (dist <= BLOCK)
        qrows = pl.ds(pl.multiple_of(i * BLOCK, BLOCK), BLOCK)
        k = k_ref[0, r, pl.ds(pl.multiple_of(prev * BLOCK, BLOCK), 2 * BLOCK), :]
        qd = _block_diag_queries(q_ref[0, r, qrows, :].T, heads, lambda h: h, heads)
        st_all = _dot(k, qd)
        vt_cur = v_ref[0, r, qrows, :].T
        vt_ref[r, i] = vt_cur
        vt_prev = vt_ref[r, prev]
        outs, lses = [], []
        for h in range(heads):
            o_t, m, l = _head_attend(st_all[:, h * BLOCK:(h + 1) * BLOCK], mask,
                                     _values_ext(vt_prev, vt_cur, h), None)
            outs.append(o_t)
            lses.append(jnp.broadcast_to(m * LN2 + jnp.log(l), (HEAD_DIM, BLOCK)))
        o_ref[0, r, qrows, :] = jnp.concatenate(outs, axis=0).T.astype(BF16)
        lse_ref[0, r, qrows, :] = jnp.concatenate(lses, axis=0).T

    if nres * nblk <= ATTN_BLOCKS_IN_FLIGHT:
        for r in range(nres):
            for i in range(nblk):
                block(r, i)
    else:
        assert nres == 1 and nblk % ATTN_BLOCKS_IN_FLIGHT == 0

        def chunk(c, carry):
            for u in range(ATTN_BLOCKS_IN_FLIGHT):
                block(0, c * ATTN_BLOCKS_IN_FLIGHT + u)
            return carry

        lax.fori_loop(0, nblk // ATTN_BLOCKS_IN_FLIGHT, chunk, 0)


def _dilated_steps(dil):
    nres = max(1, dil // 2)
    return nres, dil // nres


def _dilated_attn(q, k, v, g, job=None):
    nb, dil, m_rows, _ = q.shape
    nres, nsteps = _dilated_steps(dil)
    blk = pl.BlockSpec((1, nres, m_rows, GROUP_WIDTH), lambda b, r: (b, r, 0, 0))
    in_specs, args, aliases = [blk, blk, blk], [q, k, v], {}
    out_specs = [blk, blk]
    out_shape = [jax.ShapeDtypeStruct(q.shape, BF16), jax.ShapeDtypeStruct(q.shape, F32)]
    if job is not None:
        assert job.count == nb * nsteps * job.per_step
        j_in, j_args, j_out, j_shape, aliases = _job_operands(job, lambda b, r: b * nsteps + r, len(args), 2)
        in_specs, args = in_specs + j_in, args + j_args
        out_specs, out_shape = out_specs + j_out, out_shape + j_shape
    return pl.pallas_call(
        functools.partial(_dilated_attn_kernel, job=job.static if job else None),
        grid=(nb, nsteps), in_specs=in_specs, out_specs=out_specs, out_shape=out_shape,
        scratch_shapes=[pltpu.VMEM((nres, m_rows // BLOCK, GROUP_WIDTH, BLOCK), BF16)],
        input_output_aliases=aliases, compiler_params=_ARB(2), name=f"dilated_attn_g{g}",
    )(*_arrays(args))


SWA_QUERY_TILE = 512


def _swa_attn_kernel(q_ref, k_ref, v_ref, sink_ref, gz_ref, x_ref, gate_ref, g_ref, w_ref, *rest, job):
    job_in, (y_ref,), job_out, (vt_ref, o_scr) = _split_job_refs(rest, job, 1)
    j = pl.program_id(1)
    nq = q_ref.shape[1] // BLOCK
    base = _query_minus_key()
    if job:
        new_cols = _run_job(job, job_in, job_out, pl.program_id(0) * pl.num_programs(1) + j, shift_now=False)

    for i in range(nq):
        cur = j * nq + i
        prev = jnp.maximum(cur - 1, 0)
        dist = base + (cur - prev) * BLOCK
        mask = (dist >= 0) & (dist < B_WINDOW)
        qrows = slice(i * BLOCK, (i + 1) * BLOCK)
        k = k_ref[0, pl.ds(pl.multiple_of(prev * BLOCK, BLOCK), 2 * BLOCK), :]
        qt = q_ref[0, qrows, :].T
        vt_cur = v_ref[0, pl.ds(pl.multiple_of(cur * BLOCK, BLOCK), BLOCK), :].T
        vt_ref[cur] = vt_cur
        vt_prev = vt_ref[prev]
        outs = []
        for kvh in range(B_KV_HEADS):
            qd = _block_diag_queries(qt[kvh * B_GROUP * HEAD_DIM:(kvh + 1) * B_GROUP * HEAD_DIM], B_GROUP,
                                     lambda h, kvh=kvh: kvh, B_KV_HEADS)
            st_all = _dot(k, qd)
            sinks = [sink_ref[kvh * B_GROUP + gq:kvh * B_GROUP + gq + 1, :] for gq in range(B_GROUP)]
            probs = [_head_probs(st_all[:, gq * BLOCK:(gq + 1) * BLOCK], mask, sinks[gq]) for gq in range(B_GROUP)]
            ext = _dot(_values_ext(vt_prev, vt_cur, kvh), jnp.concatenate([p for p, _ in probs], axis=1))
            for gq in range(B_GROUP):
                outs.append(_head_finish(ext[:, gq * BLOCK:(gq + 1) * BLOCK], probs[gq][1], sinks[gq])[0])
        o_scr[qrows, :] = jnp.concatenate(outs, axis=0).T.astype(BF16)
        if job:
            _job_shift_part(job, job_in, job_out, new_cols, i, nq)
    a = (o_scr[...].astype(F32) * gz_ref[0].astype(F32)).astype(BF16)
    _finish([a], w_ref, x_ref, gate_ref, g_ref, y_ref)


def _swa_attn(q, k, v, sinks, gz, x, gate, g, w, job=None):
    nb, t, _ = q.shape
    tq = min(SWA_QUERY_TILE, t)
    nj = t // tq
    qblk = pl.BlockSpec((1, tq, B_WIDTH), lambda b, j: (b, j, 0))
    xblk = pl.BlockSpec((1, tq, D_MODEL), lambda b, j: (b, j, 0))
    kvblk = pl.BlockSpec((1, t, B_KV_WIDTH), lambda b, j: (b, 0, 0))
    in_specs = [qblk, kvblk, kvblk, pl.BlockSpec((B_Q_HEADS, LANES), lambda b, j: (0, 0)),
                qblk, xblk, _mod_spec(gate, tq), pl.BlockSpec((1, D_MODEL), lambda b, j: (0, 0)),
                pl.BlockSpec((B_WIDTH, D_MODEL), lambda b, j: (0, 0))]
    args, aliases = [q, k, v, sinks, gz, x, gate, g, w], {}
    out_specs, out_shape = [xblk], [jax.ShapeDtypeStruct((nb, t, D_MODEL), F32)]
    if job is not None:
        assert job.count == nb * nj * job.per_step
        j_in, j_args, j_out, j_shape, aliases = _job_operands(job, lambda b, j: b * nj + j, len(args), 1)
        in_specs, args = in_specs + j_in, args + j_args
        out_specs, out_shape = out_specs + j_out, out_shape + j_shape
    return pl.pallas_call(
        functools.partial(_swa_attn_kernel, job=job.static if job else None),
        grid=(nb, nj), in_specs=in_specs, out_specs=out_specs, out_shape=out_shape,
        scratch_shapes=[pltpu.VMEM((t // BLOCK, B_KV_WIDTH, BLOCK), BF16), pltpu.VMEM((tq, B_WIDTH), BF16)],
        input_output_aliases=aliases, compiler_params=_ARB(2), name="swa_attn",
    )(*_arrays(args))


def _finish_value(a_parts, w_ref, x, gate, g):
    acc = None
    row = 0
    for a in a_parts:
        part = _dot(a, w_ref[row:row + a.shape[1], :])
        acc = part if acc is None else acc + part
        row += a.shape[1]
    return x + gate * (_normalize(acc) * g)


def _finish(a_parts, w_ref, x_ref, gate_ref, g_ref, y_ref):
    y_ref[0] = _finish_value(a_parts, w_ref, x_ref[0], gate_ref[0], g_ref[...])


def _in_row_order(ref, scr):
    dil = ref.shape[1]
    if dil == 1:
        return ref[0, 0].astype(F32)
    nslab = ref.shape[3] // LANES
    for r in range(dil):
        for h in range(nslab):
            scr[h, pl.ds(r, ref.shape[2], stride=dil), :] = ref[0, r, :, h * LANES:(h + 1) * LANES].astype(F32)
    return jnp.concatenate([scr[h] for h in range(nslab)], axis=1)


def _mixture_parts(o_refs, l_refs, gz_ref, scratch):
    os = [_in_row_order(r, s) for r, s in zip(o_refs, scratch[0:3])]
    lses = [_in_row_order(r, s) for r, s in zip(l_refs, scratch[3:6])]
    top = jnp.maximum(jnp.maximum(lses[0], lses[1]), lses[2])
    es = [jnp.exp(l - top) for l in lses]
    inv = 1.0 / (es[0] + es[1] + es[2])
    parts = []
    for g in range(len(A_GROUPS)):
        gz = gz_ref[0, :, g * GROUP_WIDTH:(g + 1) * GROUP_WIDTH].astype(F32)
        parts.append((os[g] * (es[g] * inv) * gz).astype(BF16))
    return parts


def _outproj_mix_kernel(o0_ref, o1_ref, o2_ref, l0_ref, l1_ref, l2_ref, gz_ref, x_ref, gate_ref, g_ref, w_ref,
                        y_ref, *scratch):
    parts = _mixture_parts((o0_ref, o1_ref, o2_ref), (l0_ref, l1_ref, l2_ref), gz_ref, scratch)
    _finish(parts, w_ref, x_ref, gate_ref, g_ref, y_ref)


def _outproj_kernel(o_ref, gz_ref, x_ref, gate_ref, g_ref, w_ref, y_ref):
    a = (o_ref[0].astype(F32) * gz_ref[0].astype(F32)).astype(BF16)
    _finish([a], w_ref, x_ref, gate_ref, g_ref, y_ref)


def _outproj(os, lses, gz, x, gate, g, w):
    nb, t, _ = x.shape
    tm = min(TOKEN_TILE, t)
    row = lambda width: pl.BlockSpec((1, tm, width), lambda b, i: (b, i, 0))
    width = w.shape[0]
    tail_specs = [row(width), row(D_MODEL), _mod_spec(gate, tm),
                  pl.BlockSpec((1, D_MODEL), lambda b, i: (0, 0)),
                  pl.BlockSpec((width, D_MODEL), lambda b, i: (0, 0))]
    scratch = []
    if lses is None:
        kern, name = _outproj_kernel, "outproj_b"
        in_specs = [row(width)] + tail_specs
        args = (os[0], gz, x, gate, g, w)
    else:
        kern, name = _outproj_mix_kernel, "outproj_a"
        split = lambda a: pl.BlockSpec((1, a.shape[1], tm // a.shape[1], GROUP_WIDTH), lambda b, i: (b, 0, i, 0))
        in_specs = [split(a) for a in (*os, *lses)] + tail_specs
        args = (*os, *lses, gz, x, gate, g, w)
        scratch = [pltpu.VMEM((GROUP_WIDTH // LANES, tm, LANES), F32)] * 6
    return pl.pallas_call(
        kern, grid=(nb, t // tm), in_specs=in_specs, out_specs=row(D_MODEL),
        out_shape=jax.ShapeDtypeStruct((nb, t, D_MODEL), F32), scratch_shapes=scratch,
        compiler_params=_ARB(2), name=name,
    )(*_arrays(args))


def _inproj_b_body(x, sh_ref, sc_ref, ksh_ref, ksc_ref, g_ref, gkv_ref, w_ref, wkv_ref, cos_ref, sin_ref, rows):
    xn = _normalize(x)
    hb = ((xn * g_ref[...]) * (1.0 + _mod_rows(sc_ref, rows)) + _mod_rows(sh_ref, rows)).astype(BF16)
    hk = ((xn * gkv_ref[...]) * (1.0 + _mod_rows(ksc_ref, rows)) + _mod_rows(ksh_ref, rows)).astype(BF16)
    cos, sin = cos_ref[rows, :], sin_ref[rows, :]
    q = _rope(_dot(hb, w_ref[:, 0:B_WIDTH]), cos, sin) * Q_SCALE
    gz = jax.nn.silu(_dot(hb, w_ref[:, B_WIDTH:2 * B_WIDTH]))
    kv = _dot(hk, wkv_ref[...])
    k = _rope(kv[:, 0:B_KV_WIDTH], cos, sin)
    v = kv[:, B_KV_WIDTH:2 * B_KV_WIDTH]
    return q, gz, k, v


def _mid_prompt_kernel(o0_ref, o1_ref, o2_ref, l0_ref, l1_ref, l2_ref, gza_ref, x_ref, gate_ref, gpost_ref, wo_ref,
                       sh_ref, sc_ref, ksh_ref, ksc_ref, g_ref, gkv_ref, w_ref, wkv_ref, cos_ref, sin_ref,
                       *rest, tm, nt, job):
    job_in, (x1_ref, q_ref, gz_ref, kd_ref, vd_ref, c_ref), job_out, scratch = _split_job_refs(rest, job, 6)
    if job:
        _run_job(job, job_in, job_out, pl.program_id(0) * nt + pl.program_id(1))
    parts = _mixture_parts((o0_ref, o1_ref, o2_ref), (l0_ref, l1_ref, l2_ref), gza_ref, scratch)
    x1 = _finish_value(parts, wo_ref, x_ref[0], gate_ref[0], gpost_ref[...])
    x1_ref[0] = x1
    q, gz, k, v = _inproj_b_body(x1, sh_ref, sc_ref, ksh_ref, ksc_ref, g_ref, gkv_ref, w_ref, wkv_ref,
                                 cos_ref, sin_ref, slice(0, tm))
    q_ref[0] = q.astype(BF16)
    gz_ref[0] = gz.astype(BF16)
    kd_ref[0] = k.astype(BF16)
    vd_ref[0] = v.astype(BF16)

    @pl.when(pl.program_id(1) == nt - 1)
    def _():
        _store_standard_rows(c_ref, k[tm - B_WINDOW:, :].T, slice(None))
        c_ref[0, B_KV_WIDTH:2 * B_KV_WIDTH, :] = v[tm - B_WINDOW:, :].T


def _inproj_b_sample_kernel(x_ref, sh_ref, sc_ref, ksh_ref, ksc_ref, g_ref, gkv_ref, w_ref, wkv_ref, cos_ref,
                            sin_ref, q_ref, gz_ref, k_ref, v_ref):
    q, gz, k, v = _inproj_b_body(x_ref[0], sh_ref, sc_ref, ksh_ref, ksc_ref, g_ref, gkv_ref, w_ref, wkv_ref,
                                 cos_ref, sin_ref, slice(0, x_ref.shape[1]))
    q_ref[0] = _standard_lanes(q)
    gz_ref[0] = gz.astype(BF16)
    k_ref[0] = _standard_lanes(k)
    v_ref[0] = v


def _inproj_b_sample(x, shift, scale, kshift, kscale, g, gkv, w, wkv, cos, sin):
    nb, t, _ = x.shape
    tm = min(TOKEN_TILE, t)
    row = lambda width: pl.BlockSpec((1, tm, width), lambda b, i: (b, i, 0))
    vec = pl.BlockSpec((1, D_MODEL), lambda b, i: (0, 0))
    tab = pl.BlockSpec((tm, LANES), lambda b, i: (i, 0))
    in_specs = [row(D_MODEL), _mod_spec(shift, tm), _mod_spec(scale, tm), _mod_spec(kshift, tm),
                _mod_spec(kscale, tm), vec, vec,
                pl.BlockSpec((D_MODEL, 2 * B_WIDTH), lambda b, i: (0, 0)),
                pl.BlockSpec((D_MODEL, 2 * B_KV_WIDTH), lambda b, i: (0, 0)), tab, tab]
    return pl.pallas_call(
        _inproj_b_sample_kernel, grid=(nb, t // tm), in_specs=in_specs,
        out_specs=[row(B_WIDTH), row(B_WIDTH), row(B_KV_WIDTH), row(B_KV_WIDTH)],
        out_shape=[jax.ShapeDtypeStruct((nb, t, B_WIDTH), F32), jax.ShapeDtypeStruct((nb, t, B_WIDTH), BF16),
                   jax.ShapeDtypeStruct((nb, t, B_KV_WIDTH), F32), jax.ShapeDtypeStruct((nb, t, B_KV_WIDTH), F32)],
        compiler_params=_ARB(2), name="inproj_b_sample",
    )(*_arrays([x, shift, scale, kshift, kscale, g, gkv, w, wkv, cos, sin]))


def _mid_prompt(os, lses, gza, x, gate, gpost, wo, shift, scale, kshift, kscale, g, gkv, w, wkv, cos, sin, job=None):
    nb, t, _ = x.shape
    tm = min(TOKEN_TILE, t)
    nt = t // tm
    row = lambda width: pl.BlockSpec((1, tm, width), lambda b, i: (b, i, 0))
    vec = pl.BlockSpec((1, D_MODEL), lambda b, i: (0, 0))
    tab = pl.BlockSpec((tm, LANES), lambda b, i: (i, 0))
    full = lambda a: pl.BlockSpec(a.shape, lambda b, i: (0, 0))
    split = lambda a: pl.BlockSpec((1, a.shape[1], tm // a.shape[1], GROUP_WIDTH), lambda b, i: (b, 0, i, 0))
    in_specs = ([split(a) for a in (*os, *lses)]
                + [row(A_WIDTH), row(D_MODEL), _mod_spec(gate, tm), vec, full(wo)]
                + [_mod_spec(m, tm) for m in (shift, scale, kshift, kscale)] + [vec, vec, full(w), full(wkv), tab, tab])
    args = [*os, *lses, gza, x, gate, gpost, wo, shift, scale, kshift, kscale, g, gkv, w, wkv, cos, sin]
    out_specs = [row(D_MODEL), row(B_WIDTH), row(B_WIDTH), row(B_KV_WIDTH), row(B_KV_WIDTH),
                 pl.BlockSpec((1, 2 * B_KV_WIDTH, B_WINDOW), lambda b, i: (b, 0, 0))]
    out_shape = [jax.ShapeDtypeStruct((nb, t, D_MODEL), F32),
                 jax.ShapeDtypeStruct((nb, t, B_WIDTH), BF16), jax.ShapeDtypeStruct((nb, t, B_WIDTH), BF16),
                 jax.ShapeDtypeStruct((nb, t, B_KV_WIDTH), BF16), jax.ShapeDtypeStruct((nb, t, B_KV_WIDTH), BF16),
                 jax.ShapeDtypeStruct((nb, 2 * B_KV_WIDTH, B_WINDOW), F32)]
    aliases = {}
    if job is not None:
        assert job.count == nb * nt * job.per_step
        j_in, j_args, j_out, j_shape, aliases = _job_operands(job, lambda b, i: b * nt + i, len(args),
                                                              len(out_specs))
        in_specs, args = in_specs + j_in, args + j_args
        out_specs, out_shape = out_specs + j_out, out_shape + j_shape
    return pl.pallas_call(
        functools.partial(_mid_prompt_kernel, tm=tm, nt=nt, job=job.static if job else None),
        grid=(nb, nt), in_specs=in_specs, out_specs=out_specs, out_shape=out_shape,
        scratch_shapes=[pltpu.VMEM((GROUP_WIDTH // LANES, tm, LANES), F32)] * 6,
        input_output_aliases=aliases, compiler_params=_ARB(2), name="mid_prompt",
    )(*_arrays(args))


def _eye():
    r = lax.broadcasted_iota(jnp.int32, (LANES, LANES), 0)
    c = lax.broadcasted_iota(jnp.int32, (LANES, LANES), 1)
    return r == c


def _row_to_col(row):
    eye = _eye()
    chunks = [jnp.sum(jnp.where(eye, row[:, j * LANES:(j + 1) * LANES], 0.0), axis=1, keepdims=True)
              for j in range(row.shape[1] // LANES)]
    return chunks[0] if len(chunks) == 1 else jnp.concatenate(chunks, axis=0)


def _shift_rows(cache_ref, out_ref, e, new_col, part=0, nparts=1):
    ntile = cache_ref.shape[2] // LANES
    first, last = part * ntile // nparts, (part + 1) * ntile // nparts
    if first == last:
        return
    lane = lax.broadcasted_iota(jnp.int32, (1, LANES), 1)
    nxt = pltpu.roll(cache_ref[e, :, first * LANES:(first + 1) * LANES], LANES - 1, 1)
    for j in range(first, last):
        cur = nxt
        if j + 1 < ntile:
            nxt = pltpu.roll(cache_ref[e, :, (j + 1) * LANES:(j + 2) * LANES], LANES - 1, 1)
            fill = nxt
        else:
            fill = new_col
        out_ref[e, :, j * LANES:(j + 1) * LANES] = jnp.where(lane < LANES - 1, cur, fill)


_NT = (((1,), (1,)), ((), ()))


def _sample_dilated_step(q_ref, kn_ref, vn_ref, cache_ref, out_cache_ref, o_ref, lse_ref, step, *, dil, bb,
                         shift_now=True):
    width, length = q_ref.shape[1], cache_ref.shape[2]
    new_cols = []
    sel = (lax.broadcasted_iota(jnp.int32, (8, width), 0)
           == lax.broadcasted_iota(jnp.int32, (8, width), 1) // HEAD_DIM)
    valid = lax.broadcasted_iota(jnp.int32, (1, length), 1) % dil == 0
    per_head = lambda x: jnp.sum(jnp.where(sel, x, 0.0), axis=0, keepdims=True)
    for e in range(bb):
        row = pl.ds(step * bb + e, 1)
        q, kn, vn = q_ref[row, :], kn_ref[row, :], vn_ref[row, :]
        qb = jnp.where(sel, q, 0.0)
        kt = cache_ref[e, 0:width, :].astype(BF16)
        vt = cache_ref[e, width:2 * width, :].astype(BF16)
        s = jnp.where(valid, _dot(qb.astype(BF16), kt), NEG)
        s_new = jnp.sum(qb * kn, axis=1, keepdims=True)
        m = jnp.maximum(jnp.max(s, axis=1, keepdims=True), s_new)
        p = jnp.exp2(s - m)
        p_new = jnp.exp2(s_new - m)
        l = per_head(jnp.sum(p, axis=1, keepdims=True) + p_new)
        o = lax.dot_general(p.astype(BF16), vt, _NT, preferred_element_type=F32)
        o_ref[row, :] = (per_head(o) + per_head(p_new) * vn) * (1.0 / l)
        lse_ref[row, :] = per_head(m) * LN2 + jnp.log(l)
        new_cols.append(jnp.concatenate([_row_to_col(kn), _row_to_col(vn)], axis=0))
        if shift_now:
            _shift_rows(cache_ref, out_cache_ref, e, new_cols[-1])
    return new_cols


def _sample_dilated_kernel(q_ref, kn_ref, vn_ref, cache_ref, out_cache_ref, o_ref, lse_ref, *, dil, bb):
    _sample_dilated_step(q_ref, kn_ref, vn_ref, cache_ref, out_cache_ref, o_ref, lse_ref, pl.program_id(0),
                         dil=dil, bb=bb)


class _ShiftJob(NamedTuple):
    q: jax.Array
    k_new: jax.Array
    v_new: jax.Array
    cache: jax.Array
    group: int
    first: int
    count: int
    per_step: int
    dil: int
    prev_out: Optional[jax.Array]

    @property
    def static(self):
        return (4 if self.prev_out is None else 5, self.dil, self.per_step)


def _job_operands(job, step_of, inputs_before, outputs_before):
    count, wq = job.count, GROUP_WIDTH
    _, chans, length = job.cache.shape
    first_blk = job.first // job.per_step
    rows = pl.BlockSpec((None, count, wq), lambda *ids: (0, job.first // count, job.group))
    full = pl.BlockSpec((count, wq), lambda *ids: (0, 0))
    tile = pl.BlockSpec((job.per_step, chans, length), lambda *ids: (first_blk + step_of(*ids), 0, 0))
    in_specs, args, aliases = [rows, rows, rows, tile], [job.q, job.k_new, job.v_new, job.cache], {}
    if job.prev_out is not None:
        in_specs.append(pl.BlockSpec(memory_space=pl.ANY))
        args.append(job.prev_out)
        aliases[inputs_before + 4] = outputs_before
    out_shape = [jax.ShapeDtypeStruct(job.cache.shape, F32)] + [jax.ShapeDtypeStruct((count, wq), F32)] * 2
    return in_specs, args, [tile, full, full], out_shape, aliases


def _split_job_refs(rest, job_static, n_main_out):
    n_in = job_static[0] if job_static else 0
    n_out = 3 if job_static else 0
    job_in, rest = rest[:n_in], rest[n_in:]
    return job_in, rest[:n_main_out], rest[n_main_out:n_main_out + n_out], rest[n_main_out + n_out:]


def _run_job(job_static, in_refs, out_refs, step, shift_now=True):
    _, dil, per_step = job_static
    return _sample_dilated_step(*in_refs[:4], *out_refs, step, dil=dil, bb=per_step, shift_now=shift_now)


def _job_shift_part(job_static, in_refs, out_refs, new_cols, part, nparts):
    for e in range(job_static[2]):
        _shift_rows(in_refs[3], out_refs[0], e, new_cols[e], part, nparts)


def _sample_swa_kernel(q_ref, kn_ref, vn_ref, cache_ref, sink_ref, out_cache_ref, o_ref, qexp, oexp, *, bb):
    step = pl.program_id(0)
    nb, length = q_ref.shape[0], cache_ref.shape[2]
    low = lax.broadcasted_iota(jnp.int32, (1, LANES), 1) < HEAD_DIM

    @pl.when(step == 0)
    def _():
        for hq in range(B_Q_HEADS):
            slab = q_ref[:, (hq // 2) * LANES:(hq // 2 + 1) * LANES]
            src_low, dst_low = hq % 2 == 0, hq // B_GROUP == 0
            x = slab if src_low == dst_low else pltpu.roll(slab, HEAD_DIM, 1)
            qexp[hq * nb:(hq + 1) * nb, :] = jnp.where(low if dst_low else jnp.logical_not(low), x, 0.0)

    own_half = (lax.broadcasted_iota(jnp.int32, (B_Q_HEADS, LANES), 0) // B_GROUP
                == lax.broadcasted_iota(jnp.int32, (B_Q_HEADS, LANES), 1) // HEAD_DIM)
    valid = lax.broadcasted_iota(jnp.int32, (1, length), 1) >= 1
    sink = sink_ref[:, 0:1]
    for e in range(bb):
        b = step * bb + e
        heads = pl.ds(b, B_Q_HEADS, stride=nb)
        kn, vn = kn_ref[pl.ds(b, 1), :], vn_ref[pl.ds(b, 1), :]
        qb = qexp[heads, :]
        kt = cache_ref[e, 0:B_KV_WIDTH, :].astype(BF16)
        vt = cache_ref[e, B_KV_WIDTH:2 * B_KV_WIDTH, :].astype(BF16)
        s = jnp.where(valid, _dot(qb.astype(BF16), kt), NEG)
        s_new = jnp.sum(qb * kn, axis=1, keepdims=True)
        m = jnp.maximum(jnp.maximum(jnp.max(s, axis=1, keepdims=True), s_new), sink)
        p = jnp.exp2(s - m)
        p_new = jnp.exp2(s_new - m)
        l = jnp.sum(p, axis=1, keepdims=True) + p_new + jnp.exp2(sink - m)
        o = lax.dot_general(p.astype(BF16), vt, _NT, preferred_element_type=F32)
        oexp[heads, :] = jnp.where(own_half, o + p_new * vn, 0.0) * (1.0 / l)
        _shift_rows(cache_ref, out_cache_ref, e, jnp.concatenate([_row_to_col(kn), _row_to_col(vn)], axis=0))

    @pl.when(step == pl.num_programs(0) - 1)
    def _():
        for j in range(B_Q_HEADS // 2):
            even, odd = oexp[2 * j * nb:(2 * j + 1) * nb, :], oexp[(2 * j + 1) * nb:(2 * j + 2) * nb, :]
            if 2 * j // B_GROUP == 0:
                odd = pltpu.roll(odd, HEAD_DIM, 1)
            else:
                even = pltpu.roll(even, HEAD_DIM, 1)
            o_ref[:, j * LANES:(j + 1) * LANES] = jnp.where(low, even, odd)


def _sample_attn(q, k_new, v_new, cache_t, *, dil=1, sinks=None):
    nb, chans, length = cache_t.shape
    wq, wkv = q.shape[1], k_new.shape[1]
    bb = max(1, min(8, 2048 // length))
    full = lambda w: pl.BlockSpec((nb, w), lambda s: (0, 0))
    tile = pl.BlockSpec((bb, chans, length), lambda s: (s, 0, 0))
    in_specs = [full(wq), full(wkv), full(wkv), tile]
    args = [q, k_new, v_new, cache_t]
    out_specs = [tile, full(wq)]
    out_shape = [jax.ShapeDtypeStruct(cache_t.shape, F32), jax.ShapeDtypeStruct((nb, wq), F32)]
    if sinks is None:
        kern = functools.partial(_sample_dilated_kernel, dil=dil, bb=bb)
        out_specs.append(full(wq))
        out_shape.append(jax.ShapeDtypeStruct((nb, wq), F32))
        scratch = []
    else:
        kern = functools.partial(_sample_swa_kernel, bb=bb)
        in_specs.append(pl.BlockSpec(sinks.shape, lambda s: (0, 0)))
        args.append(sinks)
        scratch = [pltpu.VMEM((B_Q_HEADS * nb, LANES), F32)] * 2
    return pl.pallas_call(
        kern, grid=(nb // bb,), in_specs=in_specs, out_specs=out_specs, out_shape=out_shape,
        scratch_shapes=scratch, compiler_params=_ARB(1),
        name=f"sample_attn_d{dil}_l{length}" if sinks is None else "sample_swa",
    )(*_arrays(args))


def _rope_tables(pos):
    half = HEAD_DIM // 2
    inv = ROPE_THETA ** (-jnp.arange(half, dtype=F32) / half)
    ang = pos.astype(F32)[:, None] * jnp.tile(inv, 4)[None, :]
    sign = jnp.where(jnp.arange(LANES) < HEAD_DIM, -1.0, 1.0).astype(F32)
    return jnp.cos(ang), jnp.sin(ang) * sign[None, :]


def _to_tiles(cache):
    nb, length = cache.shape[0], cache.shape[1]
    return jnp.transpose(cache, (0, 2, 3, 4, 1)).reshape(nb, -1, length)


def _from_tiles(tiles, heads):
    nb, _, length = tiles.shape
    return jnp.transpose(tiles.reshape(nb, 2, heads, HEAD_DIM, length), (0, 4, 1, 2, 3))


def kernel(x_prompt, x_sample, c_prompt, c_sample, cache_a_kv_g0, cache_a_kv_g1, cache_a_kv_g2, cache_b_kv, ada_w,
           ada_b, g_pre, g_post, w_in_a, w_o_a, w_in_b, w_o_b, sinks_b, ada_kv_w, ada_kv_b, g_kv, w_kv):
    nbp, t, _ = x_prompt.shape
    nbs = x_sample.shape[0]
    assert x_sample.shape[1] == 1

    c_all = jnp.concatenate([c_sample, c_prompt], axis=0)
    mod = _ada(c_all, ada_w, ada_b)
    mod_kv = _ada(c_all, ada_kv_w[None], ada_kv_b[None])

    def split(a, layer, parts, sample):
        nl, m, n = a.shape
        view = a.reshape(nl, 1, m, n) if sample else a.reshape(nl, m, 1, n)
        return [_Mod(view, layer, p, 0 if sample else nbs, sample) for p in range(parts)]

    def paired(w, ncols):
        return jnp.concatenate([_paired_columns(w[:, :ncols]), w[:, ncols:]], axis=1).astype(BF16)

    w_in_a_b, w_o_a_b = paired(w_in_a[0], 2 * A_WIDTH), w_o_a[0].astype(BF16)
    w_in_b_b, w_o_b_b, w_kv_b = paired(w_in_b[0], B_WIDTH), w_o_b[0].astype(BF16), paired(w_kv, B_KV_WIDTH)
    g_pre0, g_pre1 = g_pre[0:1], g_pre[1:2]
    g_post0, g_post1 = g_post[0:1], g_post[1:2]
    g_kv_r = g_kv[None]
    sinks = jnp.broadcast_to(sinks_b[0][:, None] * LOG2E, (B_Q_HEADS, LANES))

    xs = x_sample.reshape(1, nbs, D_MODEL)
    cos_s, sin_s = _rope_tables(jnp.full((nbs,), PAST_LEN, jnp.int32))
    sh0s, sc0s, gt0s = split(mod, 0, 3, True)
    sh1s, sc1s, gt1s = split(mod, 1, 3, True)
    kshs, kscs = split(mod_kv, 0, 2, True)
    qs, ks, vs, gzs = _inproj_a(xs, sh0s, sc0s, g_pre0, w_in_a_b, cos_s, sin_s, sample=True)
    tiles = [_to_tiles(c[0]) for c in (cache_a_kv_g0, cache_a_kv_g1, cache_a_kv_g2)]

    def job(g, first, count, per_step, prev_out=None):
        assert first % count == 0
        return _ShiftJob(qs, ks, vs, tiles[g], g, first, count, per_step, A_GROUPS[g][1], prev_out)

    proj_steps = nbp * (t // min(TOKEN_TILE, t))

    cos_p, sin_p = _rope_tables(jnp.arange(t))
    sh0, sc0, gt0 = split(mod, 0, 3, False)
    sh1, sc1, gt1 = split(mod, 1, 3, False)
    ksh, ksc = split(mod_kv, 0, 2, False)
    *qkv, gz, ca0, ca1, ca2, shifted2, o2a, lse2a = _inproj_a(
        x_prompt, sh0, sc0, g_pre0, w_in_a_b, cos_p, sin_p, sample=False, job=job(2, 0, proj_steps, 1))
    g1_steps = nbp * _dilated_steps(A_GROUPS[1][1])[1]
    attn = [_dilated_attn(*qkv[0:3], 0),
            _dilated_attn(*qkv[3:6], 1, job=job(0, 0, nbs, nbs // g1_steps)),
            _dilated_attn(*qkv[6:9], 2)]
    os, lses = [a[0] for a in attn], [a[1] for a in attn]
    shifted0, o0, lse0 = attn[1][2:]
    x1, qb, gzb, kd, vd, cb, shifted1, o1, lse1 = _mid_prompt(
        os, lses, gz, x_prompt, gt0, g_post0, w_o_a_b, sh1, sc1, ksh, ksc, g_pre1, g_kv_r, w_in_b_b, w_kv_b,
        cos_p, sin_p, job=job(1, 0, nbs, nbs // proj_steps))
    y_prompt, shifted2, o2b, lse2b = _swa_attn(qb, kd, vd, sinks, gzb, x1, gt1, g_post1, w_o_b_b,
                                               job=job(2, proj_steps, nbs - proj_steps, 1, prev_out=shifted2))
    new_a_prompt = [_from_tiles(c, A_HEADS_PER_GROUP)[None] for c in (ca0, ca1, ca2)]
    new_b_prompt = _from_tiles(cb, B_KV_HEADS)

    os = [o0, o1, jnp.concatenate([o2a, o2b], axis=0)]
    lses = [lse0, lse1, jnp.concatenate([lse2a, lse2b], axis=0)]
    os = [o.astype(BF16)[None, None] for o in os]
    lses = [l[None, None] for l in lses]
    new_a_sample = [_from_tiles(s, A_HEADS_PER_GROUP)[None] for s in (shifted0, shifted1, shifted2)]
    sh1, sc1, gt1, ksh, ksc, gz, gt0 = sh1s, sc1s, gt1s, kshs, kscs, gzs, gt0s
    xs1 = _outproj(os, lses, gz, xs, gt0, g_post0, w_o_a_b)
    qb, gzb, kb, vb = _inproj_b_sample(xs1, sh1, sc1, ksh, ksc, g_pre1, g_kv_r, w_in_b_b, w_kv_b, cos_s, sin_s)
    shifted_b, ob = _sample_attn(qb[0], kb[0], vb[0], _to_tiles(cache_b_kv), sinks=sinks)
    y_sample = _outproj([ob.astype(BF16)[None]], None, gzb, xs1, gt1, g_post1, w_o_b_b).reshape(nbs, 1, D_MODEL)
    new_b_sample = _from_tiles(shifted_b, B_KV_HEADS)

    return (y_prompt, y_sample, *new_a_prompt, new_b_prompt, *new_a_sample, new_b_sample)
```

```python
import functools
from typing import NamedTuple, Optional

import jax
import jax.numpy as jnp
from jax import lax
from jax.experimental import pallas as pl
from jax.experimental.pallas import tpu as pltpu

D_MODEL = 1024
HEAD_DIM = 64
A_GROUPS = ((128, 1), (512, 4), (2048, 16))
A_HEADS_PER_GROUP = 4
GROUP_WIDTH = A_HEADS_PER_GROUP * HEAD_DIM
A_WIDTH = GROUP_WIDTH * len(A_GROUPS)
B_Q_HEADS = 16
B_KV_HEADS = 2
B_GROUP = B_Q_HEADS // B_KV_HEADS
B_WIDTH = B_Q_HEADS * HEAD_DIM
B_KV_WIDTH = B_KV_HEADS * HEAD_DIM
B_WINDOW = 128
BLOCK = 128
PAST_LEN = 16384
ROPE_THETA = 10000.0
EPS = 1e-6
NEG = -1e30
LANES = 128
LOG2E = 1.4426950408889634
LN2 = 0.6931471805599453
Q_SCALE = HEAD_DIM ** -0.5 * LOG2E

BF16 = jnp.bfloat16
F32 = jnp.float32
TOKEN_TILE = 512

_ARB = lambda n: pltpu.CompilerParams(dimension_semantics=("arbitrary",) * n)


HALF = HEAD_DIM // 2
PAIRED_QUARTERS = (0, 2, 1, 3)


def _paired_columns(w):
    d, n = w.shape
    return w.reshape(d, n // LANES, 4, HALF)[:, :, PAIRED_QUARTERS, :].reshape(d, n)


def _rope(x, cos, sin_signed):
    outs = []
    for j in range(x.shape[1] // LANES):
        xs = x[:, j * LANES:(j + 1) * LANES]
        outs.append(xs * cos + pltpu.roll(xs, HEAD_DIM, 1) * sin_signed)
    return outs[0] if len(outs) == 1 else jnp.concatenate(outs, axis=1)


def _standard_lanes(x):
    quarter = lax.broadcasted_iota(jnp.int32, (1, LANES), 1) // HALF
    outs = []
    for j in range(x.shape[1] // LANES):
        xs = x[:, j * LANES:(j + 1) * LANES]
        outs.append(jnp.where(quarter == 1, pltpu.roll(xs, LANES - HALF, 1),
                              jnp.where(quarter == 2, pltpu.roll(xs, HALF, 1), xs)))
    return outs[0] if len(outs) == 1 else jnp.concatenate(outs, axis=1)


def _store_standard_rows(ref, xt, dst):
    for s in range(xt.shape[0] // LANES):
        for quarter, src in enumerate(PAIRED_QUARTERS):
            row, src_row = s * LANES + quarter * HALF, s * LANES + src * HALF
            ref[0, row:row + HALF, dst] = xt[src_row:src_row + HALF]


def _normalize(x):
    return x * lax.rsqrt(jnp.mean(x * x, axis=-1, keepdims=True) + EPS)


def _dot(a, b):
    return jnp.dot(a, b, preferred_element_type=F32)


def _ada_kernel(c_ref, w_ref, b_ref, o_ref):
    s = jax.nn.silu(c_ref[...]).astype(BF16)
    o_ref[...] = _dot(s, w_ref[...].astype(BF16)) + b_ref[...]


def _ada(c, w, b):
    nl, _, n = w.shape
    m = c.shape[0]
    tn = 1024
    return pl.pallas_call(
        _ada_kernel,
        grid=(nl, n // tn),
        in_specs=[
            pl.BlockSpec((m, D_MODEL), lambda l, j: (0, 0)),
            pl.BlockSpec((None, D_MODEL, tn), lambda l, j: (l, 0, j)),
            pl.BlockSpec((None, 1, tn), lambda l, j: (l, 0, j)),
        ],
        out_specs=pl.BlockSpec((None, m, tn), lambda l, j: (l, 0, j)),
        out_shape=jax.ShapeDtypeStruct((nl, m, n), F32),
        compiler_params=_ARB(2),
        name="ada",
    )(c, w, b.reshape(nl, 1, n))


ROW_CHUNK = 512


def _row_chunks(tm):
    size = min(ROW_CHUNK, tm)
    return [slice(c * size, (c + 1) * size) for c in range(tm // size)]


def _mod_rows(ref, rows):
    return ref[0] if ref.shape[1] == 1 else ref[0, rows, :]


def _inproj_a_body(x_ref, sh_ref, sc_ref, g_ref, w_ref, cos_ref, sin_ref, rows):
    h = _normalize(x_ref[0, rows, :]) * g_ref[...]
    hb = (h * (1.0 + _mod_rows(sc_ref, rows)) + _mod_rows(sh_ref, rows)).astype(BF16)
    cos, sin = cos_ref[rows, :], sin_ref[rows, :]
    q = _rope(_dot(hb, w_ref[:, 0:A_WIDTH]), cos, sin) * Q_SCALE
    k = _rope(_dot(hb, w_ref[:, A_WIDTH:2 * A_WIDTH]), cos, sin)
    v = _dot(hb, w_ref[:, 2 * A_WIDTH:3 * A_WIDTH])
    gz = jax.nn.silu(_dot(hb, w_ref[:, 3 * A_WIDTH:4 * A_WIDTH]))
    return q, k, v, gz


def _inproj_a_prompt_kernel(x_ref, sh_ref, sc_ref, g_ref, w_ref, cos_ref, sin_ref, *rest, tm, nt, job):
    job_in, main_out, job_out, (scr,) = _split_job_refs(rest, job, 13)
    qkv_refs, (gz_ref, c0_ref, c1_ref, c2_ref) = main_out[:9], main_out[9:]
    i = pl.program_id(1)
    if job:
        _run_job(job, job_in, job_out, pl.program_id(0) * nt + i)
    slabs_per_group = GROUP_WIDTH // LANES
    for c, rows in enumerate(_row_chunks(tm)):
        n = rows.stop - rows.start
        q, k, v, gz = _inproj_a_body(x_ref, sh_ref, sc_ref, g_ref, w_ref, cos_ref, sin_ref, rows)
        gz_ref[0, rows, :] = gz.astype(BF16)
        for which, val in enumerate((q, k, v)):
            for j in range(A_WIDTH // LANES):
                scr[c, which, j] = val[:, j * LANES:(j + 1) * LANES]
            for g, (_, dil) in enumerate(A_GROUPS):
                out_ref = qkv_refs[3 * g + which]
                dst = slice(rows.start // dil, rows.stop // dil)
                for r in range(dil):
                    for h in range(slabs_per_group):
                        part = scr[c, which, g * slabs_per_group + h, pl.ds(r, n // dil, stride=dil), :]
                        out_ref[0, r, dst, h * LANES:(h + 1) * LANES] = part.astype(BF16)
        for g, c_ref in enumerate((c0_ref, c1_ref, c2_ref)):
            win = A_GROUPS[g][0]
            width = min(tm, win)
            first_tile = nt - max(win // tm, 1)
            lo = max(rows.start, tm - width)
            if lo >= rows.stop:
                continue

            @pl.when(i >= first_tile)
            def _(g=g, c_ref=c_ref, lo=lo, k=k, v=v, rows=rows, width=width):
                cols = slice(g * GROUP_WIDTH, (g + 1) * GROUP_WIDTH)
                dst = slice(lo - (tm - width), rows.stop - (tm - width))
                _store_standard_rows(c_ref, k[lo - rows.start:, cols].T, dst)
                c_ref[0, GROUP_WIDTH:2 * GROUP_WIDTH, dst] = v[lo - rows.start:, cols].T


def _inproj_a_sample_kernel(x_ref, sh_ref, sc_ref, g_ref, w_ref, cos_ref, sin_ref,
                            q_ref, k_ref, v_ref, gz_ref):
    q, k, v, gz = _inproj_a_body(x_ref, sh_ref, sc_ref, g_ref, w_ref, cos_ref, sin_ref, slice(0, x_ref.shape[1]))
    q_ref[0] = _standard_lanes(q)
    k_ref[0] = _standard_lanes(k)
    v_ref[0] = v
    gz_ref[0] = gz.astype(BF16)


class _Mod(NamedTuple):
    array: jax.Array
    layer: int
    part: int
    row0: int
    per_row: bool


def _mod_spec(m, tm):
    if m.per_row:
        return pl.BlockSpec((None, 1, tm, D_MODEL), lambda b, i: (m.layer, 0, m.row0 // tm + i, m.part))
    return pl.BlockSpec((None, 1, 1, D_MODEL), lambda b, i: (m.layer, m.row0 + b, 0, m.part))


def _arrays(args):
    return [a.array if isinstance(a, _Mod) else a for a in args]


def _inproj_a(x, shift, scale, g, w, cos, sin, *, sample, job=None):
    nb, t, _ = x.shape
    tm = min(TOKEN_TILE, t)
    nt = t // tm
    row = lambda width: pl.BlockSpec((1, tm, width), lambda b, i: (b, i, 0))
    in_specs = [
        row(D_MODEL), _mod_spec(shift, tm), _mod_spec(scale, tm),
        pl.BlockSpec((1, D_MODEL), lambda b, i: (0, 0)),
        pl.BlockSpec((D_MODEL, 4 * A_WIDTH), lambda b, i: (0, 0)),
        pl.BlockSpec((tm, LANES), lambda b, i: (i, 0)),
        pl.BlockSpec((tm, LANES), lambda b, i: (i, 0)),
    ]
    if sample:
        out_shape = [jax.ShapeDtypeStruct((nb, t, A_WIDTH), dt) for dt in (F32, F32, F32, BF16)]
        return pl.pallas_call(
            _inproj_a_sample_kernel, grid=(nb, nt), in_specs=in_specs,
            out_specs=[row(A_WIDTH)] * 4, out_shape=out_shape,
            compiler_params=_ARB(2), name="inproj_a_sample",
        )(*_arrays([x, shift, scale, g, w, cos, sin]))
    out_shape, out_specs = [], []
    for _, dil in A_GROUPS:
        out_shape += [jax.ShapeDtypeStruct((nb, dil, t // dil, GROUP_WIDTH), BF16)] * 3
        out_specs += [pl.BlockSpec((1, dil, tm // dil, GROUP_WIDTH), lambda b, i: (b, 0, i, 0))] * 3
    out_shape.append(jax.ShapeDtypeStruct((nb, t, A_WIDTH), BF16))
    out_specs.append(row(A_WIDTH))
    for win, _ in A_GROUPS:
        rows = min(tm, win)
        first_tile = nt - max(win // tm, 1)
        out_shape.append(jax.ShapeDtypeStruct((nb, 2 * GROUP_WIDTH, min(win, t)), F32))
        out_specs.append(pl.BlockSpec(
            (1, 2 * GROUP_WIDTH, rows),
            lambda b, i, first_tile=first_tile: (b, 0, jnp.maximum(i - first_tile, 0))))
    args, aliases = [x, shift, scale, g, w, cos, sin], {}
    if job is not None:
        assert job.count == nb * nt * job.per_step
        j_in, j_args, j_out, j_shape, aliases = _job_operands(job, lambda b, i: b * nt + i, len(args),
                                                              len(out_specs))
        in_specs, args = in_specs + j_in, args + j_args
        out_specs, out_shape = out_specs + j_out, out_shape + j_shape
    return pl.pallas_call(
        functools.partial(_inproj_a_prompt_kernel, tm=tm, nt=nt, job=job.static if job else None),
        grid=(nb, nt), in_specs=in_specs, out_specs=out_specs, out_shape=out_shape,
        scratch_shapes=[pltpu.VMEM((tm // min(ROW_CHUNK, tm), 3, A_WIDTH // LANES, min(ROW_CHUNK, tm), LANES), F32)],
        input_output_aliases=aliases, compiler_params=_ARB(2), name="inproj_a_prompt",
    )(*_arrays(args))


def _query_minus_key():
    kj = lax.broadcasted_iota(jnp.int32, (2 * BLOCK, BLOCK), 0)
    qi = lax.broadcasted_iota(jnp.int32, (2 * BLOCK, BLOCK), 1)
    return qi - kj


def _head_attend(st, mask, vt_ext, sink):
    p, m = _head_probs(st, mask, sink)
    o_t, l = _head_finish(_dot(vt_ext, p), m, sink)
    return o_t, m, l


def _head_probs(st, mask, sink):
    st = jnp.where(mask, st, NEG)
    m = jnp.max(st, axis=0, keepdims=True)
    if sink is not None:
        m = jnp.maximum(m, sink)
    return jnp.exp2(st - m).astype(BF16), m


def _head_finish(ext, m, sink):
    l = ext[HEAD_DIM:HEAD_DIM + 1, :]
    if sink is not None:
        l = l + jnp.exp2(sink - m)
    return ext[0:HEAD_DIM, :] * (1.0 / l), l


def _values_ext(vt_prev, vt_cur, head):
    rows = slice(head * HEAD_DIM, (head + 1) * HEAD_DIM)
    ones = jnp.ones((16, 2 * BLOCK), BF16)
    return jnp.concatenate([jnp.concatenate([vt_prev[rows], vt_cur[rows]], axis=1), ones], axis=0)


def _block_diag_queries(qt, heads, kv_of_head, kv_heads):
    zeros = jnp.zeros((HALF, BLOCK), BF16)
    cols = []
    for h in range(heads):
        src = (h // 2) * LANES + (h % 2) * HALF
        lo, hi = qt[src:src + HALF], qt[src + HEAD_DIM:src + HEAD_DIM + HALF]
        kv = kv_of_head(h)
        pieces = []
        for slab in range(kv_heads // 2):
            for quarter in range(4):
                mine = slab == kv // 2 and quarter % 2 == kv % 2
                pieces.append((lo if quarter < 2 else hi) if mine else zeros)
        cols.append(jnp.concatenate(pieces, axis=0))
    return jnp.concatenate(cols, axis=1)


ATTN_BLOCKS_IN_FLIGHT = 16


def _dilated_attn_kernel(q_ref, k_ref, v_ref, *rest, job):
    job_in, (o_ref, lse_ref), job_out, (vt_ref,) = _split_job_refs(rest, job, 2)
    if job:
        _run_job(job, job_in, job_out, pl.program_id(0) * pl.num_programs(1) + pl.program_id(1))
    nres, m_rows = q_ref.shape[1], q_ref.shape[2]
    nblk = m_rows // BLOCK
    base = _query_minus_key()
    heads = A_HEADS_PER_GROUP

    def block(r, i):
        prev = jnp.maximum(i - 1, 0)
        dist = base + (i - prev) * BLOCK
        mask = (dist >= 0) & (dist <= BLOCK)
        qrows = pl.ds(pl.multiple_of(i * BLOCK, BLOCK), BLOCK)
        k = k_ref[0, r, pl.ds(pl.multiple_of(prev * BLOCK, BLOCK), 2 * BLOCK), :]
        qd = _block_diag_queries(q_ref[0, r, qrows, :].T, heads, lambda h: h, heads)
        st_all = _dot(k, qd)
        vt_cur = v_ref[0, r, qrows, :].T
        vt_ref[r, i] = vt_cur
        vt_prev = vt_ref[r, prev]
        outs, lses = [], []
        for h in range(heads):
            o_t, m, l = _head_attend(st_all[:, h * BLOCK:(h + 1) * BLOCK], mask,
                                     _values_ext(vt_prev, vt_cur, h), None)
            outs.append(o_t)
            lses.append(jnp.broadcast_to(m * LN2 + jnp.log(l), (HEAD_DIM, BLOCK)))
        o_ref[0, r, qrows, :] = jnp.concatenate(outs, axis=0).T.astype(BF16)
        lse_ref[0, r, qrows, :] = jnp.concatenate(lses, axis=0).T

    if nres * nblk <= ATTN_BLOCKS_IN_FLIGHT:
        for r in range(nres):
            for i in range(nblk):
                block(r, i)
    else:
        assert nres == 1 and nblk % ATTN_BLOCKS_IN_FLIGHT == 0

        def chunk(c, carry):
            for u in range(ATTN_BLOCKS_IN_FLIGHT):
                block(0, c * ATTN_BLOCKS_IN_FLIGHT + u)
            return carry

        lax.fori_loop(0, nblk // ATTN_BLOCKS_IN_FLIGHT, chunk, 0)


def _dilated_steps(dil):
    nres = max(1, dil // 2)
    return nres, dil // nres


def _dilated_attn(q, k, v, g, job=None):
    nb, dil, m_rows, _ = q.shape
    nres, nsteps = _dilated_steps(dil)
    blk = pl.BlockSpec((1, nres, m_rows, GROUP_WIDTH), lambda b, r: (b, r, 0, 0))
    in_specs, args, aliases = [blk, blk, blk], [q, k, v], {}
    out_specs = [blk, blk]
    out_shape = [jax.ShapeDtypeStruct(q.shape, BF16), jax.ShapeDtypeStruct(q.shape, F32)]
    if job is not None:
        assert job.count == nb * nsteps * job.per_step
        j_in, j_args, j_out, j_shape, aliases = _job_operands(job, lambda b, r: b * nsteps + r, len(args), 2)
        in_specs, args = in_specs + j_in, args + j_args
        out_specs, out_shape = out_specs + j_out, out_shape + j_shape
    return pl.pallas_call(
        functools.partial(_dilated_attn_kernel, job=job.static if job else None),
        grid=(nb, nsteps), in_specs=in_specs, out_specs=out_specs, out_shape=out_shape,
        scratch_shapes=[pltpu.VMEM((nres, m_rows // BLOCK, GROUP_WIDTH, BLOCK), BF16)],
        input_output_aliases=aliases, compiler_params=_ARB(2), name=f"dilated_attn_g{g}",
    )(*_arrays(args))


SWA_QUERY_TILE = 256


def _swa_attn_kernel(q_ref, k_ref, v_ref, sink_ref, gz_ref, x_ref, gate_ref, g_ref, w_ref, *rest, job):
    job_in, (y_ref,), job_out, (vt_ref, o_scr) = _split_job_refs(rest, job, 1)
    j = pl.program_id(1)
    nq = q_ref.shape[1] // BLOCK
    base = _query_minus_key()
    if job:
        new_cols = _run_job(job, job_in, job_out, pl.program_id(0) * pl.num_programs(1) + j, shift_now=False)

    for i in range(nq):
        cur = j * nq + i
        prev = jnp.maximum(cur - 1, 0)
        dist = base + (cur - prev) * BLOCK
        mask = (dist >= 0) & (dist < B_WINDOW)
        qrows = slice(i * BLOCK, (i + 1) * BLOCK)
        k = k_ref[0, pl.ds(pl.multiple_of(prev * BLOCK, BLOCK), 2 * BLOCK), :]
        qt = q_ref[0, qrows, :].T
        vt_cur = v_ref[0, pl.ds(pl.multiple_of(cur * BLOCK, BLOCK), BLOCK), :].T
        vt_ref[cur] = vt_cur
        vt_prev = vt_ref[prev]
        outs = []
        for kvh in range(B_KV_HEADS):
            qd = _block_diag_queries(qt[kvh * B_GROUP * HEAD_DIM:(kvh + 1) * B_GROUP * HEAD_DIM], B_GROUP,
                                     lambda h, kvh=kvh: kvh, B_KV_HEADS)
            st_all = _dot(k, qd)
            sinks = [sink_ref[kvh * B_GROUP + gq:kvh * B_GROUP + gq + 1, :] for gq in range(B_GROUP)]
            probs = [_head_probs(st_all[:, gq * BLOCK:(gq + 1) * BLOCK], mask, sinks[gq]) for gq in range(B_GROUP)]
            ext = _dot(_values_ext(vt_prev, vt_cur, kvh), jnp.concatenate([p for p, _ in probs], axis=1))
            for gq in range(B_GROUP):
                outs.append(_head_finish(ext[:, gq * BLOCK:(gq + 1) * BLOCK], probs[gq][1], sinks[gq])[0])
        o_scr[qrows, :] = jnp.concatenate(outs, axis=0).T.astype(BF16)
        if job:
            _job_shift_part(job, job_in, job_out, new_cols, i, nq)
    a = (o_scr[...].astype(F32) * gz_ref[0].astype(F32)).astype(BF16)
    _finish([a], w_ref, x_ref, gate_ref, g_ref, y_ref)


def _swa_attn(q, k, v, sinks, gz, x, gate, g, w, job=None):
    nb, t, _ = q.shape
    tq = min(SWA_QUERY_TILE, t)
    nj = t // tq
    qblk = pl.BlockSpec((1, tq, B_WIDTH), lambda b, j: (b, j, 0))
    xblk = pl.BlockSpec((1, tq, D_MODEL), lambda b, j: (b, j, 0))
    kvblk = pl.BlockSpec((1, t, B_KV_WIDTH), lambda b, j: (b, 0, 0))
    in_specs = [qblk, kvblk, kvblk, pl.BlockSpec((B_Q_HEADS, LANES), lambda b, j: (0, 0)),
                qblk, xblk, _mod_spec(gate, tq), pl.BlockSpec((1, D_MODEL), lambda b, j: (0, 0)),
                pl.BlockSpec((B_WIDTH, D_MODEL), lambda b, j: (0, 0))]
    args, aliases = [q, k, v, sinks, gz, x, gate, g, w], {}
    out_specs, out_shape = [xblk], [jax.ShapeDtypeStruct((nb, t, D_MODEL), F32)]
    if job is not None:
        assert job.count == nb * nj * job.per_step
        j_in, j_args, j_out, j_shape, aliases = _job_operands(job, lambda b, j: b * nj + j, len(args), 1)
        in_specs, args = in_specs + j_in, args + j_args
        out_specs, out_shape = out_specs + j_out, out_shape + j_shape
    return pl.pallas_call(
        functools.partial(_swa_attn_kernel, job=job.static if job else None),
        grid=(nb, nj), in_specs=in_specs, out_specs=out_specs, out_shape=out_shape,
        scratch_shapes=[pltpu.VMEM((t // BLOCK, B_KV_WIDTH, BLOCK), BF16), pltpu.VMEM((tq, B_WIDTH), BF16)],
        input_output_aliases=aliases, compiler_params=_ARB(2), name="swa_attn",
    )(*_arrays(args))


def _finish_value(a_parts, w_ref, x, gate, g):
    acc = None
    row = 0
    for a in a_parts:
        part = _dot(a, w_ref[row:row + a.shape[1], :])
        acc = part if acc is None else acc + part
        row += a.shape[1]
    return x + gate * (_normalize(acc) * g)


def _finish(a_parts, w_ref, x_ref, gate_ref, g_ref, y_ref):
    y_ref[0] = _finish_value(a_parts, w_ref, x_ref[0], gate_ref[0], g_ref[...])


def _in_row_order(ref, scr):
    dil = ref.shape[1]
    if dil == 1:
        return ref[0, 0].astype(F32)
    nslab = ref.shape[3] // LANES
    for r in range(dil):
        for h in range(nslab):
            scr[h, pl.ds(r, ref.shape[2], stride=dil), :] = ref[0, r, :, h * LANES:(h + 1) * LANES].astype(F32)
    return jnp.concatenate([scr[h] for h in range(nslab)], axis=1)


def _mixture_parts(o_refs, l_refs, gz_ref, scratch):
    os = [_in_row_order(r, s) for r, s in zip(o_refs, scratch[0:3])]
    lses = [_in_row_order(r, s) for r, s in zip(l_refs, scratch[3:6])]
    top = jnp.maximum(jnp.maximum(lses[0], lses[1]), lses[2])
    es = [jnp.exp(l - top) for l in lses]
    inv = 1.0 / (es[0] + es[1] + es[2])
    parts = []
    for g in range(len(A_GROUPS)):
        gz = gz_ref[0, :, g * GROUP_WIDTH:(g + 1) * GROUP_WIDTH].astype(F32)
        parts.append((os[g] * (es[g] * inv) * gz).astype(BF16))
    return parts


def _outproj_mix_kernel(o0_ref, o1_ref, o2_ref, l0_ref, l1_ref, l2_ref, gz_ref, x_ref, gate_ref, g_ref, w_ref,
                        y_ref, *scratch):
    parts = _mixture_parts((o0_ref, o1_ref, o2_ref), (l0_ref, l1_ref, l2_ref), gz_ref, scratch)
    _finish(parts, w_ref, x_ref, gate_ref, g_ref, y_ref)


def _outproj_kernel(o_ref, gz_ref, x_ref, gate_ref, g_ref, w_ref, y_ref):
    a = (o_ref[0].astype(F32) * gz_ref[0].astype(F32)).astype(BF16)
    _finish([a], w_ref, x_ref, gate_ref, g_ref, y_ref)


def _outproj(os, lses, gz, x, gate, g, w):
    nb, t, _ = x.shape
    tm = min(TOKEN_TILE, t)
    row = lambda width: pl.BlockSpec((1, tm, width), lambda b, i: (b, i, 0))
    width = w.shape[0]
    tail_specs = [row(width), row(D_MODEL), _mod_spec(gate, tm),
                  pl.BlockSpec((1, D_MODEL), lambda b, i: (0, 0)),
                  pl.BlockSpec((width, D_MODEL), lambda b, i: (0, 0))]
    scratch = []
    if lses is None:
        kern, name = _outproj_kernel, "outproj_b"
        in_specs = [row(width)] + tail_specs
        args = (os[0], gz, x, gate, g, w)
    else:
        kern, name = _outproj_mix_kernel, "outproj_a"
        split = lambda a: pl.BlockSpec((1, a.shape[1], tm // a.shape[1], GROUP_WIDTH), lambda b, i: (b, 0, i, 0))
        in_specs = [split(a) for a in (*os, *lses)] + tail_specs
        args = (*os, *lses, gz, x, gate, g, w)
        scratch = [pltpu.VMEM((GROUP_WIDTH // LANES, tm, LANES), F32)] * 6
    return pl.pallas_call(
        kern, grid=(nb, t // tm), in_specs=in_specs, out_specs=row(D_MODEL),
        out_shape=jax.ShapeDtypeStruct((nb, t, D_MODEL), F32), scratch_shapes=scratch,
        compiler_params=_ARB(2), name=name,
    )(*_arrays(args))


def _inproj_b_body(x, sh_ref, sc_ref, ksh_ref, ksc_ref, g_ref, gkv_ref, w_ref, wkv_ref, cos_ref, sin_ref, rows):
    xn = _normalize(x)
    hb = ((xn * g_ref[...]) * (1.0 + _mod_rows(sc_ref, rows)) + _mod_rows(sh_ref, rows)).astype(BF16)
    hk = ((xn * gkv_ref[...]) * (1.0 + _mod_rows(ksc_ref, rows)) + _mod_rows(ksh_ref, rows)).astype(BF16)
    cos, sin = cos_ref[rows, :], sin_ref[rows, :]
    q = _rope(_dot(hb, w_ref[:, 0:B_WIDTH]), cos, sin) * Q_SCALE
    gz = jax.nn.silu(_dot(hb, w_ref[:, B_WIDTH:2 * B_WIDTH]))
    kv = _dot(hk, wkv_ref[...])
    k = _rope(kv[:, 0:B_KV_WIDTH], cos, sin)
    v = kv[:, B_KV_WIDTH:2 * B_KV_WIDTH]
    return q, gz, k, v


def _mid_prompt_kernel(o0_ref, o1_ref, o2_ref, l0_ref, l1_ref, l2_ref, gza_ref, x_ref, gate_ref, gpost_ref, wo_ref,
                       sh_ref, sc_ref, ksh_ref, ksc_ref, g_ref, gkv_ref, w_ref, wkv_ref, cos_ref, sin_ref,
                       *rest, tm, nt, job):
    job_in, (x1_ref, q_ref, gz_ref, kd_ref, vd_ref, c_ref), job_out, scratch = _split_job_refs(rest, job, 6)
    if job:
        _run_job(job, job_in, job_out, pl.program_id(0) * nt + pl.program_id(1))
    parts = _mixture_parts((o0_ref, o1_ref, o2_ref), (l0_ref, l1_ref, l2_ref), gza_ref, scratch)
    x1 = _finish_value(parts, wo_ref, x_ref[0], gate_ref[0], gpost_ref[...])
    x1_ref[0] = x1
    q, gz, k, v = _inproj_b_body(x1, sh_ref, sc_ref, ksh_ref, ksc_ref, g_ref, gkv_ref, w_ref, wkv_ref,
                                 cos_ref, sin_ref, slice(0, tm))
    q_ref[0] = q.astype(BF16)
    gz_ref[0] = gz.astype(BF16)
    kd_ref[0] = k.astype(BF16)
    vd_ref[0] = v.astype(BF16)

    @pl.when(pl.program_id(1) == nt - 1)
    def _():
        _store_standard_rows(c_ref, k[tm - B_WINDOW:, :].T, slice(None))
        c_ref[0, B_KV_WIDTH:2 * B_KV_WIDTH, :] = v[tm - B_WINDOW:, :].T


def _inproj_b_sample_kernel(x_ref, sh_ref, sc_ref, ksh_ref, ksc_ref, g_ref, gkv_ref, w_ref, wkv_ref, cos_ref,
                            sin_ref, q_ref, gz_ref, k_ref, v_ref):
    q, gz, k, v = _inproj_b_body(x_ref[0], sh_ref, sc_ref, ksh_ref, ksc_ref, g_ref, gkv_ref, w_ref, wkv_ref,
                                 cos_ref, sin_ref, slice(0, x_ref.shape[1]))
    q_ref[0] = _standard_lanes(q)
    gz_ref[0] = gz.astype(BF16)
    k_ref[0] = _standard_lanes(k)
    v_ref[0] = v


def _inproj_b_sample(x, shift, scale, kshift, kscale, g, gkv, w, wkv, cos, sin):
    nb, t, _ = x.shape
    tm = min(TOKEN_TILE, t)
    row = lambda width: pl.BlockSpec((1, tm, width), lambda b, i: (b, i, 0))
    vec = pl.BlockSpec((1, D_MODEL), lambda b, i: (0, 0))
    tab = pl.BlockSpec((tm, LANES), lambda b, i: (i, 0))
    in_specs = [row(D_MODEL), _mod_spec(shift, tm), _mod_spec(scale, tm), _mod_spec(kshift, tm),
                _mod_spec(kscale, tm), vec, vec,
                pl.BlockSpec((D_MODEL, 2 * B_WIDTH), lambda b, i: (0, 0)),
                pl.BlockSpec((D_MODEL, 2 * B_KV_WIDTH), lambda b, i: (0, 0)), tab, tab]
    return pl.pallas_call(
        _inproj_b_sample_kernel, grid=(nb, t // tm), in_specs=in_specs,
        out_specs=[row(B_WIDTH), row(B_WIDTH), row(B_KV_WIDTH), row(B_KV_WIDTH)],
        out_shape=[jax.ShapeDtypeStruct((nb, t, B_WIDTH), F32), jax.ShapeDtypeStruct((nb, t, B_WIDTH), BF16),
                   jax.ShapeDtypeStruct((nb, t, B_KV_WIDTH), F32), jax.ShapeDtypeStruct((nb, t, B_KV_WIDTH), F32)],
        compiler_params=_ARB(2), name="inproj_b_sample",
    )(*_arrays([x, shift, scale, kshift, kscale, g, gkv, w, wkv, cos, sin]))


def _mid_prompt(os, lses, gza, x, gate, gpost, wo, shift, scale, kshift, kscale, g, gkv, w, wkv, cos, sin, job=None):
    nb, t, _ = x.shape
    tm = min(TOKEN_TILE, t)
    nt = t // tm
    row = lambda width: pl.BlockSpec((1, tm, width), lambda b, i: (b, i, 0))
    vec = pl.BlockSpec((1, D_MODEL), lambda b, i: (0, 0))
    tab = pl.BlockSpec((tm, LANES), lambda b, i: (i, 0))
    full = lambda a: pl.BlockSpec(a.shape, lambda b, i: (0, 0))
    split = lambda a: pl.BlockSpec((1, a.shape[1], tm // a.shape[1], GROUP_WIDTH), lambda b, i: (b, 0, i, 0))
    in_specs = ([split(a) for a in (*os, *lses)]
                + [row(A_WIDTH), row(D_MODEL), _mod_spec(gate, tm), vec, full(wo)]
                + [_mod_spec(m, tm) for m in (shift, scale, kshift, kscale)] + [vec, vec, full(w), full(wkv), tab, tab])
    args = [*os, *lses, gza, x, gate, gpost, wo, shift, scale, kshift, kscale, g, gkv, w, wkv, cos, sin]
    out_specs = [row(D_MODEL), row(B_WIDTH), row(B_WIDTH), row(B_KV_WIDTH), row(B_KV_WIDTH),
                 pl.BlockSpec((1, 2 * B_KV_WIDTH, B_WINDOW), lambda b, i: (b, 0, 0))]
    out_shape = [jax.ShapeDtypeStruct((nb, t, D_MODEL), F32),
                 jax.ShapeDtypeStruct((nb, t, B_WIDTH), BF16), jax.ShapeDtypeStruct((nb, t, B_WIDTH), BF16),
                 jax.ShapeDtypeStruct((nb, t, B_KV_WIDTH), BF16), jax.ShapeDtypeStruct((nb, t, B_KV_WIDTH), BF16),
                 jax.ShapeDtypeStruct((nb, 2 * B_KV_WIDTH, B_WINDOW), F32)]
    aliases = {}
    if job is not None:
        assert job.count == nb * nt * job.per_step
        j_in, j_args, j_out, j_shape, aliases = _job_operands(job, lambda b, i: b * nt + i, len(args),
                                                              len(out_specs))
        in_specs, args = in_specs + j_in, args + j_args
        out_specs, out_shape = out_specs + j_out, out_shape + j_shape
    return pl.pallas_call(
        functools.partial(_mid_prompt_kernel, tm=tm, nt=nt, job=job.static if job else None),
        grid=(nb, nt), in_specs=in_specs, out_specs=out_specs, out_shape=out_shape,
        scratch_shapes=[pltpu.VMEM((GROUP_WIDTH // LANES, tm, LANES), F32)] * 6,
        input_output_aliases=aliases, compiler_params=_ARB(2), name="mid_prompt",
    )(*_arrays(args))


def _eye():
    r = lax.broadcasted_iota(jnp.int32, (LANES, LANES), 0)
    c = lax.broadcasted_iota(jnp.int32, (LANES, LANES), 1)
    return r == c


def _row_to_col(row):
    eye = _eye()
    chunks = [jnp.sum(jnp.where(eye, row[:, j * LANES:(j + 1) * LANES], 0.0), axis=1, keepdims=True)
              for j in range(row.shape[1] // LANES)]
    return chunks[0] if len(chunks) == 1 else jnp.concatenate(chunks, axis=0)


def _shift_rows(cache_ref, out_ref, e, new_col, part=0, nparts=1):
    ntile = cache_ref.shape[2] // LANES
    first, last = part * ntile // nparts, (part + 1) * ntile // nparts
    if first == last:
        return
    lane = lax.broadcasted_iota(jnp.int32, (1, LANES), 1)
    nxt = pltpu.roll(cache_ref[e, :, first * LANES:(first + 1) * LANES], LANES - 1, 1)
    for j in range(first, last):
        cur = nxt
        if j + 1 < ntile:
            nxt = pltpu.roll(cache_ref[e, :, (j + 1) * LANES:(j + 2) * LANES], LANES - 1, 1)
            fill = nxt
        else:
            fill = new_col
        out_ref[e, :, j * LANES:(j + 1) * LANES] = jnp.where(lane < LANES - 1, cur, fill)


_NT = (((1,), (1,)), ((), ()))


def _sample_dilated_step(q_ref, kn_ref, vn_ref, cache_ref, out_cache_ref, o_ref, lse_ref, step, *, dil, bb,
                         shift_now=True):
    width, length = q_ref.shape[1], cache_ref.shape[2]
    new_cols = []
    sel = (lax.broadcasted_iota(jnp.int32, (8, width), 0)
           == lax.broadcasted_iota(jnp.int32, (8, width), 1) // HEAD_DIM)
    valid = lax.broadcasted_iota(jnp.int32, (1, length), 1) % dil == 0
    per_head = lambda x: jnp.sum(jnp.where(sel, x, 0.0), axis=0, keepdims=True)
    for e in range(bb):
        row = pl.ds(step * bb + e, 1)
        q, kn, vn = q_ref[row, :], kn_ref[row, :], vn_ref[row, :]
        qb = jnp.where(sel, q, 0.0)
        kt = cache_ref[e, 0:width, :].astype(BF16)
        vt = cache_ref[e, width:2 * width, :].astype(BF16)
        s = jnp.where(valid, _dot(qb.astype(BF16), kt), NEG)
        s_new = jnp.sum(qb * kn, axis=1, keepdims=True)
        m = jnp.maximum(jnp.max(s, axis=1, keepdims=True), s_new)
        p = jnp.exp2(s - m)
        p_new = jnp.exp2(s_new - m)
        l = per_head(jnp.sum(p, axis=1, keepdims=True) + p_new)
        o = lax.dot_general(p.astype(BF16), vt, _NT, preferred_element_type=F32)
        o_ref[row, :] = (per_head(o) + per_head(p_new) * vn) * (1.0 / l)
        lse_ref[row, :] = per_head(m) * LN2 + jnp.log(l)
        new_cols.append(jnp.concatenate([_row_to_col(kn), _row_to_col(vn)], axis=0))
        if shift_now:
            _shift_rows(cache_ref, out_cache_ref, e, new_cols[-1])
    return new_cols


def _sample_dilated_kernel(q_ref, kn_ref, vn_ref, cache_ref, out_cache_ref, o_ref, lse_ref, *, dil, bb):
    _sample_dilated_step(q_ref, kn_ref, vn_ref, cache_ref, out_cache_ref, o_ref, lse_ref, pl.program_id(0),
                         dil=dil, bb=bb)


class _ShiftJob(NamedTuple):
    q: jax.Array
    k_new: jax.Array
    v_new: jax.Array
    cache: jax.Array
    group: int
    first: int
    count: int
    per_step: int
    dil: int
    prev_out: Optional[jax.Array]

    @property
    def static(self):
        return (4 if self.prev_out is None else 5, self.dil, self.per_step)


def _job_operands(job, step_of, inputs_before, outputs_before):
    count, wq = job.count, GROUP_WIDTH
    _, chans, length = job.cache.shape
    first_blk = job.first // job.per_step
    rows = pl.BlockSpec((None, count, wq), lambda *ids: (0, job.first // count, job.group))
    full = pl.BlockSpec((count, wq), lambda *ids: (0, 0))
    tile = pl.BlockSpec((job.per_step, chans, length), lambda *ids: (first_blk + step_of(*ids), 0, 0))
    in_specs, args, aliases = [rows, rows, rows, tile], [job.q, job.k_new, job.v_new, job.cache], {}
    if job.prev_out is not None:
        in_specs.append(pl.BlockSpec(memory_space=pl.ANY))
        args.append(job.prev_out)
        aliases[inputs_before + 4] = outputs_before
    out_shape = [jax.ShapeDtypeStruct(job.cache.shape, F32)] + [jax.ShapeDtypeStruct((count, wq), F32)] * 2
    return in_specs, args, [tile, full, full], out_shape, aliases


def _split_job_refs(rest, job_static, n_main_out):
    n_in = job_static[0] if job_static else 0
    n_out = 3 if job_static else 0
    job_in, rest = rest[:n_in], rest[n_in:]
    return job_in, rest[:n_main_out], rest[n_main_out:n_main_out + n_out], rest[n_main_out + n_out:]


def _run_job(job_static, in_refs, out_refs, step, shift_now=True):
    _, dil, per_step = job_static
    return _sample_dilated_step(*in_refs[:4], *out_refs, step, dil=dil, bb=per_step, shift_now=shift_now)


def _job_shift_part(job_static, in_refs, out_refs, new_cols, part, nparts):
    for e in range(job_static[2]):
        _shift_rows(in_refs[3], out_refs[0], e, new_cols[e], part, nparts)


def _sample_swa_kernel(q_ref, kn_ref, vn_ref, cache_ref, sink_ref, out_cache_ref, o_ref, qexp, oexp, *, bb):
    step = pl.program_id(0)
    nb, length = q_ref.shape[0], cache_ref.shape[2]
    low = lax.broadcasted_iota(jnp.int32, (1, LANES), 1) < HEAD_DIM

    @pl.when(step == 0)
    def _():
        for hq in range(B_Q_HEADS):
            slab = q_ref[:, (hq // 2) * LANES:(hq // 2 + 1) * LANES]
            src_low, dst_low = hq % 2 == 0, hq // B_GROUP == 0
            x = slab if src_low == dst_low else pltpu.roll(slab, HEAD_DIM, 1)
            qexp[hq * nb:(hq + 1) * nb, :] = jnp.where(low if dst_low else jnp.logical_not(low), x, 0.0)

    own_half = (lax.broadcasted_iota(jnp.int32, (B_Q_HEADS, LANES), 0) // B_GROUP
                == lax.broadcasted_iota(jnp.int32, (B_Q_HEADS, LANES), 1) // HEAD_DIM)
    valid = lax.broadcasted_iota(jnp.int32, (1, length), 1) >= 1
    sink = sink_ref[:, 0:1]
    for e in range(bb):
        b = step * bb + e
        heads = pl.ds(b, B_Q_HEADS, stride=nb)
        kn, vn = kn_ref[pl.ds(b, 1), :], vn_ref[pl.ds(b, 1), :]
        qb = qexp[heads, :]
        kt = cache_ref[e, 0:B_KV_WIDTH, :].astype(BF16)
        vt = cache_ref[e, B_KV_WIDTH:2 * B_KV_WIDTH, :].astype(BF16)
        s = jnp.where(valid, _dot(qb.astype(BF16), kt), NEG)
        s_new = jnp.sum(qb * kn, axis=1, keepdims=True)
        m = jnp.maximum(jnp.maximum(jnp.max(s, axis=1, keepdims=True), s_new), sink)
        p = jnp.exp2(s - m)
        p_new = jnp.exp2(s_new - m)
        l = jnp.sum(p, axis=1, keepdims=True) + p_new + jnp.exp2(sink - m)
        o = lax.dot_general(p.astype(BF16), vt, _NT, preferred_element_type=F32)
        oexp[heads, :] = jnp.where(own_half, o + p_new * vn, 0.0) * (1.0 / l)
        _shift_rows(cache_ref, out_cache_ref, e, jnp.concatenate([_row_to_col(kn), _row_to_col(vn)], axis=0))

    @pl.when(step == pl.num_programs(0) - 1)
    def _():
        for j in range(B_Q_HEADS // 2):
            even, odd = oexp[2 * j * nb:(2 * j + 1) * nb, :], oexp[(2 * j + 1) * nb:(2 * j + 2) * nb, :]
            if 2 * j // B_GROUP == 0:
                odd = pltpu.roll(odd, HEAD_DIM, 1)
            else:
                even = pltpu.roll(even, HEAD_DIM, 1)
            o_ref[:, j * LANES:(j + 1) * LANES] = jnp.where(low, even, odd)


def _sample_attn(q, k_new, v_new, cache_t, *, dil=1, sinks=None):
    nb, chans, length = cache_t.shape
    wq, wkv = q.shape[1], k_new.shape[1]
    bb = max(1, min(8, 2048 // length))
    full = lambda w: pl.BlockSpec((nb, w), lambda s: (0, 0))
    tile = pl.BlockSpec((bb, chans, length), lambda s: (s, 0, 0))
    in_specs = [full(wq), full(wkv), full(wkv), tile]
    args = [q, k_new, v_new, cache_t]
    out_specs = [tile, full(wq)]
    out_shape = [jax.ShapeDtypeStruct(cache_t.shape, F32), jax.ShapeDtypeStruct((nb, wq), F32)]
    if sinks is None:
        kern = functools.partial(_sample_dilated_kernel, dil=dil, bb=bb)
        out_specs.append(full(wq))
        out_shape.append(jax.ShapeDtypeStruct((nb, wq), F32))
        scratch = []
    else:
        kern = functools.partial(_sample_swa_kernel, bb=bb)
        in_specs.append(pl.BlockSpec(sinks.shape, lambda s: (0, 0)))
        args.append(sinks)
        scratch = [pltpu.VMEM((B_Q_HEADS * nb, LANES), F32)] * 2
    return pl.pallas_call(
        kern, grid=(nb // bb,), in_specs=in_specs, out_specs=out_specs, out_shape=out_shape,
        scratch_shapes=scratch, compiler_params=_ARB(1),
        name=f"sample_attn_d{dil}_l{length}" if sinks is None else "sample_swa",
    )(*_arrays(args))


def _rope_tables(pos):
    half = HEAD_DIM // 2
    inv = ROPE_THETA ** (-jnp.arange(half, dtype=F32) / half)
    ang = pos.astype(F32)[:, None] * jnp.tile(inv, 4)[None, :]
    sign = jnp.where(jnp.arange(LANES) < HEAD_DIM, -1.0, 1.0).astype(F32)
    return jnp.cos(ang), jnp.sin(ang) * sign[None, :]


def _to_tiles(cache):
    nb, length = cache.shape[0], cache.shape[1]
    return jnp.transpose(cache, (0, 2, 3, 4, 1)).reshape(nb, -1, length)


def _from_tiles(tiles, heads):
    nb, _, length = tiles.shape
    return jnp.transpose(tiles.reshape(nb, 2, heads, HEAD_DIM, length), (0, 4, 1, 2, 3))


def kernel(x_prompt, x_sample, c_prompt, c_sample, cache_a_kv_g0, cache_a_kv_g1, cache_a_kv_g2, cache_b_kv, ada_w,
           ada_b, g_pre, g_post, w_in_a, w_o_a, w_in_b, w_o_b, sinks_b, ada_kv_w, ada_kv_b, g_kv, w_kv):
    nbp, t, _ = x_prompt.shape
    nbs = x_sample.shape[0]
    assert x_sample.shape[1] == 1

    c_all = jnp.concatenate([c_sample, c_prompt], axis=0)
    mod = _ada(c_all, ada_w, ada_b)
    mod_kv = _ada(c_all, ada_kv_w[None], ada_kv_b[None])

    def split(a, layer, parts, sample):
        nl, m, n = a.shape
        view = a.reshape(nl, 1, m, n) if sample else a.reshape(nl, m, 1, n)
        return [_Mod(view, layer, p, 0 if sample else nbs, sample) for p in range(parts)]

    def paired(w, ncols):
        return jnp.concatenate([_paired_columns(w[:, :ncols]), w[:, ncols:]], axis=1).astype(BF16)

    w_in_a_b, w_o_a_b = paired(w_in_a[0], 2 * A_WIDTH), w_o_a[0].astype(BF16)
    w_in_b_b, w_o_b_b, w_kv_b = paired(w_in_b[0], B_WIDTH), w_o_b[0].astype(BF16), paired(w_kv, B_KV_WIDTH)
    g_pre0, g_pre1 = g_pre[0:1], g_pre[1:2]
    g_post0, g_post1 = g_post[0:1], g_post[1:2]
    g_kv_r = g_kv[None]
    sinks = jnp.broadcast_to(sinks_b[0][:, None] * LOG2E, (B_Q_HEADS, LANES))

    xs = x_sample.reshape(1, nbs, D_MODEL)
    cos_s, sin_s = _rope_tables(jnp.full((nbs,), PAST_LEN, jnp.int32))
    sh0s, sc0s, gt0s = split(mod, 0, 3, True)
    sh1s, sc1s, gt1s = split(mod, 1, 3, True)
    kshs, kscs = split(mod_kv, 0, 2, True)
    qs, ks, vs, gzs = _inproj_a(xs, sh0s, sc0s, g_pre0, w_in_a_b, cos_s, sin_s, sample=True)
    tiles = [_to_tiles(c[0]) for c in (cache_a_kv_g0, cache_a_kv_g1, cache_a_kv_g2)]

    def job(g, first, count, per_step, prev_out=None):
        assert first % count == 0
        return _ShiftJob(qs, ks, vs, tiles[g], g, first, count, per_step, A_GROUPS[g][1], prev_out)

    proj_steps = nbp * (t // min(TOKEN_TILE, t))

    cos_p, sin_p = _rope_tables(jnp.arange(t))
    sh0, sc0, gt0 = split(mod, 0, 3, False)
    sh1, sc1, gt1 = split(mod, 1, 3, False)
    ksh, ksc = split(mod_kv, 0, 2, False)
    *qkv, gz, ca0, ca1, ca2 = _inproj_a(x_prompt, sh0, sc0, g_pre0, w_in_a_b, cos_p, sin_p, sample=False)
    g1_steps = nbp * _dilated_steps(A_GROUPS[1][1])[1]
    attn = [_dilated_attn(*qkv[0:3], 0),
            _dilated_attn(*qkv[3:6], 1, job=job(0, 0, nbs, nbs // g1_steps)),
            _dilated_attn(*qkv[6:9], 2)]
    os, lses = [a[0] for a in attn], [a[1] for a in attn]
    shifted0, o0, lse0 = attn[1][2:]
    x1, qb, gzb, kd, vd, cb, shifted1, o1, lse1 = _mid_prompt(
        os, lses, gz, x_prompt, gt0, g_post0, w_o_a_b, sh1, sc1, ksh, ksc, g_pre1, g_kv_r, w_in_b_b, w_kv_b,
        cos_p, sin_p, job=job(1, 0, nbs, nbs // proj_steps))
    y_prompt, shifted2, o2, lse2 = _swa_attn(qb, kd, vd, sinks, gzb, x1, gt1, g_post1, w_o_b_b,
                                             job=job(2, 0, nbs, 1))
    new_a_prompt = [_from_tiles(c, A_HEADS_PER_GROUP)[None] for c in (ca0, ca1, ca2)]
    new_b_prompt = _from_tiles(cb, B_KV_HEADS)

    os = [o0, o1, o2]
    lses = [lse0, lse1, lse2]
    os = [o.astype(BF16)[None, None] for o in os]
    lses = [l[None, None] for l in lses]
    new_a_sample = [_from_tiles(s, A_HEADS_PER_GROUP)[None] for s in (shifted0, shifted1, shifted2)]
    sh1, sc1, gt1, ksh, ksc, gz, gt0 = sh1s, sc1s, gt1s, kshs, kscs, gzs, gt0s
    xs1 = _outproj(os, lses, gz, xs, gt0, g_post0, w_o_a_b)
    qb, gzb, kb, vb = _inproj_b_sample(xs1, sh1, sc1, ksh, ksc, g_pre1, g_kv_r, w_in_b_b, w_kv_b, cos_s, sin_s)
    shifted_b, ob = _sample_attn(qb[0], kb[0], vb[0], _to_tiles(cache_b_kv), sinks=sinks)
    y_sample = _outproj([ob.astype(BF16)[None]], None, gzb, xs1, gt1, g_post1, w_o_b_b).reshape(nbs, 1, D_MODEL)
    new_b_sample = _from_tiles(shifted_b, B_KV_HEADS)

    return (y_prompt, y_sample, *new_a_prompt, new_b_prompt, *new_a_sample, new_b_sample)
```
